```python
import math
import jax, jax.numpy as jnp
from jax import lax
import numpy as np

D_MODEL = 1024
BATCH = 4
SEQ = 4096
DEPTH = 2

GRID_W = 64
CTX_LEN = 256
EPS = 1e-6
POOL_WINDOWS = (2, 4, 8, 16)
POOL_GROUP = D_MODEL // 16
POOL_WIDTH = len(POOL_WINDOWS) * POOL_GROUP
DIFF_HEAD_DIM = 64
DIFF_V_DIM = 2 * DIFF_HEAD_DIM
DIFF_HEADS = (D_MODEL - POOL_WIDTH) // DIFF_V_DIM
DIFF_WIDTH = DIFF_HEADS * DIFF_V_DIM
QK_WIDTH = DIFF_HEADS * 2 * DIFF_HEAD_DIM
EVEN_IN_WIDTH = POOL_WIDTH + 2 * QK_WIDTH + DIFF_WIDTH
SPLIT_IDX = [POOL_WIDTH, POOL_WIDTH + QK_WIDTH, POOL_WIDTH + 2 * QK_WIDTH]
KV_START = POOL_WIDTH + QK_WIDTH
Q_BLOCK = 128
ROPE_THETA = 10000.0
ROPE_AXIS_PAIRS = DIFF_HEAD_DIM // 4
CHUNK = 128
SG_WIDTH = 2 * D_MODEL
SG_GROUPS = 8
SG_GROUP_DIM = SG_WIDTH // SG_GROUPS
FFN_HIDDEN = 4 * D_MODEL

kernel_name = 'hybrid_pool_diffattn_sgmlp_prefix_dit'


def rms_norm(x, g):
    xf = x.astype(jnp.float32)
    y = xf * lax.rsqrt(jnp.mean(xf * xf, axis=-1, keepdims=True) + EPS) * g
    return y.astype(x.dtype)


def layer_norm(x, g, b):
    xf = x.astype(jnp.float32)
    mu = jnp.mean(xf, axis=-1, keepdims=True)
    var = jnp.mean(jnp.square(xf - mu), axis=-1, keepdims=True)
    return ((xf - mu) * lax.rsqrt(var + EPS) * g + b).astype(x.dtype)


def modulate(h, shift, scale):
    return h * (1.0 + scale) + shift


def axial_rope(n_tok):
    t = jnp.arange(n_tok, dtype=jnp.int32)
    rows = (t // GRID_W).astype(jnp.float32)
    cols = (t % GRID_W).astype(jnp.float32)
    inv = ROPE_THETA ** (-jnp.arange(ROPE_AXIS_PAIRS, dtype=jnp.float32) / ROPE_AXIS_PAIRS)
    ang = jnp.concatenate([rows[:, None] * inv, cols[:, None] * inv], axis=-1)
    return jnp.cos(ang), jnp.sin(ang)


def apply_rope(x, cos, sin):
    shp = (1, cos.shape[0]) + (1,) * (x.ndim - 3) + (cos.shape[1],)
    cs, sn = cos.reshape(shp), sin.reshape(shp)
    xr = x.astype(jnp.float32).reshape(x.shape[:-1] + (-1, 2))
    x0, x1 = xr[..., 0], xr[..., 1]
    out = jnp.stack([x0 * cs - x1 * sn, x0 * sn + x1 * cs], axis=-1)
    return out.reshape(x.shape).astype(x.dtype)


def heads_qk(t, g):
    t = t.reshape(t.shape[:2] + (DIFF_HEADS, 2, DIFF_HEAD_DIM))
    return rms_norm(t, g)


def pool_mix(u, pool_w, pool_scale):
    b_, n, _ = u.shape
    uf = u.astype(jnp.float32)
    csum = jnp.concatenate([jnp.zeros((b_, 1, POOL_WIDTH), jnp.float32), jnp.cumsum(uf, axis=1)], axis=1)
    t = jnp.arange(n)
    outs = []
    for gi, w in enumerate(POOL_WINDOWS):
        lo = jnp.clip(t - w // 2, 0, n)
        hi = jnp.clip(t + w // 2, 0, n)
        sl = slice(gi * POOL_GROUP, (gi + 1) * POOL_GROUP)
        cs = csum[:, :, sl]
        mean = (cs[:, hi] - cs[:, lo]) / (hi - lo).astype(jnp.float32)[None, :, None]
        outs.append(mean - uf[:, :, sl])
    pooled = jnp.stack(outs, axis=2).astype(u.dtype)
    y = jnp.einsum('blgc,gcd->blgd', pooled, pool_w).reshape(b_, n, POOL_WIDTH)
    return y * pool_scale


def diff_attention(q, k, v, lam):
    b_, n = q.shape[:2]
    nb = n // Q_BLOCK
    qb = jnp.moveaxis(q.reshape((b_, nb, Q_BLOCK) + q.shape[2:]), 1, 0)
    scale = DIFF_HEAD_DIM ** -0.5

    def block(qi):
        s = jnp.einsum('bqhcd,bkhcd->bchqk', qi, k, preferred_element_type=jnp.float32) * scale
        p = jax.nn.softmax(s, axis=-1)
        wgt = p[:, 0] - lam * p[:, 1]
        return jnp.einsum('bhqk,bkhe->bqhe', wgt.astype(v.dtype), v)

    out = lax.map(block, qb)
    return jnp.moveaxis(out, 0, 1).reshape(b_, n, DIFF_HEADS, DIFF_V_DIM)


def even_merge(pool_in, attn, w_out, pool_w, pool_scale, sub_g, lam_init):
    attn = rms_norm(attn, sub_g) * (1.0 - lam_init)
    y = jnp.concatenate([pool_mix(pool_in, pool_w, pool_scale),
                         attn.reshape(attn.shape[:2] + (DIFF_WIDTH,))], axis=-1)
    return y @ w_out


def even_mixer(h, hc, w_in, w_out, pool_w, pool_scale, q_g, k_g, lam, sub_g, lam_init, cos, sin, ctx_out):
    b_, n, _ = h.shape
    pool_in, q, k, v = jnp.split(h @ w_in, SPLIT_IDX, axis=-1)
    q = apply_rope(heads_qk(q, q_g), cos, sin)
    k = apply_rope(heads_qk(k, k_g), cos, sin)
    v = v.reshape(b_, n, DIFF_HEADS, DIFF_V_DIM)
    if ctx_out:
        pool_c, qc, kc, vc = jnp.split(hc @ w_in, SPLIT_IDX, axis=-1)
    else:
        kc, vc = jnp.split(hc @ w_in[:, KV_START:], [QK_WIDTH], axis=-1)
    kc = heads_qk(kc, k_g)
    vc = vc.reshape(b_, -1, DIFF_HEADS, DIFF_V_DIM)
    k_all = jnp.concatenate([kc, k], axis=1)
    v_all = jnp.concatenate([vc, v], axis=1)
    y = even_merge(pool_in, diff_attention(q, k_all, v_all, lam), w_out, pool_w, pool_scale, sub_g, lam_init)
    yc = None
    if ctx_out:
        attn_c = diff_attention(heads_qk(qc, q_g), kc, vc, lam)
        yc = even_merge(pool_c, attn_c, w_out, pool_w, pool_scale, sub_g, lam_init)
    return y, yc


def spatial_gating(h, w_in, ln_g, ln_b, sg_w, sg_b, w_out):
    b_, n, _ = h.shape
    u, v = jnp.split(jax.nn.gelu(h @ w_in, approximate=False), 2, axis=-1)
    v = layer_norm(v, ln_g, ln_b)
    vc = v.reshape(b_, n // CHUNK, CHUNK, SG_GROUPS, SG_GROUP_DIM)
    mixed = jnp.einsum('gmn,bkngc->bkmgc', sg_w, vc) + jnp.swapaxes(sg_b, 0, 1)[:, :, None]
    return (u * mixed.reshape(b_, n, SG_WIDTH)) @ w_out


def sq_relu_ffn(h, w1, w2):
    return jnp.square(jax.nn.relu(h @ w1)) @ w2


def setup_inputs(seed: int = 0) -> dict:
    key = jax.random.key(seed)
    ks = iter(jax.random.split(key, 32))
    D = D_MODEL
    ne, no = (DEPTH + 1) // 2, DEPTH // 2

    def nrm(shape, s):
        return jax.random.normal(next(ks), shape, jnp.float32) * s

    return {
        'x': nrm((BATCH, SEQ, D), 1.0),
        'c': nrm((BATCH, D), 1.0),
        'ctx': nrm((BATCH, CTX_LEN, D), 1.0),
        'c_ctx': nrm((D,), 1.0),
        'ada_w': nrm((DEPTH, D, 6 * D), D ** -0.5),
        'ada_b': nrm((DEPTH, 6 * D), 0.02),
        'norm_mix_g': 1.0 + nrm((DEPTH, D), 0.02),
        'norm_ffn_g': 1.0 + nrm((DEPTH, D), 0.02),
        'ffn_w1': nrm((DEPTH, D, FFN_HIDDEN), D ** -0.5),
        'ffn_w2': nrm((DEPTH, FFN_HIDDEN, D), FFN_HIDDEN ** -0.5),
        'ev_w_in': nrm((ne, D, EVEN_IN_WIDTH), D ** -0.5),
        'ev_w_out': nrm((ne, POOL_WIDTH + DIFF_WIDTH, D), (POOL_WIDTH + DIFF_WIDTH) ** -0.5),
        'pool_w': nrm((ne, len(POOL_WINDOWS), POOL_GROUP, POOL_GROUP), POOL_GROUP ** -0.5),
        'pool_scale': 1.0 + nrm((ne, POOL_WIDTH), 0.02),
        'q_norm_g': 1.0 + nrm((ne, DIFF_HEAD_DIM), 0.02),
        'k_norm_g': 1.0 + nrm((ne, DIFF_HEAD_DIM), 0.02),
        'lam_q1': nrm((ne, DIFF_HEAD_DIM), 0.1),
        'lam_k1': nrm((ne, DIFF_HEAD_DIM), 0.1),
        'lam_q2': nrm((ne, DIFF_HEAD_DIM), 0.1),
        'lam_k2': nrm((ne, DIFF_HEAD_DIM), 0.1),
        'sub_norm_g': 1.0 + nrm((ne, DIFF_V_DIM), 0.02),
        'od_w_in': nrm((no, D, 2 * SG_WIDTH), D ** -0.5),
        'sg_ln_g': 1.0 + nrm((no, SG_WIDTH), 0.02),
        'sg_ln_b': nrm((no, SG_WIDTH), 0.02),
        'sg_w': nrm((no, SG_GROUPS, CHUNK, CHUNK), CHUNK ** -0.5),
        'sg_b': 1.0 + nrm((no, SG_GROUPS, CHUNK), 0.02),
        'od_w_out': nrm((no, SG_WIDTH, D), SG_WIDTH ** -0.5),
    }


def reference(x, c, ctx, c_ctx, ada_w, ada_b, norm_mix_g, norm_ffn_g, ffn_w1, ffn_w2,
              ev_w_in, ev_w_out, pool_w, pool_scale, q_norm_g, k_norm_g,
              lam_q1, lam_k1, lam_q2, lam_k2, sub_norm_g,
              od_w_in, sg_ln_g, sg_ln_b, sg_w, sg_b, od_w_out):
    n_lat = x.shape[1]
    cos, sin = axial_rope(n_lat)
    sc = jax.nn.silu(c)
    scc = jax.nn.silu(c_ctx)
    xc = ctx
    for l in range(DEPTH):
        ctx_out = any(j % 2 == 0 for j in range(l + 1, DEPTH))
        uses_ctx = (l % 2 == 0) or ctx_out
        sh1, sc1, g1, sh2, sc2, g2 = jnp.split((sc @ ada_w[l] + ada_b[l])[:, None, :], 6, axis=-1)
        h = modulate(rms_norm(x, norm_mix_g[l]), sh1, sc1)
        hc = None
        if uses_ctx:
            csh1, csc1, cg1, csh2, csc2, cg2 = jnp.split(scc @ ada_w[l] + ada_b[l], 6)
            hc = modulate(rms_norm(xc, norm_mix_g[l]), csh1, csc1)
        if l % 2 == 0:
            e = l // 2
            lam_init = 0.8 - 0.6 * math.exp(-0.3 * l)
            lam = (jnp.exp(jnp.sum(lam_q1[e].astype(jnp.float32) * lam_k1[e].astype(jnp.float32)))
                   - jnp.exp(jnp.sum(lam_q2[e].astype(jnp.float32) * lam_k2[e].astype(jnp.float32)))
                   + lam_init)
            y, yc = even_mixer(h, hc, ev_w_in[e], ev_w_out[e], pool_w[e], pool_scale[e],
                               q_norm_g[e], k_norm_g[e], lam, sub_norm_g[e], lam_init, cos, sin, ctx_out)
        else:
            o = l // 2
            y = spatial_gating(h, od_w_in[o], sg_ln_g[o], sg_ln_b[o], sg_w[o], sg_b[o], od_w_out[o])
            yc = None
            if ctx_out:
                yc = spatial_gating(hc, od_w_in[o], sg_ln_g[o], sg_ln_b[o], sg_w[o], sg_b[o], od_w_out[o])
        x = x + g1 * y
        x = x + g2 * sq_relu_ffn(modulate(rms_norm(x, norm_ffn_g[l]), sh2, sc2), ffn_w1[l], ffn_w2[l])
        if ctx_out:
            xc = xc + cg1 * yc
            xc = xc + cg2 * sq_relu_ffn(modulate(rms_norm(xc, norm_ffn_g[l]), csh2, csc2), ffn_w1[l], ffn_w2[l])
    return x
```

```python
import functools
import math

import jax
import jax.numpy as jnp
import numpy as np
from jax import lax
from jax.experimental import pallas as pl
from jax.experimental.pallas import tpu as pltpu

F32 = jnp.float32
BF16 = jnp.bfloat16

EPS = 1e-6
GRID_W = 64
ROPE_THETA = 10000.0
POOL_WINDOWS = (2, 4, 8, 16)
POOL_GROUP = 64
POOL_WIDTH = 256
HEAD_DIM = 64
V_DIM = 128
N_HEADS = 6
QK_WIDTH = N_HEADS * 2 * HEAD_DIM
CHUNK = 128
SG_GROUPS = 8
SG_GROUP_DIM = 256

LANES = 128
SUBLANES = 8
POOL_HALO = 8
VMEM_LIMIT = 56 * 1024 * 1024

TM_PROJ = 512
TQ = 256
TM_MERGE = 512
TM_FFN = 256
TM_SG = 256


def _params(sem):
    return pltpu.CompilerParams(dimension_semantics=sem, vmem_limit_bytes=VMEM_LIMIT)


def _const_spec(shape):
    zeros = (0,) * len(shape)
    return pl.BlockSpec(shape, lambda *_: zeros, pipeline_mode=pl.Buffered(1))


def _norm_mod(x, g, shift, scale):
    ms = jnp.mean(x * x, axis=-1, keepdims=True)
    return x * lax.rsqrt(ms + EPS) * (g * (1.0 + scale)) + shift


def _ada_kernel(c_ref, w_ref, b_ref, o_ref):
    c = c_ref[...]
    sc = (c * jax.nn.sigmoid(c)).astype(BF16)
    o_ref[0] = jnp.dot(sc, w_ref[0].astype(BF16), preferred_element_type=F32) + b_ref[0]


def _ada_call(cin, ada_w, ada_b):
    depth, d, n = ada_w.shape
    tn = 1536
    return pl.pallas_call(
        _ada_kernel,
        grid=(depth, n // tn),
        in_specs=[
            pl.BlockSpec((SUBLANES, d), lambda l, j: (0, 0)),
            pl.BlockSpec((1, d, tn), lambda l, j: (l, 0, j)),
            pl.BlockSpec((1, 1, tn), lambda l, j: (l, 0, j)),
        ],
        out_specs=pl.BlockSpec((1, SUBLANES, tn), lambda l, j: (l, 0, j)),
        out_shape=jax.ShapeDtypeStruct((depth, SUBLANES, n), F32),
        compiler_params=_params(("parallel", "parallel")),
        name="ada_ln",
    )(cin, ada_w, ada_b.reshape(depth, 1, n))


def _qk_norm_rope(t, bd, g, cos, sin_signed, first_half):
    outs = []
    for j in range(t.shape[1] // 256):
        tj = t[:, 256 * j:256 * (j + 1)]
        ss = jnp.dot((tj * tj).astype(BF16), bd, preferred_element_type=F32)
        tn = tj * lax.rsqrt(ss * (1.0 / HEAD_DIM) + EPS) * g[:, 256 * j:256 * (j + 1)]
        for c in range(2):
            xc = tn[:, LANES * c:LANES * (c + 1)]
            swapped = jnp.where(first_half, pltpu.roll(xc, 96, 1), pltpu.roll(xc, 32, 1))
            outs.append(xc * cos + swapped * sin_signed)
    return jnp.concatenate(outs, axis=1)


def _inproj_kernel(x_ref, mod_ref, g_ref, w_ref, bd_ref, qg_ref, kg_ref, cos_ref, sin_ref,
                   *out_refs, with_q):
    d = x_ref.shape[-1]
    x = x_ref[0]
    mod = mod_ref[0]
    h = _norm_mod(x, g_ref[...], mod[:, 0:d], mod[:, d:2 * d]).astype(BF16)
    y = jnp.dot(h, w_ref[...], preferred_element_type=F32)
    lane = lax.broadcasted_iota(jnp.int32, (x.shape[0], LANES), 1)
    first_half = (lane % HEAD_DIM) < (HEAD_DIM // 2)
    cos = cos_ref[...]
    sin_signed = sin_ref[...]
    bd = bd_ref[...]
    if with_q:
        pool_ref, q_ref, k_ref, vt_ref = out_refs
        pool_ref[0] = y[:, :POOL_WIDTH]
        q = _qk_norm_rope(y[:, POOL_WIDTH:POOL_WIDTH + QK_WIDTH], bd, qg_ref[...], cos, sin_signed,
                          first_half)
        q_ref[0] = q.astype(BF16)
        off = POOL_WIDTH + QK_WIDTH
    else:
        k_ref, vt_ref = out_refs
        off = 0
    k = _qk_norm_rope(y[:, off:off + QK_WIDTH], bd, kg_ref[...], cos, sin_signed, first_half)
    k_ref[0] = k.astype(BF16)
    v = y[:, off + QK_WIDTH:]
    vt_ref[0, 0] = v.T.astype(BF16)


def _inproj_call(x, mods, mod_row, g, w, bd, qg, kg, cos, sin_signed, with_q):
    b, s, d = x.shape
    tm = min(TM_PROJ, s)
    nt = s // tm
    n = w.shape[1]
    dv = N_HEADS * V_DIM
    out_shape = [jax.ShapeDtypeStruct((b, s, QK_WIDTH), BF16),
                 jax.ShapeDtypeStruct((b, nt, dv, tm), BF16)]
    out_specs = [pl.BlockSpec((1, tm, QK_WIDTH), lambda bi, i: (bi, i, 0)),
                 pl.BlockSpec((1, 1, dv, tm), lambda bi, i: (bi, i, 0, 0))]
    if with_q:
        out_shape = [jax.ShapeDtypeStruct((b, s, POOL_WIDTH), F32),
                     jax.ShapeDtypeStruct((b, s, QK_WIDTH), BF16)] + out_shape
        out_specs = [pl.BlockSpec((1, tm, POOL_WIDTH), lambda bi, i: (bi, i, 0)),
                     pl.BlockSpec((1, tm, QK_WIDTH), lambda bi, i: (bi, i, 0))] + out_specs
    return pl.pallas_call(
        functools.partial(_inproj_kernel, with_q=with_q),
        grid=(b, nt),
        in_specs=[
            pl.BlockSpec((1, tm, d), lambda bi, i: (bi, i, 0)),
            pl.BlockSpec((1, 1, mods.shape[-1]), lambda bi, i: (mod_row(bi), 0, 0)),
            _const_spec((1, d)),
            _const_spec((d, n)),
            _const_spec(bd.shape),
            _const_spec(qg.shape),
            _const_spec(kg.shape),
            pl.BlockSpec((tm, LANES), lambda bi, i: (i, 0)),
            pl.BlockSpec((tm, LANES), lambda bi, i: (i, 0)),
        ],
        out_specs=out_specs,
        out_shape=out_shape,
        compiler_params=_params(("parallel", "parallel")),
        name="inproj_q" if with_q else "inproj_ctx",
    )(x, mods, g, w, bd, qg, kg, cos, sin_signed)


def _attn_kernel(lam_ref, subg_ref, q_ref, k_ref, vt_ref, kc_ref, vct_ref, o_ref, *, lam_init):
    tq = q_ref.shape[1]
    n_chunks = vt_ref.shape[1]
    tk = vt_ref.shape[-1]

    lam_rows = lam_ref[...]
    a1 = jnp.sum(lam_rows[0:1] * lam_rows[1:2], axis=1, keepdims=True)
    a2 = jnp.sum(lam_rows[2:3] * lam_rows[3:4], axis=1, keepdims=True)
    lam = jnp.exp(a1) - jnp.exp(a2) + lam_init

    q = q_ref[0]
    lane = lax.broadcasted_iota(jnp.int32, q.shape, 1)
    zero = jnp.zeros_like(q)
    qq = jnp.concatenate([jnp.where(lane < HEAD_DIM, q, zero),
                          jnp.where(lane >= HEAD_DIM, q, zero)], axis=0)

    def step(kb, vtb, carry):
        m, l, acc1, acc2 = carry
        st = lax.dot_general(kb, qq, (((1,), (1,)), ((), ())), preferred_element_type=F32)
        m_new = jnp.maximum(m, jnp.max(st, axis=0, keepdims=True))
        alpha = jnp.exp(m - m_new)
        p = jnp.exp(st - m_new)
        l = alpha * l + jnp.sum(p, axis=0, keepdims=True)
        pb = p.astype(BF16)
        acc1 = alpha[:, :tq] * acc1 + jnp.dot(vtb, pb[:, :tq], preferred_element_type=F32)
        acc2 = alpha[:, tq:] * acc2 + jnp.dot(vtb, pb[:, tq:], preferred_element_type=F32)
        return m_new, l, acc1, acc2

    carry = (jnp.full((1, 2 * tq), -jnp.inf, F32), jnp.zeros((1, 2 * tq), F32),
             jnp.zeros((V_DIM, tq), F32), jnp.zeros((V_DIM, tq), F32))
    carry = step(kc_ref[0], vct_ref[0, 0], carry)

    def body(c, carry):
        start = pl.multiple_of(c * tk, tk)
        return step(k_ref[0, pl.ds(start, tk), :], vt_ref[0, c], carry)

    _, l, acc1, acc2 = lax.fori_loop(0, n_chunks, body, carry)
    ot = acc1 / l[:, :tq] - lam * (acc2 / l[:, tq:])
    o = ot.T
    ms = jnp.mean(o * o, axis=-1, keepdims=True)
    o_ref[0] = (o * lax.rsqrt(ms + EPS) * (subg_ref[...] * (1.0 - lam_init))).astype(BF16)


def _attn_call(lam_rows, sub_g, q, k, vt, kc, vct, lam_init):
    b, s, _ = q.shape
    tq = min(TQ, s)
    n_chunks, tk = vt.shape[1], vt.shape[3]
    ctx_len = kc.shape[1]
    return pl.pallas_call(
        functools.partial(_attn_kernel, lam_init=lam_init),
        grid=(b, N_HEADS, s // tq),
        in_specs=[
            _const_spec(lam_rows.shape),
            _const_spec(sub_g.shape),
            pl.BlockSpec((1, tq, V_DIM), lambda bi, h, i: (bi, i, h)),
            pl.BlockSpec((1, s, V_DIM), lambda bi, h, i: (bi, 0, h)),
            pl.BlockSpec((1, n_chunks, V_DIM, tk), lambda bi, h, i: (bi, 0, h, 0)),
            pl.BlockSpec((1, ctx_len, V_DIM), lambda bi, h, i: (bi, 0, h)),
            pl.BlockSpec((1, 1, V_DIM, ctx_len), lambda bi, h, i: (bi, 0, h, 0)),
        ],
        out_specs=pl.BlockSpec((1, tq, V_DIM), lambda bi, h, i: (bi, i, h)),
        out_shape=jax.ShapeDtypeStruct((b, s, N_HEADS * V_DIM), BF16),
        compiler_params=_params(("parallel", "parallel", "arbitrary")),
        name="diff_attn",
    )(lam_rows, sub_g, q, k, vt, kc, vct)


def _merge_kernel(x_ref, mod_ref, pm_ref, pp_ref, pn_ref, attn_ref, pw_ref, ps_ref, wo_ref,
                  o_ref, ext_ref, *, seq):
    d = x_ref.shape[-1]
    tm = x_ref.shape[1]
    i = pl.program_id(1)
    last = pl.num_programs(1) - 1
    ext_ref[0:POOL_HALO, :] = jnp.where(i > 0, pp_ref[0], 0.0)
    ext_ref[POOL_HALO:POOL_HALO + tm, :] = pm_ref[0]
    ext_ref[POOL_HALO + tm:2 * POOL_HALO + tm, :] = jnp.where(i < last, pn_ref[0], 0.0)

    def window_sum(lo, hi, half):
        acc = None
        for off in range(lo, hi + 1):
            piece = ext_ref[pl.ds(POOL_HALO + off, tm), LANES * half:LANES * (half + 1)]
            acc = piece if acc is None else acc + piece
        return acc

    t = i * tm + lax.broadcasted_iota(jnp.int32, (tm, LANES), 0)
    lane = lax.broadcasted_iota(jnp.int32, (tm, LANES), 1)
    low_group = lane < POOL_GROUP
    pooled = []
    for half in range(2):
        w_small, w_big = POOL_WINDOWS[2 * half], POOL_WINDOWS[2 * half + 1]
        s_small = window_sum(-(w_small // 2), w_small // 2 - 1, half)
        s_big = (s_small + window_sum(-(w_big // 2), -(w_small // 2) - 1, half)
                 + window_sum(w_small // 2, w_big // 2 - 1, half))
        reach = jnp.where(low_group, w_small // 2, w_big // 2)
        cnt = jnp.minimum(t + reach, seq) - jnp.maximum(t - reach, 0)
        mean = jnp.where(low_group, s_small, s_big) / cnt.astype(F32)
        pooled.append(mean - pm_ref[0, :, LANES * half:LANES * (half + 1)])
    pooled = jnp.concatenate(pooled, axis=1).astype(BF16)
    pool_y = (jnp.dot(pooled, pw_ref[...], preferred_element_type=F32) * ps_ref[...]).astype(BF16)
    y = (jnp.dot(pool_y, wo_ref[0:POOL_WIDTH, :], preferred_element_type=F32)
         + jnp.dot(attn_ref[0], wo_ref[POOL_WIDTH:, :], preferred_element_type=F32))
    o_ref[0] = x_ref[0] + mod_ref[0][:, 2 * d:3 * d] * y


def _merge_call(x, mods, mod_row, pool_in, attn, pool_w_bd, pool_scale, w_out):
    b, s, d = x.shape
    tm = min(TM_MERGE, s)
    hb = tm // POOL_HALO
    n_halo = s // POOL_HALO
    return pl.pallas_call(
        functools.partial(_merge_kernel, seq=s),
        grid=(b, s // tm),
        in_specs=[
            pl.BlockSpec((1, tm, d), lambda bi, i: (bi, i, 0)),
            pl.BlockSpec((1, 1, mods.shape[-1]), lambda bi, i: (mod_row(bi), 0, 0)),
            pl.BlockSpec((1, tm, POOL_WIDTH), lambda bi, i: (bi, i, 0)),
            pl.BlockSpec((1, POOL_HALO, POOL_WIDTH),
                         lambda bi, i: (bi, jnp.maximum(i * hb - 1, 0), 0)),
            pl.BlockSpec((1, POOL_HALO, POOL_WIDTH),
                         lambda bi, i: (bi, jnp.minimum((i + 1) * hb, n_halo - 1), 0)),
            pl.BlockSpec((1, tm, attn.shape[-1]), lambda bi, i: (bi, i, 0)),
            _const_spec(pool_w_bd.shape),
            _const_spec(pool_scale.shape),
            _const_spec(w_out.shape),
        ],
        out_specs=pl.BlockSpec((1, tm, d), lambda bi, i: (bi, i, 0)),
        out_shape=jax.ShapeDtypeStruct((b, s, d), F32),
        scratch_shapes=[pltpu.VMEM((tm + 2 * POOL_HALO, POOL_WIDTH), F32)],
        compiler_params=_params(("parallel", "parallel")),
        name="pool_merge",
    )(x, mods, pool_in, pool_in, pool_in, attn, pool_w_bd, pool_scale, w_out)


def _ffn_kernel(x_ref, mod_ref, g_ref, w1_ref, w2_ref, o_ref):
    d = x_ref.shape[-1]
    x = x_ref[0]
    mod = mod_ref[0]
    h = _norm_mod(x, g_ref[...], mod[:, 3 * d:4 * d], mod[:, 4 * d:5 * d]).astype(BF16)
    a = jnp.maximum(jnp.dot(h, w1_ref[...], preferred_element_type=F32), 0.0)
    a = (a * a).astype(BF16)
    y = jnp.dot(a, w2_ref[...], preferred_element_type=F32)
    o_ref[0] = x + mod[:, 5 * d:6 * d] * y


def _ffn_call(x, mods, mod_row, g, w1, w2):
    b, s, d = x.shape
    tm = min(TM_FFN, s)
    return pl.pallas_call(
        _ffn_kernel,
        grid=(b, s // tm),
        in_specs=[
            pl.BlockSpec((1, tm, d), lambda bi, i: (bi, i, 0)),
            pl.BlockSpec((1, 1, mods.shape[-1]), lambda bi, i: (mod_row(bi), 0, 0)),
            _const_spec((1, d)),
            _const_spec(w1.shape),
            _const_spec(w2.shape),
        ],
        out_specs=pl.BlockSpec((1, tm, d), lambda bi, i: (bi, i, 0)),
        out_shape=jax.ShapeDtypeStruct((b, s, d), F32),
        compiler_params=_params(("parallel", "parallel")),
        name="ffn",
    )(x, mods, g, w1, w2)


def _sgmlp_kernel(x_ref, mod_ref, g_ref, win_ref, lng_ref, lnb_ref, sgw_ref, sgb_ref, wout_ref,
                  o_ref):
    d = x_ref.shape[-1]
    tm = x_ref.shape[1]
    width = win_ref.shape[1] // 2
    x = x_ref[0]
    mod = mod_ref[0]
    h = _norm_mod(x, g_ref[...], mod[:, 0:d], mod[:, d:2 * d]).astype(BF16)
    a = jnp.dot(h, win_ref[...], preferred_element_type=F32)
    a = 0.5 * a * (1.0 + lax.erf(a * (1.0 / math.sqrt(2.0))))
    u = a[:, :width]
    v = a[:, width:]
    mu = jnp.mean(v, axis=-1, keepdims=True)
    vc = v - mu
    var = jnp.mean(vc * vc, axis=-1, keepdims=True)
    vn = (vc * lax.rsqrt(var + EPS) * lng_ref[...] + lnb_ref[...]).astype(BF16)
    rows = []
    for r in range(tm // CHUNK):
        cols = [jnp.dot(sgw_ref[gi],
                        vn[r * CHUNK:(r + 1) * CHUNK, gi * SG_GROUP_DIM:(gi + 1) * SG_GROUP_DIM],
                        preferred_element_type=F32) for gi in range(SG_GROUPS)]
        rows.append(jnp.concatenate(cols, axis=1) + sgb_ref[...])
    mixed = jnp.concatenate(rows, axis=0)
    z = (u * mixed).astype(BF16)
    y = jnp.dot(z, wout_ref[...], preferred_element_type=F32)
    o_ref[0] = x + mod[:, 2 * d:3 * d] * y


def _sgmlp_call(x, mods, mod_row, g, w_in, ln_g, ln_b, sg_w, sg_b_full, w_out):
    b, s, d = x.shape
    tm = min(TM_SG, s)
    return pl.pallas_call(
        _sgmlp_kernel,
        grid=(b, s // tm),
        in_specs=[
            pl.BlockSpec((1, tm, d), lambda bi, i: (bi, i, 0)),
            pl.BlockSpec((1, 1, mods.shape[-1]), lambda bi, i: (mod_row(bi), 0, 0)),
            _const_spec((1, d)),
            _const_spec(w_in.shape),
            _const_spec(ln_g.shape),
            _const_spec(ln_b.shape),
            _const_spec(sg_w.shape),
            _const_spec(sg_b_full.shape),
            _const_spec(w_out.shape),
        ],
        out_specs=pl.BlockSpec((1, tm, d), lambda bi, i: (bi, i, 0)),
        out_shape=jax.ShapeDtypeStruct((b, s, d), F32),
        compiler_params=_params(("parallel", "parallel")),
        name="sg_mlp",
    )(x, mods, g, w_in, ln_g, ln_b, sg_w, sg_b_full, w_out)


def _rope_tables(n_tok):
    t = jnp.arange(n_tok, dtype=jnp.int32)
    rows = (t // GRID_W).astype(F32)
    cols = (t % GRID_W).astype(F32)
    pairs = HEAD_DIM // 4
    inv = ROPE_THETA ** (-jnp.arange(pairs, dtype=F32) / pairs)
    ang = jnp.concatenate([rows[:, None] * inv, cols[:, None] * inv], axis=-1)
    cos, sin = jnp.cos(ang), jnp.sin(ang)
    return (jnp.concatenate([cos, cos, cos, cos], axis=-1),
            jnp.concatenate([-sin, sin, -sin, sin], axis=-1))


def _block_diag(blocks):
    n = len(blocks)
    rows = []
    for i, blk in enumerate(blocks):
        z = jnp.zeros_like(blk)
        rows.append(jnp.concatenate([blk if j == i else z for j in range(n)], axis=1))
    return jnp.concatenate(rows, axis=0)


def kernel(x, c, ctx, c_ctx, ada_w, ada_b, norm_mix_g, norm_ffn_g, ffn_w1, ffn_w2, ev_w_in, ev_w_out,
           pool_w, pool_scale, q_norm_g, k_norm_g, lam_q1, lam_k1, lam_q2, lam_k2, sub_norm_g,
           od_w_in, sg_ln_g, sg_ln_b, sg_w, sg_b, od_w_out):
    b, s, d = x.shape
    depth = ada_w.shape[0]
    assert depth == 2 and b <= SUBLANES - 1

    cin = jnp.zeros((SUBLANES, d), F32).at[:b].set(c).at[b].set(c_ctx)
    mods = _ada_call(cin, ada_w, ada_b).reshape(depth * SUBLANES, 1, 6 * d)

    def lat_row(layer):
        return lambda bi: layer * SUBLANES + bi

    def ctx_row(layer):
        return lambda bi: layer * SUBLANES + b

    perm64 = np.concatenate([np.arange(0, HEAD_DIM, 2), np.arange(1, HEAD_DIM, 2)])
    perm = (np.arange(QK_WIDTH).reshape(-1, HEAD_DIM)[:, perm64]).reshape(-1)
    w_in = ev_w_in[0]
    w_q = w_in[:, POOL_WIDTH:POOL_WIDTH + QK_WIDTH][:, perm]
    w_k = w_in[:, POOL_WIDTH + QK_WIDTH:POOL_WIDTH + 2 * QK_WIDTH][:, perm]
    w_v = w_in[:, POOL_WIDTH + 2 * QK_WIDTH:]
    w_lat = jnp.concatenate([w_in[:, :POOL_WIDTH], w_q, w_k, w_v], axis=1).astype(BF16)
    w_ctx = jnp.concatenate([w_k, w_v], axis=1).astype(BF16)
    n_groups = QK_WIDTH // HEAD_DIM
    qg = jnp.tile(q_norm_g[0][perm64] * (HEAD_DIM ** -0.5), n_groups)[None, :]
    kg = jnp.tile(k_norm_g[0][perm64], n_groups)[None, :]
    ones = jnp.ones((HEAD_DIM, HEAD_DIM), BF16)
    bd = _block_diag([ones] * (256 // HEAD_DIM))
    cos, sin_signed = _rope_tables(s)
    ctx_len = ctx.shape[1]
    cos_ctx = jnp.ones((ctx_len, LANES), F32)
    sin_ctx = jnp.zeros((ctx_len, LANES), F32)
    g_mix0 = norm_mix_g[0][None, :]

    pool_in, q, k, vt = _inproj_call(x, mods, lat_row(0), g_mix0, w_lat, bd, qg, kg, cos, sin_signed, True)
    kc, vct = _inproj_call(ctx, mods, ctx_row(0), g_mix0, w_ctx, bd, qg, kg, cos_ctx, sin_ctx, False)

    lam_init = 0.8 - 0.6 * math.exp(-0.3 * 0)
    lam_rows = jnp.zeros((SUBLANES, LANES), F32)
    for r, vec in enumerate((lam_q1[0], lam_k1[0], lam_q2[0], lam_k2[0])):
        lam_rows = lam_rows.at[r, :HEAD_DIM].set(vec)
    attn = _attn_call(lam_rows, sub_norm_g[0][None, :], q, k, vt, kc, vct, lam_init)

    pool_w_bd = _block_diag([pool_w[0, gi] for gi in range(len(POOL_WINDOWS))]).astype(BF16)
    x = _merge_call(x, mods, lat_row(0), pool_in, attn, pool_w_bd, pool_scale[0][None, :],
                    ev_w_out[0].astype(BF16))
    x = _ffn_call(x, mods, lat_row(0), norm_ffn_g[0][None, :], ffn_w1[0].astype(BF16),
                  ffn_w2[0].astype(BF16))

    sg_b_full = jnp.repeat(sg_b[0].T, SG_GROUP_DIM, axis=1)
    x = _sgmlp_call(x, mods, lat_row(1), norm_mix_g[1][None, :], od_w_in[0].astype(BF16),
                    sg_ln_g[0][None, :], sg_ln_b[0][None, :], sg_w[0].astype(BF16), sg_b_full,
                    od_w_out[0].astype(BF16))
    x = _ffn_call(x, mods, lat_row(1), norm_ffn_g[1][None, :], ffn_w1[1].astype(BF16),
                  ffn_w2[1].astype(BF16))
    return x
```

```python
import functools
import math

import jax
import jax.numpy as jnp
import numpy as np
from jax import lax
from jax.experimental import pallas as pl
from jax.experimental.pallas import tpu as pltpu

F32 = jnp.float32
BF16 = jnp.bfloat16

EPS = 1e-6
GRID_W = 64
ROPE_THETA = 10000.0
POOL_WINDOWS = (2, 4, 8, 16)
POOL_GROUP = 64
POOL_WIDTH = 256
HEAD_DIM = 64
V_DIM = 128
N_HEADS = 6
QK_WIDTH = N_HEADS * 2 * HEAD_DIM
CHUNK = 128
SG_GROUPS = 8
SG_GROUP_DIM = 256

LANES = 128
SUBLANES = 8
POOL_HALO = 8
VMEM_LIMIT = 56 * 1024 * 1024

MAX_UNSHIFTED_SCORE = 60.0

TM_PROJ = 512
TQ = 256
TM_MERGE = 512
TM_FFN = 256
TM_SG = 256


def _params(sem):
    return pltpu.CompilerParams(dimension_semantics=sem, vmem_limit_bytes=VMEM_LIMIT)


def _const_spec(shape):
    zeros = (0,) * len(shape)
    return pl.BlockSpec(shape, lambda *_: zeros, pipeline_mode=pl.Buffered(1))


def _norm_mod(x, g, shift, scale):
    ms = jnp.mean(x * x, axis=-1, keepdims=True)
    return x * lax.rsqrt(ms + EPS) * (g * (1.0 + scale)) + shift


def _ada_kernel(c_ref, w_ref, b_ref, o_ref):
    c = c_ref[...]
    sc = (c * jax.nn.sigmoid(c)).astype(BF16)
    o_ref[0] = jnp.dot(sc, w_ref[0].astype(BF16), preferred_element_type=F32) + b_ref[0]


def _ada_call(cin, ada_w, ada_b):
    depth, d, n = ada_w.shape
    tn = 1536
    return pl.pallas_call(
        _ada_kernel,
        grid=(depth, n // tn),
        in_specs=[
            pl.BlockSpec((SUBLANES, d), lambda l, j: (0, 0)),
            pl.BlockSpec((1, d, tn), lambda l, j: (l, 0, j)),
            pl.BlockSpec((1, 1, tn), lambda l, j: (l, 0, j)),
        ],
        out_specs=pl.BlockSpec((1, SUBLANES, tn), lambda l, j: (l, 0, j)),
        out_shape=jax.ShapeDtypeStruct((depth, SUBLANES, n), F32),
        compiler_params=_params(("parallel", "parallel")),
        name="ada_ln",
    )(cin, ada_w, ada_b.reshape(depth, 1, n))


def _qk_norm_rope(t, bd, g, cos, sin_signed, first_half):
    outs = []
    for j in range(t.shape[1] // 256):
        tj = t[:, 256 * j:256 * (j + 1)]
        ss = jnp.dot((tj * tj).astype(BF16), bd, preferred_element_type=F32)
        tn = tj * lax.rsqrt(ss * (1.0 / HEAD_DIM) + EPS) * g[:, 256 * j:256 * (j + 1)]
        for c in range(2):
            xc = tn[:, LANES * c:LANES * (c + 1)]
            swapped = jnp.where(first_half, pltpu.roll(xc, 96, 1), pltpu.roll(xc, 32, 1))
            outs.append(xc * cos + swapped * sin_signed)
    return jnp.concatenate(outs, axis=1)


def _inproj_kernel(x_ref, mod_ref, g_ref, w_ref, bd_ref, qg_ref, kg_ref, cos_ref, sin_ref,
                   *out_refs, with_q):
    d = x_ref.shape[-1]
    x = x_ref[0]
    mod = mod_ref[0]
    h = _norm_mod(x, g_ref[...], mod[:, 0:d], mod[:, d:2 * d]).astype(BF16)
    y = jnp.dot(h, w_ref[...], preferred_element_type=F32)
    lane = lax.broadcasted_iota(jnp.int32, (x.shape[0], LANES), 1)
    first_half = (lane % HEAD_DIM) < (HEAD_DIM // 2)
    cos = cos_ref[...]
    sin_signed = sin_ref[...]
    bd = bd_ref[...]
    if with_q:
        pool_ref, q_ref, k_ref, vt_ref = out_refs
        pool_ref[0] = y[:, :POOL_WIDTH]
        q = _qk_norm_rope(y[:, POOL_WIDTH:POOL_WIDTH + QK_WIDTH], bd, qg_ref[...], cos, sin_signed,
                          first_half)
        q_ref[0] = q.astype(BF16)
        off = POOL_WIDTH + QK_WIDTH
    else:
        k_ref, vt_ref = out_refs
        off = 0
    k = _qk_norm_rope(y[:, off:off + QK_WIDTH], bd, kg_ref[...], cos, sin_signed, first_half)
    k_ref[0] = k.astype(BF16)
    v = y[:, off + QK_WIDTH:]
    vt_ref[0, 0] = v.T.astype(BF16)


def _inproj_call(x, mods, mod_row, g, w, bd, qg, kg, cos, sin_signed, with_q):
    b, s, d = x.shape
    tm = min(TM_PROJ, s)
    nt = s // tm
    n = w.shape[1]
    dv = N_HEADS * V_DIM
    out_shape = [jax.ShapeDtypeStruct((b, s, QK_WIDTH), BF16),
                 jax.ShapeDtypeStruct((b, nt, dv, tm), BF16)]
    out_specs = [pl.BlockSpec((1, tm, QK_WIDTH), lambda bi, i: (bi, i, 0)),
                 pl.BlockSpec((1, 1, dv, tm), lambda bi, i: (bi, i, 0, 0))]
    if with_q:
        out_shape = [jax.ShapeDtypeStruct((b, s, POOL_WIDTH), F32),
                     jax.ShapeDtypeStruct((b, s, QK_WIDTH), BF16)] + out_shape
        out_specs = [pl.BlockSpec((1, tm, POOL_WIDTH), lambda bi, i: (bi, i, 0)),
                     pl.BlockSpec((1, tm, QK_WIDTH), lambda bi, i: (bi, i, 0))] + out_specs
    return pl.pallas_call(
        functools.partial(_inproj_kernel, with_q=with_q),
        grid=(b, nt),
        in_specs=[
            pl.BlockSpec((1, tm, d), lambda bi, i: (bi, i, 0)),
            pl.BlockSpec((1, 1, mods.shape[-1]), lambda bi, i: (mod_row(bi), 0, 0)),
            _const_spec((1, d)),
            _const_spec((d, n)),
            _const_spec(bd.shape),
            _const_spec(qg.shape),
            _const_spec(kg.shape),
            pl.BlockSpec((tm, LANES), lambda bi, i: (i, 0)),
            pl.BlockSpec((tm, LANES), lambda bi, i: (i, 0)),
        ],
        out_specs=out_specs,
        out_shape=out_shape,
        compiler_params=_params(("parallel", "parallel")),
        name="inproj_q" if with_q else "inproj_ctx",
    )(x, mods, g, w, bd, qg, kg, cos, sin_signed)


def _attn_kernel(lam_ref, subg_ref, q_ref, k_ref, vt_ref, kc_ref, vct_ref, o_ref, kmax_ref, *,
                 lam_init):
    tq = q_ref.shape[1]
    n_chunks = vt_ref.shape[1]
    tk = vt_ref.shape[-1]

    @pl.when(pl.program_id(2) == 0)
    def _():
        kf = k_ref[0].astype(F32)
        kcf = kc_ref[0].astype(F32)
        kmax_ref[0] = jnp.maximum(jnp.max(jnp.sum(kf * kf, axis=1, keepdims=True)),
                                  jnp.max(jnp.sum(kcf * kcf, axis=1, keepdims=True)))

    lam_rows = lam_ref[...]
    a1 = jnp.sum(lam_rows[0:1] * lam_rows[1:2], axis=1, keepdims=True)
    a2 = jnp.sum(lam_rows[2:3] * lam_rows[3:4], axis=1, keepdims=True)
    lam = jnp.exp(a1) - jnp.exp(a2) + lam_init

    q = q_ref[0]
    lane = lax.broadcasted_iota(jnp.int32, q.shape, 1)
    zero = jnp.zeros_like(q)
    qq = jnp.concatenate([jnp.where(lane < HEAD_DIM, q, zero),
                          jnp.where(lane >= HEAD_DIM, q, zero)], axis=0)
    qf = q.astype(F32)
    qmax = jnp.max(jnp.sum(qf * qf, axis=1, keepdims=True))
    bounded = qmax * kmax_ref[0] <= MAX_UNSHIFTED_SCORE * MAX_UNSHIFTED_SCORE

    def chunks():
        yield kc_ref[0], vct_ref[0, 0]
        for c in range(n_chunks):
            yield k_ref[0, c * tk:(c + 1) * tk, :], vt_ref[0, c]

    def scores(kb):
        return lax.dot_general(kb, qq, (((1,), (1,)), ((), ())), preferred_element_type=F32)

    def finish(l, acc):
        ot = acc[:, :tq] / l[:, :tq] - lam * (acc[:, tq:] / l[:, tq:])
        o = ot.T
        ms = jnp.mean(o * o, axis=-1, keepdims=True)
        o_ref[0] = (o * lax.rsqrt(ms + EPS) * (subg_ref[...] * (1.0 - lam_init))).astype(BF16)

    @pl.when(bounded)
    def _():
        l = jnp.zeros((1, 2 * tq), F32)
        acc = jnp.zeros((V_DIM, 2 * tq), F32)
        for kb, vtb in chunks():
            p = jnp.exp2(scores(kb))
            l = l + jnp.sum(p, axis=0, keepdims=True)
            acc = acc + jnp.dot(vtb, p.astype(BF16), preferred_element_type=F32)
        finish(l, acc)

    @pl.when(jnp.logical_not(bounded))
    def _():
        m = jnp.full((1, 2 * tq), -jnp.inf, F32)
        l = jnp.zeros((1, 2 * tq), F32)
        acc = jnp.zeros((V_DIM, 2 * tq), F32)
        for kb, vtb in chunks():
            st = scores(kb)
            m_new = jnp.maximum(m, jnp.max(st, axis=0, keepdims=True))
            alpha = jnp.exp2(m - m_new)
            p = jnp.exp2(st - m_new)
            l = alpha * l + jnp.sum(p, axis=0, keepdims=True)
            acc = alpha * acc + jnp.dot(vtb, p.astype(BF16), preferred_element_type=F32)
            m = m_new
        finish(l, acc)


def _attn_call(lam_rows, sub_g, q, k, vt, kc, vct, lam_init):
    b, s, _ = q.shape
    tq = min(TQ, s)
    n_chunks, tk = vt.shape[1], vt.shape[3]
    ctx_len = kc.shape[1]
    return pl.pallas_call(
        functools.partial(_attn_kernel, lam_init=lam_init),
        grid=(b, N_HEADS, s // tq),
        in_specs=[
            _const_spec(lam_rows.shape),
            _const_spec(sub_g.shape),
            pl.BlockSpec((1, tq, V_DIM), lambda bi, h, i: (bi, i, h)),
            pl.BlockSpec((1, s, V_DIM), lambda bi, h, i: (bi, 0, h)),
            pl.BlockSpec((1, n_chunks, V_DIM, tk), lambda bi, h, i: (bi, 0, h, 0)),
            pl.BlockSpec((1, ctx_len, V_DIM), lambda bi, h, i: (bi, 0, h)),
            pl.BlockSpec((1, 1, V_DIM, ctx_len), lambda bi, h, i: (bi, 0, h, 0)),
        ],
        out_specs=pl.BlockSpec((1, tq, V_DIM), lambda bi, h, i: (bi, i, h)),
        out_shape=jax.ShapeDtypeStruct((b, s, N_HEADS * V_DIM), BF16),
        scratch_shapes=[pltpu.SMEM((1,), F32)],
        compiler_params=_params(("parallel", "parallel", "arbitrary")),
        name="diff_attn",
    )(lam_rows, sub_g, q, k, vt, kc, vct)


def _merge_kernel(x_ref, mod_ref, pm_ref, pp_ref, pn_ref, attn_ref, pw_ref, ps_ref, wo_ref,
                  o_ref, ext_ref, *, seq):
    d = x_ref.shape[-1]
    tm = x_ref.shape[1]
    i = pl.program_id(1)
    last = pl.num_programs(1) - 1
    ext_ref[0:POOL_HALO, :] = jnp.where(i > 0, pp_ref[0], 0.0)
    ext_ref[POOL_HALO:POOL_HALO + tm, :] = pm_ref[0]
    ext_ref[POOL_HALO + tm:2 * POOL_HALO + tm, :] = jnp.where(i < last, pn_ref[0], 0.0)

    def window_sum(lo, hi, half):
        acc = None
        for off in range(lo, hi + 1):
            piece = ext_ref[pl.ds(POOL_HALO + off, tm), LANES * half:LANES * (half + 1)]
            acc = piece if acc is None else acc + piece
        return acc

    t = i * tm + lax.broadcasted_iota(jnp.int32, (tm, LANES), 0)
    lane = lax.broadcasted_iota(jnp.int32, (tm, LANES), 1)
    low_group = lane < POOL_GROUP
    pooled = []
    for half in range(2):
        w_small, w_big = POOL_WINDOWS[2 * half], POOL_WINDOWS[2 * half + 1]
        s_small = window_sum(-(w_small // 2), w_small // 2 - 1, half)
        s_big = (s_small + window_sum(-(w_big // 2), -(w_small // 2) - 1, half)
                 + window_sum(w_small // 2, w_big // 2 - 1, half))
        reach = jnp.where(low_group, w_small // 2, w_big // 2)
        cnt = jnp.minimum(t + reach, seq) - jnp.maximum(t - reach, 0)
        mean = jnp.where(low_group, s_small, s_big) / cnt.astype(F32)
        pooled.append(mean - pm_ref[0, :, LANES * half:LANES * (half + 1)])
    pooled = jnp.concatenate(pooled, axis=1).astype(BF16)
    pool_y = (jnp.dot(pooled, pw_ref[...], preferred_element_type=F32) * ps_ref[...]).astype(BF16)
    y = (jnp.dot(pool_y, wo_ref[0:POOL_WIDTH, :], preferred_element_type=F32)
         + jnp.dot(attn_ref[0], wo_ref[POOL_WIDTH:, :], preferred_element_type=F32))
    o_ref[0] = x_ref[0] + mod_ref[0][:, 2 * d:3 * d] * y


def _merge_call(x, mods, mod_row, pool_in, attn, pool_w_bd, pool_scale, w_out):
    b, s, d = x.shape
    tm = min(TM_MERGE, s)
    hb = tm // POOL_HALO
    n_halo = s // POOL_HALO
    return pl.pallas_call(
        functools.partial(_merge_kernel, seq=s),
        grid=(b, s // tm),
        in_specs=[
            pl.BlockSpec((1, tm, d), lambda bi, i: (bi, i, 0)),
            pl.BlockSpec((1, 1, mods.shape[-1]), lambda bi, i: (mod_row(bi), 0, 0)),
            pl.BlockSpec((1, tm, POOL_WIDTH), lambda bi, i: (bi, i, 0)),
            pl.BlockSpec((1, POOL_HALO, POOL_WIDTH),
                         lambda bi, i: (bi, jnp.maximum(i * hb - 1, 0), 0)),
            pl.BlockSpec((1, POOL_HALO, POOL_WIDTH),
                         lambda bi, i: (bi, jnp.minimum((i + 1) * hb, n_halo - 1), 0)),
            pl.BlockSpec((1, tm, attn.shape[-1]), lambda bi, i: (bi, i, 0)),
            _const_spec(pool_w_bd.shape),
            _const_spec(pool_scale.shape),
            _const_spec(w_out.shape),
        ],
        out_specs=pl.BlockSpec((1, tm, d), lambda bi, i: (bi, i, 0)),
        out_shape=jax.ShapeDtypeStruct((b, s, d), F32),
        scratch_shapes=[pltpu.VMEM((tm + 2 * POOL_HALO, POOL_WIDTH), F32)],
        compiler_params=_params(("parallel", "parallel")),
        name="pool_merge",
    )(x, mods, pool_in, pool_in, pool_in, attn, pool_w_bd, pool_scale, w_out)


def _ffn_kernel(x_ref, mod_ref, g_ref, w1_ref, w2_ref, o_ref):
    d = x_ref.shape[-1]
    x = x_ref[0]
    mod = mod_ref[0]
    h = _norm_mod(x, g_ref[...], mod[:, 3 * d:4 * d], mod[:, 4 * d:5 * d]).astype(BF16)
    a = jnp.maximum(jnp.dot(h, w1_ref[...], preferred_element_type=F32), 0.0)
    a = (a * a).astype(BF16)
    y = jnp.dot(a, w2_ref[...], preferred_element_type=F32)
    o_ref[0] = x + mod[:, 5 * d:6 * d] * y


def _ffn_call(x, mods, mod_row, g, w1, w2):
    b, s, d = x.shape
    tm = min(TM_FFN, s)
    return pl.pallas_call(
        _ffn_kernel,
        grid=(b, s // tm),
        in_specs=[
            pl.BlockSpec((1, tm, d), lambda bi, i: (bi, i, 0)),
            pl.BlockSpec((1, 1, mods.shape[-1]), lambda bi, i: (mod_row(bi), 0, 0)),
            _const_spec((1, d)),
            _const_spec(w1.shape),
            _const_spec(w2.shape),
        ],
        out_specs=pl.BlockSpec((1, tm, d), lambda bi, i: (bi, i, 0)),
        out_shape=jax.ShapeDtypeStruct((b, s, d), F32),
        compiler_params=_params(("parallel", "parallel")),
        name="ffn",
    )(x, mods, g, w1, w2)


def _sgmlp_kernel(x_ref, mod_ref, g_ref, win_ref, lng_ref, lnb_ref, sgw_ref, sgb_ref, wout_ref,
                  o_ref):
    d = x_ref.shape[-1]
    tm = x_ref.shape[1]
    width = win_ref.shape[1] // 2
    x = x_ref[0]
    mod = mod_ref[0]
    h = _norm_mod(x, g_ref[...], mod[:, 0:d], mod[:, d:2 * d]).astype(BF16)
    a = jnp.dot(h, win_ref[...], preferred_element_type=F32)
    a = 0.5 * a * (1.0 + lax.erf(a * (1.0 / math.sqrt(2.0))))
    u = a[:, :width]
    v = a[:, width:]
    mu = jnp.mean(v, axis=-1, keepdims=True)
    vc = v - mu
    var = jnp.mean(vc * vc, axis=-1, keepdims=True)
    vn = (vc * lax.rsqrt(var + EPS) * lng_ref[...] + lnb_ref[...]).astype(BF16)
    rows = []
    for r in range(tm // CHUNK):
        cols = [jnp.dot(sgw_ref[gi],
                        vn[r * CHUNK:(r + 1) * CHUNK, gi * SG_GROUP_DIM:(gi + 1) * SG_GROUP_DIM],
                        preferred_element_type=F32) for gi in range(SG_GROUPS)]
        rows.append(jnp.concatenate(cols, axis=1) + sgb_ref[...])
    mixed = jnp.concatenate(rows, axis=0)
    z = (u * mixed).astype(BF16)
    y = jnp.dot(z, wout_ref[...], preferred_element_type=F32)
    o_ref[0] = x + mod[:, 2 * d:3 * d] * y


def _sgmlp_call(x, mods, mod_row, g, w_in, ln_g, ln_b, sg_w, sg_b_full, w_out):
    b, s, d = x.shape
    tm = min(TM_SG, s)
    return pl.pallas_call(
        _sgmlp_kernel,
        grid=(b, s // tm),
        in_specs=[
            pl.BlockSpec((1, tm, d), lambda bi, i: (bi, i, 0)),
            pl.BlockSpec((1, 1, mods.shape[-1]), lambda bi, i: (mod_row(bi), 0, 0)),
            _const_spec((1, d)),
            _const_spec(w_in.shape),
            _const_spec(ln_g.shape),
            _const_spec(ln_b.shape),
            _const_spec(sg_w.shape),
            _const_spec(sg_b_full.shape),
            _const_spec(w_out.shape),
        ],
        out_specs=pl.BlockSpec((1, tm, d), lambda bi, i: (bi, i, 0)),
        out_shape=jax.ShapeDtypeStruct((b, s, d), F32),
        compiler_params=_params(("parallel", "parallel")),
        name="sg_mlp",
    )(x, mods, g, w_in, ln_g, ln_b, sg_w, sg_b_full, w_out)


def _rope_tables(n_tok):
    t = jnp.arange(n_tok, dtype=jnp.int32)
    rows = (t // GRID_W).astype(F32)
    cols = (t % GRID_W).astype(F32)
    pairs = HEAD_DIM // 4
    inv = ROPE_THETA ** (-jnp.arange(pairs, dtype=F32) / pairs)
    ang = jnp.concatenate([rows[:, None] * inv, cols[:, None] * inv], axis=-1)
    cos, sin = jnp.cos(ang), jnp.sin(ang)
    return (jnp.concatenate([cos, cos, cos, cos], axis=-1),
            jnp.concatenate([-sin, sin, -sin, sin], axis=-1))


def _block_diag(blocks):
    n = len(blocks)
    rows = []
    for i, blk in enumerate(blocks):
        z = jnp.zeros_like(blk)
        rows.append(jnp.concatenate([blk if j == i else z for j in range(n)], axis=1))
    return jnp.concatenate(rows, axis=0)


def kernel(x, c, ctx, c_ctx, ada_w, ada_b, norm_mix_g, norm_ffn_g, ffn_w1, ffn_w2, ev_w_in, ev_w_out,
           pool_w, pool_scale, q_norm_g, k_norm_g, lam_q1, lam_k1, lam_q2, lam_k2, sub_norm_g,
           od_w_in, sg_ln_g, sg_ln_b, sg_w, sg_b, od_w_out):
    b, s, d = x.shape
    depth = ada_w.shape[0]
    assert depth == 2 and b <= SUBLANES - 1

    cin = jnp.zeros((SUBLANES, d), F32).at[:b].set(c).at[b].set(c_ctx)
    mods = _ada_call(cin, ada_w, ada_b).reshape(depth * SUBLANES, 1, 6 * d)

    def lat_row(layer):
        return lambda bi: layer * SUBLANES + bi

    def ctx_row(layer):
        return lambda bi: layer * SUBLANES + b

    perm64 = np.concatenate([np.arange(0, HEAD_DIM, 2), np.arange(1, HEAD_DIM, 2)])
    perm = (np.arange(QK_WIDTH).reshape(-1, HEAD_DIM)[:, perm64]).reshape(-1)
    w_in = ev_w_in[0]
    w_q = w_in[:, POOL_WIDTH:POOL_WIDTH + QK_WIDTH][:, perm]
    w_k = w_in[:, POOL_WIDTH + QK_WIDTH:POOL_WIDTH + 2 * QK_WIDTH][:, perm]
    w_v = w_in[:, POOL_WIDTH + 2 * QK_WIDTH:]
    w_lat = jnp.concatenate([w_in[:, :POOL_WIDTH], w_q, w_k, w_v], axis=1).astype(BF16)
    w_ctx = jnp.concatenate([w_k, w_v], axis=1).astype(BF16)
    n_groups = QK_WIDTH // HEAD_DIM
    qg = jnp.tile(q_norm_g[0][perm64] * (HEAD_DIM ** -0.5 * math.log2(math.e)), n_groups)[None, :]
    kg = jnp.tile(k_norm_g[0][perm64], n_groups)[None, :]
    ones = jnp.ones((HEAD_DIM, HEAD_DIM), BF16)
    bd = _block_diag([ones] * (256 // HEAD_DIM))
    cos, sin_signed = _rope_tables(s)
    ctx_len = ctx.shape[1]
    cos_ctx = jnp.ones((ctx_len, LANES), F32)
    sin_ctx = jnp.zeros((ctx_len, LANES), F32)
    g_mix0 = norm_mix_g[0][None, :]

    pool_in, q, k, vt = _inproj_call(x, mods, lat_row(0), g_mix0, w_lat, bd, qg, kg, cos, sin_signed, True)
    kc, vct = _inproj_call(ctx, mods, ctx_row(0), g_mix0, w_ctx, bd, qg, kg, cos_ctx, sin_ctx, False)

    lam_init = 0.8 - 0.6 * math.exp(-0.3 * 0)
    lam_rows = jnp.zeros((SUBLANES, LANES), F32)
    for r, vec in enumerate((lam_q1[0], lam_k1[0], lam_q2[0], lam_k2[0])):
        lam_rows = lam_rows.at[r, :HEAD_DIM].set(vec)
    attn = _attn_call(lam_rows, sub_norm_g[0][None, :], q, k, vt, kc, vct, lam_init)

    pool_w_bd = _block_diag([pool_w[0, gi] for gi in range(len(POOL_WINDOWS))]).astype(BF16)
    x = _merge_call(x, mods, lat_row(0), pool_in, attn, pool_w_bd, pool_scale[0][None, :],
                    ev_w_out[0].astype(BF16))
    x = _ffn_call(x, mods, lat_row(0), norm_ffn_g[0][None, :], ffn_w1[0].astype(BF16),
                  ffn_w2[0].astype(BF16))

    sg_b_full = jnp.repeat(sg_b[0].T, SG_GROUP_DIM, axis=1)
    x = _sgmlp_call(x, mods, lat_row(1), norm_mix_g[1][None, :], od_w_in[0].astype(BF16),
                    sg_ln_g[0][None, :], sg_ln_b[0][None, :], sg_w[0].astype(BF16), sg_b_full,
                    od_w_out[0].astype(BF16))
    x = _ffn_call(x, mods, lat_row(1), norm_ffn_g[1][None, :], ffn_w1[1].astype(BF16),
                  ffn_w2[1].astype(BF16))
    return x
```

```python
import functools
import math

import jax
import jax.numpy as jnp
import numpy as np
from jax import lax
from jax.experimental import pallas as pl
from jax.experimental.pallas import tpu as pltpu

F32 = jnp.float32
BF16 = jnp.bfloat16

EPS = 1e-6
GRID_W = 64
ROPE_THETA = 10000.0
POOL_WINDOWS = (2, 4, 8, 16)
POOL_GROUP = 64
POOL_WIDTH = 256
HEAD_DIM = 64
V_DIM = 128
N_HEADS = 6
QK_WIDTH = N_HEADS * 2 * HEAD_DIM
CHUNK = 128
SG_GROUPS = 8
SG_GROUP_DIM = 256

LANES = 128
SUBLANES = 8
POOL_HALO = 8
VMEM_LIMIT = 56 * 1024 * 1024

MAX_UNSHIFTED_SCORE = 60.0

TM_PROJ = 512
TQ = 256
ATTN_KEYS = 512
TM_MERGE = 512
TM_FFN = 256
TM_SG = 256


def _params(sem):
    return pltpu.CompilerParams(dimension_semantics=sem, vmem_limit_bytes=VMEM_LIMIT)


def _const_spec(shape):
    zeros = (0,) * len(shape)
    return pl.BlockSpec(shape, lambda *_: zeros, pipeline_mode=pl.Buffered(1))


def _norm_mod(x, g, shift, scale):
    ms = jnp.mean(x * x, axis=-1, keepdims=True)
    return x * lax.rsqrt(ms + EPS) * (g * (1.0 + scale)) + shift


def _ada_kernel(c_ref, w_ref, b_ref, o_ref):
    c = c_ref[...]
    sc = (c * jax.nn.sigmoid(c)).astype(BF16)
    o_ref[0] = jnp.dot(sc, w_ref[0].astype(BF16), preferred_element_type=F32) + b_ref[0]


def _ada_call(cin, ada_w, ada_b):
    depth, d, n = ada_w.shape
    tn = 1536
    return pl.pallas_call(
        _ada_kernel,
        grid=(depth, n // tn),
        in_specs=[
            pl.BlockSpec((SUBLANES, d), lambda l, j: (0, 0)),
            pl.BlockSpec((1, d, tn), lambda l, j: (l, 0, j)),
            pl.BlockSpec((1, 1, tn), lambda l, j: (l, 0, j)),
        ],
        out_specs=pl.BlockSpec((1, SUBLANES, tn), lambda l, j: (l, 0, j)),
        out_shape=jax.ShapeDtypeStruct((depth, SUBLANES, n), F32),
        compiler_params=_params(("parallel", "parallel")),
        name="ada_ln",
    )(cin, ada_w, ada_b.reshape(depth, 1, n))


def _qk_norm_rope(t, bd, g, cos, sin_signed, first_half):
    outs = []
    for j in range(t.shape[1] // 256):
        tj = t[:, 256 * j:256 * (j + 1)]
        ss = jnp.dot((tj * tj).astype(BF16), bd, preferred_element_type=F32)
        tn = tj * lax.rsqrt(ss * (1.0 / HEAD_DIM) + EPS) * g[:, 256 * j:256 * (j + 1)]
        for c in range(2):
            xc = tn[:, LANES * c:LANES * (c + 1)]
            swapped = jnp.where(first_half, pltpu.roll(xc, 96, 1), pltpu.roll(xc, 32, 1))
            outs.append(xc * cos + swapped * sin_signed)
    return jnp.concatenate(outs, axis=1)


def _inproj_kernel(x_ref, mod_ref, g_ref, w_ref, bd_ref, qg_ref, kg_ref, cos_ref, sin_ref,
                   *out_refs, with_q):
    d = x_ref.shape[-1]
    x = x_ref[0]
    mod = mod_ref[0]
    h = _norm_mod(x, g_ref[...], mod[:, 0:d], mod[:, d:2 * d]).astype(BF16)
    y = jnp.dot(h, w_ref[...], preferred_element_type=F32)
    lane = lax.broadcasted_iota(jnp.int32, (x.shape[0], LANES), 1)
    first_half = (lane % HEAD_DIM) < (HEAD_DIM // 2)
    cos = cos_ref[...]
    sin_signed = sin_ref[...]
    bd = bd_ref[...]
    if with_q:
        pool_ref, q_ref, k_ref, vt_ref = out_refs
        pool_ref[0] = y[:, :POOL_WIDTH]
        q = _qk_norm_rope(y[:, POOL_WIDTH:POOL_WIDTH + QK_WIDTH], bd, qg_ref[...], cos, sin_signed,
                          first_half)
        q_ref[0] = q.astype(BF16)
        off = POOL_WIDTH + QK_WIDTH
    else:
        k_ref, vt_ref = out_refs
        off = 0
    k = _qk_norm_rope(y[:, off:off + QK_WIDTH], bd, kg_ref[...], cos, sin_signed, first_half)
    k_ref[0] = k.astype(BF16)
    v = y[:, off + QK_WIDTH:]
    vt_ref[0, 0] = v.T.astype(BF16)


def _inproj_call(x, mods, mod_row, g, w, bd, qg, kg, cos, sin_signed, with_q):
    b, s, d = x.shape
    tm = min(TM_PROJ, s)
    nt = s // tm
    n = w.shape[1]
    dv = N_HEADS * V_DIM
    out_shape = [jax.ShapeDtypeStruct((b, s, QK_WIDTH), BF16),
                 jax.ShapeDtypeStruct((b, nt, dv, tm), BF16)]
    out_specs = [pl.BlockSpec((1, tm, QK_WIDTH), lambda bi, i: (bi, i, 0)),
                 pl.BlockSpec((1, 1, dv, tm), lambda bi, i: (bi, i, 0, 0))]
    if with_q:
        out_shape = [jax.ShapeDtypeStruct((b, s, POOL_WIDTH), F32),
                     jax.ShapeDtypeStruct((b, s, QK_WIDTH), BF16)] + out_shape
        out_specs = [pl.BlockSpec((1, tm, POOL_WIDTH), lambda bi, i: (bi, i, 0)),
                     pl.BlockSpec((1, tm, QK_WIDTH), lambda bi, i: (bi, i, 0))] + out_specs
    return pl.pallas_call(
        functools.partial(_inproj_kernel, with_q=with_q),
        grid=(b, nt),
        in_specs=[
            pl.BlockSpec((1, tm, d), lambda bi, i: (bi, i, 0)),
            pl.BlockSpec((1, 1, mods.shape[-1]), lambda bi, i: (mod_row(bi), 0, 0)),
            _const_spec((1, d)),
            _const_spec((d, n)),
            _const_spec(bd.shape),
            _const_spec(qg.shape),
            _const_spec(kg.shape),
            pl.BlockSpec((tm, LANES), lambda bi, i: (i, 0)),
            pl.BlockSpec((tm, LANES), lambda bi, i: (i, 0)),
        ],
        out_specs=out_specs,
        out_shape=out_shape,
        compiler_params=_params(("parallel", "parallel")),
        name="inproj_q" if with_q else "inproj_ctx",
    )(x, mods, g, w, bd, qg, kg, cos, sin_signed)


def _attn_kernel(bound_ref, lam_ref, subg_ref, q_ref, k_ref, vt_ref, kc_ref, vct_ref, o_ref, st_ref, *,
                 lam_init):
    tq = q_ref.shape[1]
    n_chunks = vt_ref.shape[1]
    tk = vt_ref.shape[-1]

    lam_rows = lam_ref[...]
    a1 = jnp.sum(lam_rows[0:1] * lam_rows[1:2], axis=1, keepdims=True)
    a2 = jnp.sum(lam_rows[2:3] * lam_rows[3:4], axis=1, keepdims=True)
    lam = jnp.exp(a1) - jnp.exp(a2) + lam_init

    q = q_ref[0]
    lane = lax.broadcasted_iota(jnp.int32, q.shape, 1)
    zero = jnp.zeros_like(q)
    qq = jnp.concatenate([jnp.where(lane < HEAD_DIM, q, zero),
                          jnp.where(lane >= HEAD_DIM, q, zero)], axis=0)
    bounded = bound_ref[0] <= MAX_UNSHIFTED_SCORE

    def chunks():
        for j in range(0, kc_ref.shape[1], ATTN_KEYS):
            yield kc_ref[0, j:j + ATTN_KEYS, :], vct_ref[0, 0, :, j:j + ATTN_KEYS]
        for c in range(n_chunks):
            for j in range(0, tk, ATTN_KEYS):
                yield k_ref[0, c * tk + j:c * tk + j + ATTN_KEYS, :], vt_ref[0, c, :, j:j + ATTN_KEYS]

    def scores(kb):
        return lax.dot_general(kb, qq, (((1,), (1,)), ((), ())), preferred_element_type=F32)

    def finish(l, acc):
        ot = acc[:, :tq] / l[:, :tq] - lam * (acc[:, tq:] / l[:, tq:])
        o = ot.T
        ms = jnp.mean(o * o, axis=-1, keepdims=True)
        o_ref[0] = (o * lax.rsqrt(ms + EPS) * (subg_ref[...] * (1.0 - lam_init))).astype(BF16)

    @pl.when(bounded)
    def _():
        l = jnp.zeros((1, 2 * tq), F32)
        acc = jnp.zeros((V_DIM, 2 * tq), F32)
        work = list(chunks())
        st_ref[0, :work[0][0].shape[0], :] = scores(work[0][0])
        for c, (kb, vtb) in enumerate(work):
            if c + 1 < len(work):
                nxt = work[c + 1][0]
                st_ref[(c + 1) % 2, :nxt.shape[0], :] = scores(nxt)
            p = jnp.exp2(st_ref[c % 2, :kb.shape[0], :])
            l = l + jnp.sum(p, axis=0, keepdims=True)
            acc = acc + jnp.dot(vtb, p.astype(BF16), preferred_element_type=F32)
        finish(l, acc)

    @pl.when(jnp.logical_not(bounded))
    def _():
        m = jnp.full((1, 2 * tq), -jnp.inf, F32)
        l = jnp.zeros((1, 2 * tq), F32)
        acc = jnp.zeros((V_DIM, 2 * tq), F32)
        for kb, vtb in chunks():
            st = scores(kb)
            m_new = jnp.maximum(m, jnp.max(st, axis=0, keepdims=True))
            alpha = jnp.exp2(m - m_new)
            p = jnp.exp2(st - m_new)
            l = alpha * l + jnp.sum(p, axis=0, keepdims=True)
            acc = alpha * acc + jnp.dot(vtb, p.astype(BF16), preferred_element_type=F32)
            m = m_new
        finish(l, acc)


def _score_bound(q_gain, k_gain):
    slack = 1.0 + 2.0 ** -5
    return (HEAD_DIM * slack * jnp.max(jnp.abs(q_gain)) * jnp.max(jnp.abs(k_gain))).reshape(1)


def _attn_call(score_bound, lam_rows, sub_g, q, k, vt, kc, vct, lam_init):
    b, s, _ = q.shape
    tq = min(TQ, s)
    n_chunks, tk = vt.shape[1], vt.shape[3]
    ctx_len = kc.shape[1]
    return pl.pallas_call(
        functools.partial(_attn_kernel, lam_init=lam_init),
        grid=(b, N_HEADS, s // tq),
        in_specs=[
            pl.BlockSpec(memory_space=pltpu.SMEM),
            _const_spec(lam_rows.shape),
            _const_spec(sub_g.shape),
            pl.BlockSpec((1, tq, V_DIM), lambda bi, h, i: (bi, i, h)),
            pl.BlockSpec((1, s, V_DIM), lambda bi, h, i: (bi, 0, h)),
            pl.BlockSpec((1, n_chunks, V_DIM, tk), lambda bi, h, i: (bi, 0, h, 0)),
            pl.BlockSpec((1, ctx_len, V_DIM), lambda bi, h, i: (bi, 0, h)),
            pl.BlockSpec((1, 1, V_DIM, ctx_len), lambda bi, h, i: (bi, 0, h, 0)),
        ],
        out_specs=pl.BlockSpec((1, tq, V_DIM), lambda bi, h, i: (bi, i, h)),
        out_shape=jax.ShapeDtypeStruct((b, s, N_HEADS * V_DIM), BF16),
        scratch_shapes=[pltpu.VMEM((2, ATTN_KEYS, 2 * tq), F32)],
        compiler_params=_params(("parallel", "parallel", "arbitrary")),
        name="diff_attn",
    )(score_bound, lam_rows, sub_g, q, k, vt, kc, vct)


def _merge_kernel(x_ref, mod_ref, pm_ref, pp_ref, pn_ref, attn_ref, pw_ref, ps_ref, wo_ref,
                  o_ref, ext_ref, *, seq):
    d = x_ref.shape[-1]
    tm = x_ref.shape[1]
    i = pl.program_id(1)
    last = pl.num_programs(1) - 1
    ext_ref[0:POOL_HALO, :] = jnp.where(i > 0, pp_ref[0], 0.0)
    ext_ref[POOL_HALO:POOL_HALO + tm, :] = pm_ref[0]
    ext_ref[POOL_HALO + tm:2 * POOL_HALO + tm, :] = jnp.where(i < last, pn_ref[0], 0.0)

    def window_sum(lo, hi, half):
        acc = None
        for off in range(lo, hi + 1):
            piece = ext_ref[pl.ds(POOL_HALO + off, tm), LANES * half:LANES * (half + 1)]
            acc = piece if acc is None else acc + piece
        return acc

    t = i * tm + lax.broadcasted_iota(jnp.int32, (tm, LANES), 0)
    lane = lax.broadcasted_iota(jnp.int32, (tm, LANES), 1)
    low_group = lane < POOL_GROUP
    pooled = []
    for half in range(2):
        w_small, w_big = POOL_WINDOWS[2 * half], POOL_WINDOWS[2 * half + 1]
        s_small = window_sum(-(w_small // 2), w_small // 2 - 1, half)
        s_big = (s_small + window_sum(-(w_big // 2), -(w_small // 2) - 1, half)
                 + window_sum(w_small // 2, w_big // 2 - 1, half))
        reach = jnp.where(low_group, w_small // 2, w_big // 2)
        cnt = jnp.minimum(t + reach, seq) - jnp.maximum(t - reach, 0)
        mean = jnp.where(low_group, s_small, s_big) / cnt.astype(F32)
        pooled.append(mean - pm_ref[0, :, LANES * half:LANES * (half + 1)])
    pooled = jnp.concatenate(pooled, axis=1).astype(BF16)
    pool_y = (jnp.dot(pooled, pw_ref[...], preferred_element_type=F32) * ps_ref[...]).astype(BF16)
    y = (jnp.dot(pool_y, wo_ref[0:POOL_WIDTH, :], preferred_element_type=F32)
         + jnp.dot(attn_ref[0], wo_ref[POOL_WIDTH:, :], preferred_element_type=F32))
    o_ref[0] = x_ref[0] + mod_ref[0][:, 2 * d:3 * d] * y


def _merge_call(x, mods, mod_row, pool_in, attn, pool_w_bd, pool_scale, w_out):
    b, s, d = x.shape
    tm = min(TM_MERGE, s)
    hb = tm // POOL_HALO
    n_halo = s // POOL_HALO
    return pl.pallas_call(
        functools.partial(_merge_kernel, seq=s),
        grid=(b, s // tm),
        in_specs=[
            pl.BlockSpec((1, tm, d), lambda bi, i: (bi, i, 0)),
            pl.BlockSpec((1, 1, mods.shape[-1]), lambda bi, i: (mod_row(bi), 0, 0)),
            pl.BlockSpec((1, tm, POOL_WIDTH), lambda bi, i: (bi, i, 0)),
            pl.BlockSpec((1, POOL_HALO, POOL_WIDTH),
                         lambda bi, i: (bi, jnp.maximum(i * hb - 1, 0), 0)),
            pl.BlockSpec((1, POOL_HALO, POOL_WIDTH),
                         lambda bi, i: (bi, jnp.minimum((i + 1) * hb, n_halo - 1), 0)),
            pl.BlockSpec((1, tm, attn.shape[-1]), lambda bi, i: (bi, i, 0)),
            _const_spec(pool_w_bd.shape),
            _const_spec(pool_scale.shape),
            _const_spec(w_out.shape),
        ],
        out_specs=pl.BlockSpec((1, tm, d), lambda bi, i: (bi, i, 0)),
        out_shape=jax.ShapeDtypeStruct((b, s, d), F32),
        scratch_shapes=[pltpu.VMEM((tm + 2 * POOL_HALO, POOL_WIDTH), F32)],
        compiler_params=_params(("parallel", "parallel")),
        name="pool_merge",
    )(x, mods, pool_in, pool_in, pool_in, attn, pool_w_bd, pool_scale, w_out)


def _ffn_kernel(x_ref, mod_ref, g_ref, w1_ref, w2_ref, o_ref):
    d = x_ref.shape[-1]
    x = x_ref[0]
    mod = mod_ref[0]
    h = _norm_mod(x, g_ref[...], mod[:, 3 * d:4 * d], mod[:, 4 * d:5 * d]).astype(BF16)
    a = jnp.maximum(jnp.dot(h, w1_ref[...], preferred_element_type=F32), 0.0)
    a = (a * a).astype(BF16)
    y = jnp.dot(a, w2_ref[...], preferred_element_type=F32)
    o_ref[0] = x + mod[:, 5 * d:6 * d] * y


def _ffn_call(x, mods, mod_row, g, w1, w2):
    b, s, d = x.shape
    tm = min(TM_FFN, s)
    return pl.pallas_call(
        _ffn_kernel,
        grid=(b, s // tm),
        in_specs=[
            pl.BlockSpec((1, tm, d), lambda bi, i: (bi, i, 0)),
            pl.BlockSpec((1, 1, mods.shape[-1]), lambda bi, i: (mod_row(bi), 0, 0)),
            _const_spec((1, d)),
            _const_spec(w1.shape),
            _const_spec(w2.shape),
        ],
        out_specs=pl.BlockSpec((1, tm, d), lambda bi, i: (bi, i, 0)),
        out_shape=jax.ShapeDtypeStruct((b, s, d), F32),
        compiler_params=_params(("parallel", "parallel")),
        name="ffn",
    )(x, mods, g, w1, w2)


def _sgmlp_kernel(x_ref, mod_ref, g_ref, win_ref, lng_ref, lnb_ref, sgw_ref, sgb_ref, wout_ref,
                  o_ref):
    d = x_ref.shape[-1]
    tm = x_ref.shape[1]
    width = win_ref.shape[1] // 2
    x = x_ref[0]
    mod = mod_ref[0]
    h = _norm_mod(x, g_ref[...], mod[:, 0:d], mod[:, d:2 * d]).astype(BF16)
    a = jnp.dot(h, win_ref[...], preferred_element_type=F32)
    a = 0.5 * a * (1.0 + lax.erf(a * (1.0 / math.sqrt(2.0))))
    u = a[:, :width]
    v = a[:, width:]
    mu = jnp.mean(v, axis=-1, keepdims=True)
    vc = v - mu
    var = jnp.mean(vc * vc, axis=-1, keepdims=True)
    vn = (vc * lax.rsqrt(var + EPS) * lng_ref[...] + lnb_ref[...]).astype(BF16)
    rows = []
    for r in range(tm // CHUNK):
        cols = [jnp.dot(sgw_ref[gi],
                        vn[r * CHUNK:(r + 1) * CHUNK, gi * SG_GROUP_DIM:(gi + 1) * SG_GROUP_DIM],
                        preferred_element_type=F32) for gi in range(SG_GROUPS)]
        rows.append(jnp.concatenate(cols, axis=1) + sgb_ref[...])
    mixed = jnp.concatenate(rows, axis=0)
    z = (u * mixed).astype(BF16)
    y = jnp.dot(z, wout_ref[...], preferred_element_type=F32)
    o_ref[0] = x + mod[:, 2 * d:3 * d] * y


def _sgmlp_call(x, mods, mod_row, g, w_in, ln_g, ln_b, sg_w, sg_b_full, w_out):
    b, s, d = x.shape
    tm = min(TM_SG, s)
    return pl.pallas_call(
        _sgmlp_kernel,
        grid=(b, s // tm),
        in_specs=[
            pl.BlockSpec((1, tm, d), lambda bi, i: (bi, i, 0)),
            pl.BlockSpec((1, 1, mods.shape[-1]), lambda bi, i: (mod_row(bi), 0, 0)),
            _const_spec((1, d)),
            _const_spec(w_in.shape),
            _const_spec(ln_g.shape),
            _const_spec(ln_b.shape),
            _const_spec(sg_w.shape),
            _const_spec(sg_b_full.shape),
            _const_spec(w_out.shape),
        ],
        out_specs=pl.BlockSpec((1, tm, d), lambda bi, i: (bi, i, 0)),
        out_shape=jax.ShapeDtypeStruct((b, s, d), F32),
        compiler_params=_params(("parallel", "parallel")),
        name="sg_mlp",
    )(x, mods, g, w_in, ln_g, ln_b, sg_w, sg_b_full, w_out)


def _rope_tables(n_tok):
    t = jnp.arange(n_tok, dtype=jnp.int32)
    rows = (t // GRID_W).astype(F32)
    cols = (t % GRID_W).astype(F32)
    pairs = HEAD_DIM // 4
    inv = ROPE_THETA ** (-jnp.arange(pairs, dtype=F32) / pairs)
    ang = jnp.concatenate([rows[:, None] * inv, cols[:, None] * inv], axis=-1)
    cos, sin = jnp.cos(ang), jnp.sin(ang)
    return (jnp.concatenate([cos, cos, cos, cos], axis=-1),
            jnp.concatenate([-sin, sin, -sin, sin], axis=-1))


def _block_diag(blocks):
    n = len(blocks)
    rows = []
    for i, blk in enumerate(blocks):
        z = jnp.zeros_like(blk)
        rows.append(jnp.concatenate([blk if j == i else z for j in range(n)], axis=1))
    return jnp.concatenate(rows, axis=0)


def kernel(x, c, ctx, c_ctx, ada_w, ada_b, norm_mix_g, norm_ffn_g, ffn_w1, ffn_w2, ev_w_in, ev_w_out,
           pool_w, pool_scale, q_norm_g, k_norm_g, lam_q1, lam_k1, lam_q2, lam_k2, sub_norm_g,
           od_w_in, sg_ln_g, sg_ln_b, sg_w, sg_b, od_w_out):
    b, s, d = x.shape
    depth = ada_w.shape[0]
    assert depth == 2 and b <= SUBLANES - 1

    cin = jnp.zeros((SUBLANES, d), F32).at[:b].set(c).at[b].set(c_ctx)
    mods = _ada_call(cin, ada_w, ada_b).reshape(depth * SUBLANES, 1, 6 * d)

    def lat_row(layer):
        return lambda bi: layer * SUBLANES + bi

    def ctx_row(layer):
        return lambda bi: layer * SUBLANES + b

    perm64 = np.concatenate([np.arange(0, HEAD_DIM, 2), np.arange(1, HEAD_DIM, 2)])
    perm = (np.arange(QK_WIDTH).reshape(-1, HEAD_DIM)[:, perm64]).reshape(-1)
    w_in = ev_w_in[0]
    w_q = w_in[:, POOL_WIDTH:POOL_WIDTH + QK_WIDTH][:, perm]
    w_k = w_in[:, POOL_WIDTH + QK_WIDTH:POOL_WIDTH + 2 * QK_WIDTH][:, perm]
    w_v = w_in[:, POOL_WIDTH + 2 * QK_WIDTH:]
    w_lat = jnp.concatenate([w_in[:, :POOL_WIDTH], w_q, w_k, w_v], axis=1).astype(BF16)
    w_ctx = jnp.concatenate([w_k, w_v], axis=1).astype(BF16)
    n_groups = QK_WIDTH // HEAD_DIM
    qg = jnp.tile(q_norm_g[0][perm64] * (HEAD_DIM ** -0.5 * math.log2(math.e)), n_groups)[None, :]
    kg = jnp.tile(k_norm_g[0][perm64], n_groups)[None, :]
    ones = jnp.ones((HEAD_DIM, HEAD_DIM), BF16)
    bd = _block_diag([ones] * (256 // HEAD_DIM))
    cos, sin_signed = _rope_tables(s)
    ctx_len = ctx.shape[1]
    cos_ctx = jnp.ones((ctx_len, LANES), F32)
    sin_ctx = jnp.zeros((ctx_len, LANES), F32)
    g_mix0 = norm_mix_g[0][None, :]

    pool_in, q, k, vt = _inproj_call(x, mods, lat_row(0), g_mix0, w_lat, bd, qg, kg, cos, sin_signed, True)
    kc, vct = _inproj_call(ctx, mods, ctx_row(0), g_mix0, w_ctx, bd, qg, kg, cos_ctx, sin_ctx, False)

    lam_init = 0.8 - 0.6 * math.exp(-0.3 * 0)
    lam_rows = jnp.zeros((SUBLANES, LANES), F32)
    for r, vec in enumerate((lam_q1[0], lam_k1[0], lam_q2[0], lam_k2[0])):
        lam_rows = lam_rows.at[r, :HEAD_DIM].set(vec)
    attn = _attn_call(_score_bound(qg, kg), lam_rows, sub_norm_g[0][None, :], q, k, vt, kc, vct, lam_init)

    pool_w_bd = _block_diag([pool_w[0, gi] for gi in range(len(POOL_WINDOWS))]).astype(BF16)
    x = _merge_call(x, mods, lat_row(0), pool_in, attn, pool_w_bd, pool_scale[0][None, :],
                    ev_w_out[0].astype(BF16))
    x = _ffn_call(x, mods, lat_row(0), norm_ffn_g[0][None, :], ffn_w1[0].astype(BF16),
                  ffn_w2[0].astype(BF16))

    sg_b_full = jnp.repeat(sg_b[0].T, SG_GROUP_DIM, axis=1)
    x = _sgmlp_call(x, mods, lat_row(1), norm_mix_g[1][None, :], od_w_in[0].astype(BF16),
                    sg_ln_g[0][None, :], sg_ln_b[0][None, :], sg_w[0].astype(BF16), sg_b_full,
                    od_w_out[0].astype(BF16))
    x = _ffn_call(x, mods, lat_row(1), norm_ffn_g[1][None, :], ffn_w1[1].astype(BF16),
                  ffn_w2[1].astype(BF16))
    return x
```

```python
import functools
import math

import jax
import jax.numpy as jnp
import numpy as np
from jax import lax
from jax.experimental import pallas as pl
from jax.experimental.pallas import tpu as pltpu

F32 = jnp.float32
BF16 = jnp.bfloat16

EPS = 1e-6
GRID_W = 64
ROPE_THETA = 10000.0
POOL_WINDOWS = (2, 4, 8, 16)
POOL_GROUP = 64
POOL_WIDTH = 256
HEAD_DIM = 64
V_DIM = 128
N_HEADS = 6
QK_WIDTH = N_HEADS * 2 * HEAD_DIM
CHUNK = 128
SG_GROUPS = 8
SG_GROUP_DIM = 256

LANES = 128
SUBLANES = 8
POOL_HALO = 8
VMEM_LIMIT = 56 * 1024 * 1024

MAX_UNSHIFTED_SCORE = 60.0

TM_PROJ = 512
TQ = 512
ATTN_KEYS = 512
TM_MERGE = 512
TM_FFN = 256
TM_SG = 256


def _params(sem):
    return pltpu.CompilerParams(dimension_semantics=sem, vmem_limit_bytes=VMEM_LIMIT)


def _const_spec(shape):
    zeros = (0,) * len(shape)
    return pl.BlockSpec(shape, lambda *_: zeros, pipeline_mode=pl.Buffered(1))


def _norm_mod(x, g, shift, scale):
    ms = jnp.mean(x * x, axis=-1, keepdims=True)
    return x * lax.rsqrt(ms + EPS) * (g * (1.0 + scale)) + shift


def _ada_kernel(c_ref, w_ref, b_ref, o_ref):
    c = c_ref[...]
    sc = (c * jax.nn.sigmoid(c)).astype(BF16)
    o_ref[0] = jnp.dot(sc, w_ref[0].astype(BF16), preferred_element_type=F32) + b_ref[0]


def _ada_call(cin, ada_w, ada_b):
    depth, d, n = ada_w.shape
    tn = 1536
    return pl.pallas_call(
        _ada_kernel,
        grid=(depth, n // tn),
        in_specs=[
            pl.BlockSpec((SUBLANES, d), lambda l, j: (0, 0)),
            pl.BlockSpec((1, d, tn), lambda l, j: (l, 0, j)),
            pl.BlockSpec((1, 1, tn), lambda l, j: (l, 0, j)),
        ],
        out_specs=pl.BlockSpec((1, SUBLANES, tn), lambda l, j: (l, 0, j)),
        out_shape=jax.ShapeDtypeStruct((depth, SUBLANES, n), F32),
        compiler_params=_params(("parallel", "parallel")),
        name="ada_ln",
    )(cin, ada_w, ada_b.reshape(depth, 1, n))


def _qk_norm_rope(t, bd, g, cos, sin_signed, first_half):
    outs = []
    for j in range(t.shape[1] // 256):
        tj = t[:, 256 * j:256 * (j + 1)]
        ss = jnp.dot((tj * tj).astype(BF16), bd, preferred_element_type=F32)
        tn = tj * lax.rsqrt(ss * (1.0 / HEAD_DIM) + EPS) * g[:, 256 * j:256 * (j + 1)]
        for c in range(2):
            xc = tn[:, LANES * c:LANES * (c + 1)]
            swapped = jnp.where(first_half, pltpu.roll(xc, 96, 1), pltpu.roll(xc, 32, 1))
            outs.append(xc * cos + swapped * sin_signed)
    return jnp.concatenate(outs, axis=1)


def _inproj_kernel(x_ref, mod_ref, g_ref, w_ref, bd_ref, qg_ref, kg_ref, cos_ref, sin_ref,
                   *out_refs, with_q):
    d = x_ref.shape[-1]
    x = x_ref[0]
    mod = mod_ref[0]
    h = _norm_mod(x, g_ref[...], mod[:, 0:d], mod[:, d:2 * d]).astype(BF16)
    y = jnp.dot(h, w_ref[...], preferred_element_type=F32)
    lane = lax.broadcasted_iota(jnp.int32, (x.shape[0], LANES), 1)
    first_half = (lane % HEAD_DIM) < (HEAD_DIM // 2)
    cos = cos_ref[...]
    sin_signed = sin_ref[...]
    bd = bd_ref[...]
    if with_q:
        pool_ref, q_ref, k_ref, vt_ref = out_refs
        pool_ref[0] = y[:, :POOL_WIDTH]
        q = _qk_norm_rope(y[:, POOL_WIDTH:POOL_WIDTH + QK_WIDTH], bd, qg_ref[...], cos, sin_signed,
                          first_half)
        q_ref[0] = q.astype(BF16)
        off = POOL_WIDTH + QK_WIDTH
    else:
        k_ref, vt_ref = out_refs
        off = 0
    k = _qk_norm_rope(y[:, off:off + QK_WIDTH], bd, kg_ref[...], cos, sin_signed, first_half)
    k_ref[0] = k.astype(BF16)
    v = y[:, off + QK_WIDTH:]
    vt_ref[0, 0] = v.T.astype(BF16)


def _inproj_call(x, mods, mod_row, g, w, bd, qg, kg, cos, sin_signed, with_q):
    b, s, d = x.shape
    tm = min(TM_PROJ, s)
    nt = s // tm
    n = w.shape[1]
    dv = N_HEADS * V_DIM
    out_shape = [jax.ShapeDtypeStruct((b, s, QK_WIDTH), BF16),
                 jax.ShapeDtypeStruct((b, nt, dv, tm), BF16)]
    out_specs = [pl.BlockSpec((1, tm, QK_WIDTH), lambda bi, i: (bi, i, 0)),
                 pl.BlockSpec((1, 1, dv, tm), lambda bi, i: (bi, i, 0, 0))]
    if with_q:
        out_shape = [jax.ShapeDtypeStruct((b, s, POOL_WIDTH), F32),
                     jax.ShapeDtypeStruct((b, s, QK_WIDTH), BF16)] + out_shape
        out_specs = [pl.BlockSpec((1, tm, POOL_WIDTH), lambda bi, i: (bi, i, 0)),
                     pl.BlockSpec((1, tm, QK_WIDTH), lambda bi, i: (bi, i, 0))] + out_specs
    return pl.pallas_call(
        functools.partial(_inproj_kernel, with_q=with_q),
        grid=(b, nt),
        in_specs=[
            pl.BlockSpec((1, tm, d), lambda bi, i: (bi, i, 0)),
            pl.BlockSpec((1, 1, mods.shape[-1]), lambda bi, i: (mod_row(bi), 0, 0)),
            _const_spec((1, d)),
            _const_spec((d, n)),
            _const_spec(bd.shape),
            _const_spec(qg.shape),
            _const_spec(kg.shape),
            pl.BlockSpec((tm, LANES), lambda bi, i: (i, 0)),
            pl.BlockSpec((tm, LANES), lambda bi, i: (i, 0)),
        ],
        out_specs=out_specs,
        out_shape=out_shape,
        compiler_params=_params(("parallel", "parallel")),
        name="inproj_q" if with_q else "inproj_ctx",
    )(x, mods, g, w, bd, qg, kg, cos, sin_signed)


def _attn_kernel(bound_ref, lam_ref, subg_ref, q_ref, k_ref, vt_ref, kc_ref, vct_ref, o_ref, st_ref, *,
                 lam_init):
    tq = q_ref.shape[1]
    n_chunks = vt_ref.shape[1]
    tk = vt_ref.shape[-1]

    lam_rows = lam_ref[...]
    a1 = jnp.sum(lam_rows[0:1] * lam_rows[1:2], axis=1, keepdims=True)
    a2 = jnp.sum(lam_rows[2:3] * lam_rows[3:4], axis=1, keepdims=True)
    lam = jnp.exp(a1) - jnp.exp(a2) + lam_init

    q = q_ref[0]
    lane = lax.broadcasted_iota(jnp.int32, q.shape, 1)
    zero = jnp.zeros_like(q)
    qq = jnp.concatenate([jnp.where(lane < HEAD_DIM, q, zero),
                          jnp.where(lane >= HEAD_DIM, q, zero)], axis=0)
    bounded = bound_ref[0] <= MAX_UNSHIFTED_SCORE

    def chunks():
        for j in range(0, kc_ref.shape[1], ATTN_KEYS):
            yield kc_ref[0, j:j + ATTN_KEYS, :], vct_ref[0, 0, :, j:j + ATTN_KEYS]
        for c in range(n_chunks):
            for j in range(0, tk, ATTN_KEYS):
                yield k_ref[0, c * tk + j:c * tk + j + ATTN_KEYS, :], vt_ref[0, c, :, j:j + ATTN_KEYS]

    def scores(kb):
        return lax.dot_general(kb, qq, (((1,), (1,)), ((), ())), preferred_element_type=F32)

    def finish(l, acc):
        ot = acc[:, :tq] / l[:, :tq] - lam * (acc[:, tq:] / l[:, tq:])
        o = ot.T
        ms = jnp.mean(o * o, axis=-1, keepdims=True)
        o_ref[0] = (o * lax.rsqrt(ms + EPS) * (subg_ref[...] * (1.0 - lam_init))).astype(BF16)

    @pl.when(bounded)
    def _():
        l = jnp.zeros((1, 2 * tq), F32)
        acc = jnp.zeros((V_DIM, 2 * tq), F32)
        work = list(chunks())
        st_ref[0, :work[0][0].shape[0], :] = scores(work[0][0])
        for c, (kb, vtb) in enumerate(work):
            if c + 1 < len(work):
                nxt = work[c + 1][0]
                st_ref[(c + 1) % 2, :nxt.shape[0], :] = scores(nxt)
            p = jnp.exp2(st_ref[c % 2, :kb.shape[0], :])
            l = l + jnp.sum(p, axis=0, keepdims=True)
            acc = acc + jnp.dot(vtb, p.astype(BF16), preferred_element_type=F32)
        finish(l, acc)

    @pl.when(jnp.logical_not(bounded))
    def _():
        m = jnp.full((1, 2 * tq), -jnp.inf, F32)
        l = jnp.zeros((1, 2 * tq), F32)
        acc = jnp.zeros((V_DIM, 2 * tq), F32)
        for kb, vtb in chunks():
            st = scores(kb)
            m_new = jnp.maximum(m, jnp.max(st, axis=0, keepdims=True))
            alpha = jnp.exp2(m - m_new)
            p = jnp.exp2(st - m_new)
            l = alpha * l + jnp.sum(p, axis=0, keepdims=True)
            acc = alpha * acc + jnp.dot(vtb, p.astype(BF16), preferred_element_type=F32)
            m = m_new
        finish(l, acc)


def _score_bound(q_gain, k_gain):
    slack = 1.0 + 2.0 ** -5
    return (HEAD_DIM * slack * jnp.max(jnp.abs(q_gain)) * jnp.max(jnp.abs(k_gain))).reshape(1)


def _attn_call(score_bound, lam_rows, sub_g, q, k, vt, kc, vct, lam_init):
    b, s, _ = q.shape
    tq = min(TQ, s)
    n_chunks, tk = vt.shape[1], vt.shape[3]
    ctx_len = kc.shape[1]
    return pl.pallas_call(
        functools.partial(_attn_kernel, lam_init=lam_init),
        grid=(b, N_HEADS, s // tq),
        in_specs=[
            pl.BlockSpec(memory_space=pltpu.SMEM),
            _const_spec(lam_rows.shape),
            _const_spec(sub_g.shape),
            pl.BlockSpec((1, tq, V_DIM), lambda bi, h, i: (bi, i, h)),
            pl.BlockSpec((1, s, V_DIM), lambda bi, h, i: (bi, 0, h)),
            pl.BlockSpec((1, n_chunks, V_DIM, tk), lambda bi, h, i: (bi, 0, h, 0)),
            pl.BlockSpec((1, ctx_len, V_DIM), lambda bi, h, i: (bi, 0, h)),
            pl.BlockSpec((1, 1, V_DIM, ctx_len), lambda bi, h, i: (bi, 0, h, 0)),
        ],
        out_specs=pl.BlockSpec((1, tq, V_DIM), lambda bi, h, i: (bi, i, h)),
        out_shape=jax.ShapeDtypeStruct((b, s, N_HEADS * V_DIM), BF16),
        scratch_shapes=[pltpu.VMEM((2, ATTN_KEYS, 2 * tq), F32)],
        compiler_params=_params(("parallel", "parallel", "arbitrary")),
        name="diff_attn",
    )(score_bound, lam_rows, sub_g, q, k, vt, kc, vct)


def _merge_kernel(x_ref, mod_ref, pm_ref, pp_ref, pn_ref, attn_ref, pw_ref, ps_ref, wo_ref,
                  o_ref, ext_ref, *, seq):
    d = x_ref.shape[-1]
    tm = x_ref.shape[1]
    i = pl.program_id(1)
    last = pl.num_programs(1) - 1
    ext_ref[0:POOL_HALO, :] = jnp.where(i > 0, pp_ref[0], 0.0)
    ext_ref[POOL_HALO:POOL_HALO + tm, :] = pm_ref[0]
    ext_ref[POOL_HALO + tm:2 * POOL_HALO + tm, :] = jnp.where(i < last, pn_ref[0], 0.0)

    def window_sum(lo, hi, half):
        acc = None
        for off in range(lo, hi + 1):
            piece = ext_ref[pl.ds(POOL_HALO + off, tm), LANES * half:LANES * (half + 1)]
            acc = piece if acc is None else acc + piece
        return acc

    t = i * tm + lax.broadcasted_iota(jnp.int32, (tm, LANES), 0)
    lane = lax.broadcasted_iota(jnp.int32, (tm, LANES), 1)
    low_group = lane < POOL_GROUP
    pooled = []
    for half in range(2):
        w_small, w_big = POOL_WINDOWS[2 * half], POOL_WINDOWS[2 * half + 1]
        s_small = window_sum(-(w_small // 2), w_small // 2 - 1, half)
        s_big = (s_small + window_sum(-(w_big // 2), -(w_small // 2) - 1, half)
                 + window_sum(w_small // 2, w_big // 2 - 1, half))
        reach = jnp.where(low_group, w_small // 2, w_big // 2)
        cnt = jnp.minimum(t + reach, seq) - jnp.maximum(t - reach, 0)
        mean = jnp.where(low_group, s_small, s_big) / cnt.astype(F32)
        pooled.append(mean - pm_ref[0, :, LANES * half:LANES * (half + 1)])
    pooled = jnp.concatenate(pooled, axis=1).astype(BF16)
    pool_y = (jnp.dot(pooled, pw_ref[...], preferred_element_type=F32) * ps_ref[...]).astype(BF16)
    y = (jnp.dot(pool_y, wo_ref[0:POOL_WIDTH, :], preferred_element_type=F32)
         + jnp.dot(attn_ref[0], wo_ref[POOL_WIDTH:, :], preferred_element_type=F32))
    o_ref[0] = x_ref[0] + mod_ref[0][:, 2 * d:3 * d] * y


def _merge_call(x, mods, mod_row, pool_in, attn, pool_w_bd, pool_scale, w_out):
    b, s, d = x.shape
    tm = min(TM_MERGE, s)
    hb = tm // POOL_HALO
    n_halo = s // POOL_HALO
    return pl.pallas_call(
        functools.partial(_merge_kernel, seq=s),
        grid=(b, s // tm),
        in_specs=[
            pl.BlockSpec((1, tm, d), lambda bi, i: (bi, i, 0)),
            pl.BlockSpec((1, 1, mods.shape[-1]), lambda bi, i: (mod_row(bi), 0, 0)),
            pl.BlockSpec((1, tm, POOL_WIDTH), lambda bi, i: (bi, i, 0)),
            pl.BlockSpec((1, POOL_HALO, POOL_WIDTH),
                         lambda bi, i: (bi, jnp.maximum(i * hb - 1, 0), 0)),
            pl.BlockSpec((1, POOL_HALO, POOL_WIDTH),
                         lambda bi, i: (bi, jnp.minimum((i + 1) * hb, n_halo - 1), 0)),
            pl.BlockSpec((1, tm, attn.shape[-1]), lambda bi, i: (bi, i, 0)),
            _const_spec(pool_w_bd.shape),
            _const_spec(pool_scale.shape),
            _const_spec(w_out.shape),
        ],
        out_specs=pl.BlockSpec((1, tm, d), lambda bi, i: (bi, i, 0)),
        out_shape=jax.ShapeDtypeStruct((b, s, d), F32),
        scratch_shapes=[pltpu.VMEM((tm + 2 * POOL_HALO, POOL_WIDTH), F32)],
        compiler_params=_params(("parallel", "parallel")),
        name="pool_merge",
    )(x, mods, pool_in, pool_in, pool_in, attn, pool_w_bd, pool_scale, w_out)


def _ffn_kernel(x_ref, mod_ref, g_ref, w1_ref, w2_ref, o_ref):
    d = x_ref.shape[-1]
    x = x_ref[0]
    mod = mod_ref[0]
    h = _norm_mod(x, g_ref[...], mod[:, 3 * d:4 * d], mod[:, 4 * d:5 * d]).astype(BF16)
    a = jnp.maximum(jnp.dot(h, w1_ref[...], preferred_element_type=F32), 0.0)
    a = (a * a).astype(BF16)
    y = jnp.dot(a, w2_ref[...], preferred_element_type=F32)
    o_ref[0] = x + mod[:, 5 * d:6 * d] * y


def _ffn_call(x, mods, mod_row, g, w1, w2):
    b, s, d = x.shape
    tm = min(TM_FFN, s)
    return pl.pallas_call(
        _ffn_kernel,
        grid=(b, s // tm),
        in_specs=[
            pl.BlockSpec((1, tm, d), lambda bi, i: (bi, i, 0)),
            pl.BlockSpec((1, 1, mods.shape[-1]), lambda bi, i: (mod_row(bi), 0, 0)),
            _const_spec((1, d)),
            _const_spec(w1.shape),
            _const_spec(w2.shape),
        ],
        out_specs=pl.BlockSpec((1, tm, d), lambda bi, i: (bi, i, 0)),
        out_shape=jax.ShapeDtypeStruct((b, s, d), F32),
        compiler_params=_params(("parallel", "parallel")),
        name="ffn",
    )(x, mods, g, w1, w2)


def _sgmlp_kernel(x_ref, mod_ref, g_ref, win_ref, lng_ref, lnb_ref, sgw_ref, sgb_ref, wout_ref,
                  o_ref):
    d = x_ref.shape[-1]
    tm = x_ref.shape[1]
    width = win_ref.shape[1] // 2
    x = x_ref[0]
    mod = mod_ref[0]
    h = _norm_mod(x, g_ref[...], mod[:, 0:d], mod[:, d:2 * d]).astype(BF16)
    a = jnp.dot(h, win_ref[...], preferred_element_type=F32)
    a = 0.5 * a * (1.0 + lax.erf(a * (1.0 / math.sqrt(2.0))))
    u = a[:, :width]
    v = a[:, width:]
    mu = jnp.mean(v, axis=-1, keepdims=True)
    vc = v - mu
    var = jnp.mean(vc * vc, axis=-1, keepdims=True)
    vn = (vc * lax.rsqrt(var + EPS) * lng_ref[...] + lnb_ref[...]).astype(BF16)
    rows = []
    for r in range(tm // CHUNK):
        cols = [jnp.dot(sgw_ref[gi],
                        vn[r * CHUNK:(r + 1) * CHUNK, gi * SG_GROUP_DIM:(gi + 1) * SG_GROUP_DIM],
                        preferred_element_type=F32) for gi in range(SG_GROUPS)]
        rows.append(jnp.concatenate(cols, axis=1) + sgb_ref[...])
    mixed = jnp.concatenate(rows, axis=0)
    z = (u * mixed).astype(BF16)
    y = jnp.dot(z, wout_ref[...], preferred_element_type=F32)
    o_ref[0] = x + mod[:, 2 * d:3 * d] * y


def _sgmlp_call(x, mods, mod_row, g, w_in, ln_g, ln_b, sg_w, sg_b_full, w_out):
    b, s, d = x.shape
    tm = min(TM_SG, s)
    return pl.pallas_call(
        _sgmlp_kernel,
        grid=(b, s // tm),
        in_specs=[
            pl.BlockSpec((1, tm, d), lambda bi, i: (bi, i, 0)),
            pl.BlockSpec((1, 1, mods.shape[-1]), lambda bi, i: (mod_row(bi), 0, 0)),
            _const_spec((1, d)),
            _const_spec(w_in.shape),
            _const_spec(ln_g.shape),
            _const_spec(ln_b.shape),
            _const_spec(sg_w.shape),
            _const_spec(sg_b_full.shape),
            _const_spec(w_out.shape),
        ],
        out_specs=pl.BlockSpec((1, tm, d), lambda bi, i: (bi, i, 0)),
        out_shape=jax.ShapeDtypeStruct((b, s, d), F32),
        compiler_params=_params(("parallel", "parallel")),
        name="sg_mlp",
    )(x, mods, g, w_in, ln_g, ln_b, sg_w, sg_b_full, w_out)


def _rope_tables(n_tok):
    t = jnp.arange(n_tok, dtype=jnp.int32)
    rows = (t // GRID_W).astype(F32)
    cols = (t % GRID_W).astype(F32)
    pairs = HEAD_DIM // 4
    inv = ROPE_THETA ** (-jnp.arange(pairs, dtype=F32) / pairs)
    ang = jnp.concatenate([rows[:, None] * inv, cols[:, None] * inv], axis=-1)
    cos, sin = jnp.cos(ang), jnp.sin(ang)
    return (jnp.concatenate([cos, cos, cos, cos], axis=-1),
            jnp.concatenate([-sin, sin, -sin, sin], axis=-1))


def _block_diag(blocks):
    n = len(blocks)
    rows = []
    for i, blk in enumerate(blocks):
        z = jnp.zeros_like(blk)
        rows.append(jnp.concatenate([blk if j == i else z for j in range(n)], axis=1))
    return jnp.concatenate(rows, axis=0)


def kernel(x, c, ctx, c_ctx, ada_w, ada_b, norm_mix_g, norm_ffn_g, ffn_w1, ffn_w2, ev_w_in, ev_w_out,
           pool_w, pool_scale, q_norm_g, k_norm_g, lam_q1, lam_k1, lam_q2, lam_k2, sub_norm_g,
           od_w_in, sg_ln_g, sg_ln_b, sg_w, sg_b, od_w_out):
    b, s, d = x.shape
    depth = ada_w.shape[0]
    assert depth == 2 and b <= SUBLANES - 1

    cin = jnp.zeros((SUBLANES, d), F32).at[:b].set(c).at[b].set(c_ctx)
    mods = _ada_call(cin, ada_w, ada_b).reshape(depth * SUBLANES, 1, 6 * d)

    def lat_row(layer):
        return lambda bi: layer * SUBLANES + bi

    def ctx_row(layer):
        return lambda bi: layer * SUBLANES + b

    perm64 = np.concatenate([np.arange(0, HEAD_DIM, 2), np.arange(1, HEAD_DIM, 2)])
    perm = (np.arange(QK_WIDTH).reshape(-1, HEAD_DIM)[:, perm64]).reshape(-1)
    w_in = ev_w_in[0]
    w_q = w_in[:, POOL_WIDTH:POOL_WIDTH + QK_WIDTH][:, perm]
    w_k = w_in[:, POOL_WIDTH + QK_WIDTH:POOL_WIDTH + 2 * QK_WIDTH][:, perm]
    w_v = w_in[:, POOL_WIDTH + 2 * QK_WIDTH:]
    w_lat = jnp.concatenate([w_in[:, :POOL_WIDTH], w_q, w_k, w_v], axis=1).astype(BF16)
    w_ctx = jnp.concatenate([w_k, w_v], axis=1).astype(BF16)
    n_groups = QK_WIDTH // HEAD_DIM
    qg = jnp.tile(q_norm_g[0][perm64] * (HEAD_DIM ** -0.5 * math.log2(math.e)), n_groups)[None, :]
    kg = jnp.tile(k_norm_g[0][perm64], n_groups)[None, :]
    ones = jnp.ones((HEAD_DIM, HEAD_DIM), BF16)
    bd = _block_diag([ones] * (256 // HEAD_DIM))
    cos, sin_signed = _rope_tables(s)
    ctx_len = ctx.shape[1]
    cos_ctx = jnp.ones((ctx_len, LANES), F32)
    sin_ctx = jnp.zeros((ctx_len, LANES), F32)
    g_mix0 = norm_mix_g[0][None, :]

    pool_in, q, k, vt = _inproj_call(x, mods, lat_row(0), g_mix0, w_lat, bd, qg, kg, cos, sin_signed, True)
    kc, vct = _inproj_call(ctx, mods, ctx_row(0), g_mix0, w_ctx, bd, qg, kg, cos_ctx, sin_ctx, False)

    lam_init = 0.8 - 0.6 * math.exp(-0.3 * 0)
    lam_rows = jnp.zeros((SUBLANES, LANES), F32)
    for r, vec in enumerate((lam_q1[0], lam_k1[0], lam_q2[0], lam_k2[0])):
        lam_rows = lam_rows.at[r, :HEAD_DIM].set(vec)
    attn = _attn_call(_score_bound(qg, kg), lam_rows, sub_norm_g[0][None, :], q, k, vt, kc, vct, lam_init)

    pool_w_bd = _block_diag([pool_w[0, gi] for gi in range(len(POOL_WINDOWS))]).astype(BF16)
    x = _merge_call(x, mods, lat_row(0), pool_in, attn, pool_w_bd, pool_scale[0][None, :],
                    ev_w_out[0].astype(BF16))
    x = _ffn_call(x, mods, lat_row(0), norm_ffn_g[0][None, :], ffn_w1[0].astype(BF16),
                  ffn_w2[0].astype(BF16))

    sg_b_full = jnp.repeat(sg_b[0].T, SG_GROUP_DIM, axis=1)
    x = _sgmlp_call(x, mods, lat_row(1), norm_mix_g[1][None, :], od_w_in[0].astype(BF16),
                    sg_ln_g[0][None, :], sg_ln_b[0][None, :], sg_w[0].astype(BF16), sg_b_full,
                    od_w_out[0].astype(BF16))
    x = _ffn_call(x, mods, lat_row(1), norm_ffn_g[1][None, :], ffn_w1[1].astype(BF16),
                  ffn_w2[1].astype(BF16))
    return x
```

```python
import functools
import math

import jax
import jax.numpy as jnp
import numpy as np
from jax import lax
from jax.experimental import pallas as pl
from jax.experimental.pallas import tpu as pltpu

F32 = jnp.float32
BF16 = jnp.bfloat16

EPS = 1e-6
GRID_W = 64
ROPE_THETA = 10000.0
POOL_WINDOWS = (2, 4, 8, 16)
POOL_GROUP = 64
POOL_WIDTH = 256
HEAD_DIM = 64
V_DIM = 128
N_HEADS = 6
QK_WIDTH = N_HEADS * 2 * HEAD_DIM
CHUNK = 128
SG_GROUPS = 8
SG_GROUP_DIM = 256

LANES = 128
SUBLANES = 8
POOL_HALO = 8
VMEM_LIMIT = 56 * 1024 * 1024

MAX_UNSHIFTED_SCORE = 60.0

TM_PROJ = 1024
ROWS_PROJ = 256
TQ = 512
ATTN_KEYS = 512
TM_MERGE = 512
TM_FFN = 256
TM_SG = 1024
ROWS_SG = 256


def _params(sem):
    return pltpu.CompilerParams(dimension_semantics=sem, vmem_limit_bytes=VMEM_LIMIT)


def _const_spec(shape):
    zeros = (0,) * len(shape)
    return pl.BlockSpec(shape, lambda *_: zeros, pipeline_mode=pl.Buffered(1))


def _norm_mod(x, g, shift, scale):
    ms = jnp.mean(x * x, axis=-1, keepdims=True)
    return x * lax.rsqrt(ms + EPS) * (g * (1.0 + scale)) + shift


def _software_pipeline(n, produce, consume):
    produce(0)
    for j in range(n):
        if j + 1 < n:
            produce(j + 1)
        consume(j)


def _ada_kernel(c_ref, w_ref, b_ref, o_ref):
    c = c_ref[...]
    sc = (c * jax.nn.sigmoid(c)).astype(BF16)
    o_ref[0] = jnp.dot(sc, w_ref[0].astype(BF16), preferred_element_type=F32) + b_ref[0]


def _ada_call(cin, ada_w, ada_b):
    depth, d, n = ada_w.shape
    tn = 1536
    return pl.pallas_call(
        _ada_kernel,
        grid=(depth, n // tn),
        in_specs=[
            pl.BlockSpec((SUBLANES, d), lambda l, j: (0, 0)),
            pl.BlockSpec((1, d, tn), lambda l, j: (l, 0, j)),
            pl.BlockSpec((1, 1, tn), lambda l, j: (l, 0, j)),
        ],
        out_specs=pl.BlockSpec((1, SUBLANES, tn), lambda l, j: (l, 0, j)),
        out_shape=jax.ShapeDtypeStruct((depth, SUBLANES, n), F32),
        compiler_params=_params(("parallel", "parallel")),
        name="ada_ln",
    )(cin, ada_w, ada_b.reshape(depth, 1, n))


def _qk_norm_rope(y_ref, ss_ref, off, ss_off, g, cos, sin_signed, first_half):
    outs = []
    for j in range(QK_WIDTH // LANES):
        cols = slice(LANES * j, LANES * (j + 1))
        ss = ss_ref[:, ss_off + LANES * j:ss_off + LANES * (j + 1)]
        xc = (y_ref[:, off + LANES * j:off + LANES * (j + 1)]
              * lax.rsqrt(ss * (1.0 / HEAD_DIM) + EPS) * g[:, cols])
        swapped = jnp.where(first_half, pltpu.roll(xc, 96, 1), pltpu.roll(xc, 32, 1))
        outs.append(xc * cos + swapped * sin_signed)
    return jnp.concatenate(outs, axis=1)


def _inproj_kernel(x_ref, mod_ref, g_ref, w_ref, bd_ref, qg_ref, kg_ref, cos_ref, sin_ref,
                   *refs, with_q):
    *out_refs, y_ref, ss_ref = refs
    d = x_ref.shape[-1]
    rows = y_ref.shape[1]
    mod = mod_ref[0]
    gain = g_ref[...] * (1.0 + mod[:, d:2 * d])
    shift = mod[:, 0:d]
    lane = lax.broadcasted_iota(jnp.int32, (rows, LANES), 1)
    first_half = (lane % HEAD_DIM) < (HEAD_DIM // 2)
    qk_off = POOL_WIDTH if with_q else 0
    qk_cols = ss_ref.shape[2]

    def project(j):
        x = x_ref[0, j * rows:(j + 1) * rows, :]
        ms = jnp.mean(x * x, axis=-1, keepdims=True)
        h = (x * lax.rsqrt(ms + EPS) * gain + shift).astype(BF16)
        y_ref[j % 2] = jnp.dot(h, w_ref[...], preferred_element_type=F32)
        for p in range(0, qk_cols, 256):
            t = y_ref[j % 2, :, qk_off + p:qk_off + p + 256]
            ss_ref[j % 2, :, p:p + 256] = jnp.dot((t * t).astype(BF16), bd_ref[...],
                                                  preferred_element_type=F32)

    def epilogue(j):
        blk = slice(j * rows, (j + 1) * rows)
        y = y_ref.at[j % 2]
        ss = ss_ref.at[j % 2]
        cos = cos_ref[blk, :]
        sin_signed = sin_ref[blk, :]
        if with_q:
            pool_ref, q_ref, k_ref, vt_ref = out_refs
            pool_ref[0, blk, :] = y[:, :POOL_WIDTH]
            q_ref[0, blk, :] = _qk_norm_rope(y, ss, qk_off, 0, qg_ref[...], cos, sin_signed,
                                             first_half).astype(BF16)
            k_at = QK_WIDTH
        else:
            k_ref, vt_ref = out_refs
            k_at = 0
        k_ref[0, blk, :] = _qk_norm_rope(y, ss, qk_off + k_at, k_at, kg_ref[...], cos, sin_signed,
                                         first_half).astype(BF16)
        vt_ref[0, 0, :, blk] = y[:, qk_off + k_at + QK_WIDTH:].T.astype(BF16)

    _software_pipeline(x_ref.shape[1] // rows, project, epilogue)


def _inproj_call(x, mods, mod_row, g, w, bd, qg, kg, cos, sin_signed, with_q):
    b, s, d = x.shape
    tm = min(TM_PROJ, s)
    nt = s // tm
    n = w.shape[1]
    dv = N_HEADS * V_DIM
    out_shape = [jax.ShapeDtypeStruct((b, s, QK_WIDTH), BF16),
                 jax.ShapeDtypeStruct((b, nt, dv, tm), BF16)]
    out_specs = [pl.BlockSpec((1, tm, QK_WIDTH), lambda bi, i: (bi, i, 0)),
                 pl.BlockSpec((1, 1, dv, tm), lambda bi, i: (bi, i, 0, 0))]
    if with_q:
        out_shape = [jax.ShapeDtypeStruct((b, s, POOL_WIDTH), F32),
                     jax.ShapeDtypeStruct((b, s, QK_WIDTH), BF16)] + out_shape
        out_specs = [pl.BlockSpec((1, tm, POOL_WIDTH), lambda bi, i: (bi, i, 0)),
                     pl.BlockSpec((1, tm, QK_WIDTH), lambda bi, i: (bi, i, 0))] + out_specs
    return pl.pallas_call(
        functools.partial(_inproj_kernel, with_q=with_q),
        grid=(b, nt),
        in_specs=[
            pl.BlockSpec((1, tm, d), lambda bi, i: (bi, i, 0)),
            pl.BlockSpec((1, 1, mods.shape[-1]), lambda bi, i: (mod_row(bi), 0, 0)),
            _const_spec((1, d)),
            _const_spec((d, n)),
            _const_spec(bd.shape),
            _const_spec(qg.shape),
            _const_spec(kg.shape),
            pl.BlockSpec((tm, LANES), lambda bi, i: (i, 0)),
            pl.BlockSpec((tm, LANES), lambda bi, i: (i, 0)),
        ],
        out_specs=out_specs,
        out_shape=out_shape,
        scratch_shapes=[pltpu.VMEM((2, min(ROWS_PROJ, tm), n), F32),
                        pltpu.VMEM((2, min(ROWS_PROJ, tm), n - dv - (POOL_WIDTH if with_q else 0)), F32)],
        compiler_params=_params(("parallel", "parallel")),
        name="inproj_q" if with_q else "inproj_ctx",
    )(x, mods, g, w, bd, qg, kg, cos, sin_signed)


def _attn_kernel(bound_ref, lam_ref, subg_ref, q_ref, k_ref, vt_ref, kc_ref, vct_ref, o_ref, st_ref, *,
                 lam_init):
    tq = q_ref.shape[1]
    n_chunks = vt_ref.shape[1]
    tk = vt_ref.shape[-1]

    lam_rows = lam_ref[...]
    a1 = jnp.sum(lam_rows[0:1] * lam_rows[1:2], axis=1, keepdims=True)
    a2 = jnp.sum(lam_rows[2:3] * lam_rows[3:4], axis=1, keepdims=True)
    lam = jnp.exp(a1) - jnp.exp(a2) + lam_init

    q = q_ref[0]
    lane = lax.broadcasted_iota(jnp.int32, q.shape, 1)
    zero = jnp.zeros_like(q)
    qq = jnp.concatenate([jnp.where(lane < HEAD_DIM, q, zero),
                          jnp.where(lane >= HEAD_DIM, q, zero)], axis=0)
    bounded = bound_ref[0] <= MAX_UNSHIFTED_SCORE

    def chunks():
        for j in range(0, kc_ref.shape[1], ATTN_KEYS):
            yield kc_ref[0, j:j + ATTN_KEYS, :], vct_ref[0, 0, :, j:j + ATTN_KEYS]
        for c in range(n_chunks):
            for j in range(0, tk, ATTN_KEYS):
                yield k_ref[0, c * tk + j:c * tk + j + ATTN_KEYS, :], vt_ref[0, c, :, j:j + ATTN_KEYS]

    def scores(kb):
        return lax.dot_general(kb, qq, (((1,), (1,)), ((), ())), preferred_element_type=F32)

    def finish(l, acc):
        ot = acc[:, :tq] / l[:, :tq] - lam * (acc[:, tq:] / l[:, tq:])
        o = ot.T
        ms = jnp.mean(o * o, axis=-1, keepdims=True)
        o_ref[0] = (o * lax.rsqrt(ms + EPS) * (subg_ref[...] * (1.0 - lam_init))).astype(BF16)

    @pl.when(bounded)
    def _():
        l = jnp.zeros((1, 2 * tq), F32)
        acc = jnp.zeros((V_DIM, 2 * tq), F32)
        work = list(chunks())
        st_ref[0, :work[0][0].shape[0], :] = scores(work[0][0])
        for c, (kb, vtb) in enumerate(work):
            if c + 1 < len(work):
                nxt = work[c + 1][0]
                st_ref[(c + 1) % 2, :nxt.shape[0], :] = scores(nxt)
            p = jnp.exp2(st_ref[c % 2, :kb.shape[0], :])
            l = l + jnp.sum(p, axis=0, keepdims=True)
            acc = acc + jnp.dot(vtb, p.astype(BF16), preferred_element_type=F32)
        finish(l, acc)

    @pl.when(jnp.logical_not(bounded))
    def _():
        m = jnp.full((1, 2 * tq), -jnp.inf, F32)
        l = jnp.zeros((1, 2 * tq), F32)
        acc = jnp.zeros((V_DIM, 2 * tq), F32)
        for kb, vtb in chunks():
            st = scores(kb)
            m_new = jnp.maximum(m, jnp.max(st, axis=0, keepdims=True))
            alpha = jnp.exp2(m - m_new)
            p = jnp.exp2(st - m_new)
            l = alpha * l + jnp.sum(p, axis=0, keepdims=True)
            acc = alpha * acc + jnp.dot(vtb, p.astype(BF16), preferred_element_type=F32)
            m = m_new
        finish(l, acc)


def _score_bound(q_gain, k_gain):
    slack = 1.0 + 2.0 ** -5
    return (HEAD_DIM * slack * jnp.max(jnp.abs(q_gain)) * jnp.max(jnp.abs(k_gain))).reshape(1)


def _attn_call(score_bound, lam_rows, sub_g, q, k, vt, kc, vct, lam_init):
    b, s, _ = q.shape
    tq = min(TQ, s)
    n_chunks, tk = vt.shape[1], vt.shape[3]
    ctx_len = kc.shape[1]
    return pl.pallas_call(
        functools.partial(_attn_kernel, lam_init=lam_init),
        grid=(b, N_HEADS, s // tq),
        in_specs=[
            pl.BlockSpec(memory_space=pltpu.SMEM),
            _const_spec(lam_rows.shape),
            _const_spec(sub_g.shape),
            pl.BlockSpec((1, tq, V_DIM), lambda bi, h, i: (bi, i, h)),
            pl.BlockSpec((1, s, V_DIM), lambda bi, h, i: (bi, 0, h)),
            pl.BlockSpec((1, n_chunks, V_DIM, tk), lambda bi, h, i: (bi, 0, h, 0)),
            pl.BlockSpec((1, ctx_len, V_DIM), lambda bi, h, i: (bi, 0, h)),
            pl.BlockSpec((1, 1, V_DIM, ctx_len), lambda bi, h, i: (bi, 0, h, 0)),
        ],
        out_specs=pl.BlockSpec((1, tq, V_DIM), lambda bi, h, i: (bi, i, h)),
        out_shape=jax.ShapeDtypeStruct((b, s, N_HEADS * V_DIM), BF16),
        scratch_shapes=[pltpu.VMEM((2, ATTN_KEYS, 2 * tq), F32)],
        compiler_params=_params(("parallel", "parallel", "arbitrary")),
        name="diff_attn",
    )(score_bound, lam_rows, sub_g, q, k, vt, kc, vct)


def _merge_kernel(x_ref, mod_ref, pm_ref, pp_ref, pn_ref, attn_ref, pw_ref, ps_ref, wo_ref,
                  o_ref, ext_ref, *, seq):
    d = x_ref.shape[-1]
    tm = x_ref.shape[1]
    i = pl.program_id(1)
    last = pl.num_programs(1) - 1
    ext_ref[0:POOL_HALO, :] = jnp.where(i > 0, pp_ref[0], 0.0)
    ext_ref[POOL_HALO:POOL_HALO + tm, :] = pm_ref[0]
    ext_ref[POOL_HALO + tm:2 * POOL_HALO + tm, :] = jnp.where(i < last, pn_ref[0], 0.0)

    def window_sum(lo, hi, half):
        acc = None
        for off in range(lo, hi + 1):
            piece = ext_ref[pl.ds(POOL_HALO + off, tm), LANES * half:LANES * (half + 1)]
            acc = piece if acc is None else acc + piece
        return acc

    t = i * tm + lax.broadcasted_iota(jnp.int32, (tm, LANES), 0)
    lane = lax.broadcasted_iota(jnp.int32, (tm, LANES), 1)
    low_group = lane < POOL_GROUP
    pooled = []
    for half in range(2):
        w_small, w_big = POOL_WINDOWS[2 * half], POOL_WINDOWS[2 * half + 1]
        s_small = window_sum(-(w_small // 2), w_small // 2 - 1, half)
        s_big = (s_small + window_sum(-(w_big // 2), -(w_small // 2) - 1, half)
                 + window_sum(w_small // 2, w_big // 2 - 1, half))
        reach = jnp.where(low_group, w_small // 2, w_big // 2)
        cnt = jnp.minimum(t + reach, seq) - jnp.maximum(t - reach, 0)
        mean = jnp.where(low_group, s_small, s_big) / cnt.astype(F32)
        pooled.append(mean - pm_ref[0, :, LANES * half:LANES * (half + 1)])
    pooled = jnp.concatenate(pooled, axis=1).astype(BF16)
    pool_y = (jnp.dot(pooled, pw_ref[...], preferred_element_type=F32) * ps_ref[...]).astype(BF16)
    y = (jnp.dot(pool_y, wo_ref[0:POOL_WIDTH, :], preferred_element_type=F32)
         + jnp.dot(attn_ref[0], wo_ref[POOL_WIDTH:, :], preferred_element_type=F32))
    o_ref[0] = x_ref[0] + mod_ref[0][:, 2 * d:3 * d] * y


def _merge_call(x, mods, mod_row, pool_in, attn, pool_w_bd, pool_scale, w_out):
    b, s, d = x.shape
    tm = min(TM_MERGE, s)
    hb = tm // POOL_HALO
    n_halo = s // POOL_HALO
    return pl.pallas_call(
        functools.partial(_merge_kernel, seq=s),
        grid=(b, s // tm),
        in_specs=[
            pl.BlockSpec((1, tm, d), lambda bi, i: (bi, i, 0)),
            pl.BlockSpec((1, 1, mods.shape[-1]), lambda bi, i: (mod_row(bi), 0, 0)),
            pl.BlockSpec((1, tm, POOL_WIDTH), lambda bi, i: (bi, i, 0)),
            pl.BlockSpec((1, POOL_HALO, POOL_WIDTH),
                         lambda bi, i: (bi, jnp.maximum(i * hb - 1, 0), 0)),
            pl.BlockSpec((1, POOL_HALO, POOL_WIDTH),
                         lambda bi, i: (bi, jnp.minimum((i + 1) * hb, n_halo - 1), 0)),
            pl.BlockSpec((1, tm, attn.shape[-1]), lambda bi, i: (bi, i, 0)),
            _const_spec(pool_w_bd.shape),
            _const_spec(pool_scale.shape),
            _const_spec(w_out.shape),
        ],
        out_specs=pl.BlockSpec((1, tm, d), lambda bi, i: (bi, i, 0)),
        out_shape=jax.ShapeDtypeStruct((b, s, d), F32),
        scratch_shapes=[pltpu.VMEM((tm + 2 * POOL_HALO, POOL_WIDTH), F32)],
        compiler_params=_params(("parallel", "parallel")),
        name="pool_merge",
    )(x, mods, pool_in, pool_in, pool_in, attn, pool_w_bd, pool_scale, w_out)


def _ffn_kernel(x_ref, mod_ref, g_ref, w1_ref, w2_ref, o_ref):
    d = x_ref.shape[-1]
    x = x_ref[0]
    mod = mod_ref[0]
    h = _norm_mod(x, g_ref[...], mod[:, 3 * d:4 * d], mod[:, 4 * d:5 * d]).astype(BF16)
    a = jnp.maximum(jnp.dot(h, w1_ref[...], preferred_element_type=F32), 0.0)
    a = (a * a).astype(BF16)
    y = jnp.dot(a, w2_ref[...], preferred_element_type=F32)
    o_ref[0] = x + mod[:, 5 * d:6 * d] * y


def _ffn_call(x, mods, mod_row, g, w1, w2):
    b, s, d = x.shape
    tm = min(TM_FFN, s)
    return pl.pallas_call(
        _ffn_kernel,
        grid=(b, s // tm),
        in_specs=[
            pl.BlockSpec((1, tm, d), lambda bi, i: (bi, i, 0)),
            pl.BlockSpec((1, 1, mods.shape[-1]), lambda bi, i: (mod_row(bi), 0, 0)),
            _const_spec((1, d)),
            _const_spec(w1.shape),
            _const_spec(w2.shape),
        ],
        out_specs=pl.BlockSpec((1, tm, d), lambda bi, i: (bi, i, 0)),
        out_shape=jax.ShapeDtypeStruct((b, s, d), F32),
        compiler_params=_params(("parallel", "parallel")),
        name="ffn",
    )(x, mods, g, w1, w2)


def _sgmlp_kernel(x_ref, mod_ref, g_ref, win_ref, lng_ref, lnb_ref, sgw_ref, sgb_ref, wout_ref,
                  o_ref, a_ref):
    d = x_ref.shape[-1]
    rows = a_ref.shape[1]
    width = win_ref.shape[1] // 2
    mod = mod_ref[0]
    gain = g_ref[...] * (1.0 + mod[:, d:2 * d])
    shift = mod[:, 0:d]
    gate = mod[:, 2 * d:3 * d]

    def project(j):
        x = x_ref[0, j * rows:(j + 1) * rows, :]
        ms = jnp.mean(x * x, axis=-1, keepdims=True)
        h = (x * lax.rsqrt(ms + EPS) * gain + shift).astype(BF16)
        a_ref[j % 2] = jnp.dot(h, win_ref[...], preferred_element_type=F32)

    def gelu(t):
        return 0.5 * t * (1.0 + lax.erf(t * (1.0 / math.sqrt(2.0))))

    def epilogue(j):
        blk = slice(j * rows, (j + 1) * rows)
        a = a_ref.at[j % 2]
        v = gelu(a[:, width:])
        mu = jnp.mean(v, axis=-1, keepdims=True)
        vc = v - mu
        var = jnp.mean(vc * vc, axis=-1, keepdims=True)
        vn = (vc * lax.rsqrt(var + EPS) * lng_ref[...] + lnb_ref[...]).astype(BF16)
        mixed = []
        for r in range(rows // CHUNK):
            cols = [jnp.dot(sgw_ref[gi],
                            vn[r * CHUNK:(r + 1) * CHUNK, gi * SG_GROUP_DIM:(gi + 1) * SG_GROUP_DIM],
                            preferred_element_type=F32) for gi in range(SG_GROUPS)]
            mixed.append(jnp.concatenate(cols, axis=1) + sgb_ref[...])
        z = (gelu(a[:, :width]) * jnp.concatenate(mixed, axis=0)).astype(BF16)
        y = jnp.dot(z, wout_ref[...], preferred_element_type=F32)
        o_ref[0, blk, :] = x_ref[0, blk, :] + gate * y

    _software_pipeline(x_ref.shape[1] // rows, project, epilogue)


def _sgmlp_call(x, mods, mod_row, g, w_in, ln_g, ln_b, sg_w, sg_b_full, w_out):
    b, s, d = x.shape
    tm = min(TM_SG, s)
    return pl.pallas_call(
        _sgmlp_kernel,
        grid=(b, s // tm),
        in_specs=[
            pl.BlockSpec((1, tm, d), lambda bi, i: (bi, i, 0)),
            pl.BlockSpec((1, 1, mods.shape[-1]), lambda bi, i: (mod_row(bi), 0, 0)),
            _const_spec((1, d)),
            _const_spec(w_in.shape),
            _const_spec(ln_g.shape),
            _const_spec(ln_b.shape),
            _const_spec(sg_w.shape),
            _const_spec(sg_b_full.shape),
            _const_spec(w_out.shape),
        ],
        out_specs=pl.BlockSpec((1, tm, d), lambda bi, i: (bi, i, 0)),
        out_shape=jax.ShapeDtypeStruct((b, s, d), F32),
        scratch_shapes=[pltpu.VMEM((2, min(ROWS_SG, tm), w_in.shape[1]), F32)],
        compiler_params=_params(("parallel", "parallel")),
        name="sg_mlp",
    )(x, mods, g, w_in, ln_g, ln_b, sg_w, sg_b_full, w_out)


def _rope_tables(n_tok):
    t = jnp.arange(n_tok, dtype=jnp.int32)
    rows = (t // GRID_W).astype(F32)
    cols = (t % GRID_W).astype(F32)
    pairs = HEAD_DIM // 4
    inv = ROPE_THETA ** (-jnp.arange(pairs, dtype=F32) / pairs)
    ang = jnp.concatenate([rows[:, None] * inv, cols[:, None] * inv], axis=-1)
    cos, sin = jnp.cos(ang), jnp.sin(ang)
    return (jnp.concatenate([cos, cos, cos, cos], axis=-1),
            jnp.concatenate([-sin, sin, -sin, sin], axis=-1))


def _block_diag(blocks):
    n = len(blocks)
    rows = []
    for i, blk in enumerate(blocks):
        z = jnp.zeros_like(blk)
        rows.append(jnp.concatenate([blk if j == i else z for j in range(n)], axis=1))
    return jnp.concatenate(rows, axis=0)


def kernel(x, c, ctx, c_ctx, ada_w, ada_b, norm_mix_g, norm_ffn_g, ffn_w1, ffn_w2, ev_w_in, ev_w_out,
           pool_w, pool_scale, q_norm_g, k_norm_g, lam_q1, lam_k1, lam_q2, lam_k2, sub_norm_g,
           od_w_in, sg_ln_g, sg_ln_b, sg_w, sg_b, od_w_out):
    b, s, d = x.shape
    depth = ada_w.shape[0]
    assert depth == 2 and b <= SUBLANES - 1

    cin = jnp.zeros((SUBLANES, d), F32).at[:b].set(c).at[b].set(c_ctx)
    mods = _ada_call(cin, ada_w, ada_b).reshape(depth * SUBLANES, 1, 6 * d)

    def lat_row(layer):
        return lambda bi: layer * SUBLANES + bi

    def ctx_row(layer):
        return lambda bi: layer * SUBLANES + b

    perm64 = np.concatenate([np.arange(0, HEAD_DIM, 2), np.arange(1, HEAD_DIM, 2)])
    perm = (np.arange(QK_WIDTH).reshape(-1, HEAD_DIM)[:, perm64]).reshape(-1)
    w_in = ev_w_in[0]
    w_q = w_in[:, POOL_WIDTH:POOL_WIDTH + QK_WIDTH][:, perm]
    w_k = w_in[:, POOL_WIDTH + QK_WIDTH:POOL_WIDTH + 2 * QK_WIDTH][:, perm]
    w_v = w_in[:, POOL_WIDTH + 2 * QK_WIDTH:]
    w_lat = jnp.concatenate([w_in[:, :POOL_WIDTH], w_q, w_k, w_v], axis=1).astype(BF16)
    w_ctx = jnp.concatenate([w_k, w_v], axis=1).astype(BF16)
    n_groups = QK_WIDTH // HEAD_DIM
    qg = jnp.tile(q_norm_g[0][perm64] * (HEAD_DIM ** -0.5 * math.log2(math.e)), n_groups)[None, :]
    kg = jnp.tile(k_norm_g[0][perm64], n_groups)[None, :]
    ones = jnp.ones((HEAD_DIM, HEAD_DIM), BF16)
    bd = _block_diag([ones] * (256 // HEAD_DIM))
    cos, sin_signed = _rope_tables(s)
    ctx_len = ctx.shape[1]
    cos_ctx = jnp.ones((ctx_len, LANES), F32)
    sin_ctx = jnp.zeros((ctx_len, LANES), F32)
    g_mix0 = norm_mix_g[0][None, :]

    pool_in, q, k, vt = _inproj_call(x, mods, lat_row(0), g_mix0, w_lat, bd, qg, kg, cos, sin_signed, True)
    kc, vct = _inproj_call(ctx, mods, ctx_row(0), g_mix0, w_ctx, bd, qg, kg, cos_ctx, sin_ctx, False)

    lam_init = 0.8 - 0.6 * math.exp(-0.3 * 0)
    lam_rows = jnp.zeros((SUBLANES, LANES), F32)
    for r, vec in enumerate((lam_q1[0], lam_k1[0], lam_q2[0], lam_k2[0])):
        lam_rows = lam_rows.at[r, :HEAD_DIM].set(vec)
    attn = _attn_call(_score_bound(qg, kg), lam_rows, sub_norm_g[0][None, :], q, k, vt, kc, vct, lam_init)

    pool_w_bd = _block_diag([pool_w[0, gi] for gi in range(len(POOL_WINDOWS))]).astype(BF16)
    x = _merge_call(x, mods, lat_row(0), pool_in, attn, pool_w_bd, pool_scale[0][None, :],
                    ev_w_out[0].astype(BF16))
    x = _ffn_call(x, mods, lat_row(0), norm_ffn_g[0][None, :], ffn_w1[0].astype(BF16),
                  ffn_w2[0].astype(BF16))

    sg_b_full = jnp.repeat(sg_b[0].T, SG_GROUP_DIM, axis=1)
    x = _sgmlp_call(x, mods, lat_row(1), norm_mix_g[1][None, :], od_w_in[0].astype(BF16),
                    sg_ln_g[0][None, :], sg_ln_b[0][None, :], sg_w[0].astype(BF16), sg_b_full,
                    od_w_out[0].astype(BF16))
    x = _ffn_call(x, mods, lat_row(1), norm_ffn_g[1][None, :], ffn_w1[1].astype(BF16),
                  ffn_w2[1].astype(BF16))
    return x
```

```python
import functools
import math

import jax
import jax.numpy as jnp
import numpy as np
from jax import lax
from jax.experimental import pallas as pl
from jax.experimental.pallas import tpu as pltpu

F32 = jnp.float32
BF16 = jnp.bfloat16

EPS = 1e-6
GRID_W = 64
ROPE_THETA = 10000.0
POOL_WINDOWS = (2, 4, 8, 16)
POOL_GROUP = 64
POOL_WIDTH = 256
HEAD_DIM = 64
V_DIM = 128
N_HEADS = 6
QK_WIDTH = N_HEADS * 2 * HEAD_DIM
CHUNK = 128
SG_GROUPS = 8
SG_GROUP_DIM = 256

LANES = 128
SUBLANES = 8
POOL_HALO = 8
VMEM_LIMIT = 56 * 1024 * 1024

MAX_UNSHIFTED_SCORE = 60.0

TM_PROJ = 1024
ROWS_PROJ = 256
TQ = 512
ATTN_KEYS = 512
TM_FFN = 512
ROWS_FFN = 256
TM_SG = 1024
ROWS_SG = 256


def _params(sem):
    return pltpu.CompilerParams(dimension_semantics=sem, vmem_limit_bytes=VMEM_LIMIT)


def _const_spec(shape):
    zeros = (0,) * len(shape)
    return pl.BlockSpec(shape, lambda *_: zeros, pipeline_mode=pl.Buffered(1))


def _norm_mod(x, g, shift, scale):
    ms = jnp.mean(x * x, axis=-1, keepdims=True)
    return x * lax.rsqrt(ms + EPS) * (g * (1.0 + scale)) + shift


def _software_pipeline(n, produce, consume):
    produce(0)
    for j in range(n):
        if j + 1 < n:
            produce(j + 1)
        consume(j)


def _ada_kernel(c_ref, w_ref, b_ref, o_ref):
    c = c_ref[...]
    sc = (c * jax.nn.sigmoid(c)).astype(BF16)
    o_ref[0] = jnp.dot(sc, w_ref[0].astype(BF16), preferred_element_type=F32) + b_ref[0]


def _ada_call(cin, ada_w, ada_b):
    depth, d, n = ada_w.shape
    tn = 1536
    return pl.pallas_call(
        _ada_kernel,
        grid=(depth, n // tn),
        in_specs=[
            pl.BlockSpec((SUBLANES, d), lambda l, j: (0, 0)),
            pl.BlockSpec((1, d, tn), lambda l, j: (l, 0, j)),
            pl.BlockSpec((1, 1, tn), lambda l, j: (l, 0, j)),
        ],
        out_specs=pl.BlockSpec((1, SUBLANES, tn), lambda l, j: (l, 0, j)),
        out_shape=jax.ShapeDtypeStruct((depth, SUBLANES, n), F32),
        compiler_params=_params(("parallel", "parallel")),
        name="ada_ln",
    )(cin, ada_w, ada_b.reshape(depth, 1, n))


def _qk_norm_rope(y_ref, ss_ref, off, ss_off, g, cos, sin_signed, first_half):
    outs = []
    for j in range(QK_WIDTH // LANES):
        cols = slice(LANES * j, LANES * (j + 1))
        ss = ss_ref[:, ss_off + LANES * j:ss_off + LANES * (j + 1)]
        xc = (y_ref[:, off + LANES * j:off + LANES * (j + 1)]
              * lax.rsqrt(ss * (1.0 / HEAD_DIM) + EPS) * g[:, cols])
        swapped = jnp.where(first_half, pltpu.roll(xc, 96, 1), pltpu.roll(xc, 32, 1))
        outs.append(xc * cos + swapped * sin_signed)
    return jnp.concatenate(outs, axis=1)


def _inproj_kernel(x_ref, mod_ref, g_ref, w_ref, bd_ref, qg_ref, kg_ref, cos_ref, sin_ref,
                   *refs, with_q):
    *out_refs, y_ref, ss_ref = refs
    d = x_ref.shape[-1]
    rows = y_ref.shape[1]
    mod = mod_ref[0]
    gain = g_ref[...] * (1.0 + mod[:, d:2 * d])
    shift = mod[:, 0:d]
    lane = lax.broadcasted_iota(jnp.int32, (rows, LANES), 1)
    first_half = (lane % HEAD_DIM) < (HEAD_DIM // 2)
    qk_off = POOL_WIDTH if with_q else 0
    qk_cols = ss_ref.shape[2]

    def project(j):
        x = x_ref[0, j * rows:(j + 1) * rows, :]
        ms = jnp.mean(x * x, axis=-1, keepdims=True)
        h = (x * lax.rsqrt(ms + EPS) * gain + shift).astype(BF16)
        y_ref[j % 2] = jnp.dot(h, w_ref[...], preferred_element_type=F32)
        for p in range(0, qk_cols, 256):
            t = y_ref[j % 2, :, qk_off + p:qk_off + p + 256]
            ss_ref[j % 2, :, p:p + 256] = jnp.dot((t * t).astype(BF16), bd_ref[...],
                                                  preferred_element_type=F32)

    def epilogue(j):
        blk = slice(j * rows, (j + 1) * rows)
        y = y_ref.at[j % 2]
        ss = ss_ref.at[j % 2]
        cos = cos_ref[blk, :]
        sin_signed = sin_ref[blk, :]
        if with_q:
            pool_ref, q_ref, k_ref, vt_ref = out_refs
            pool_ref[0, blk, :] = y[:, :POOL_WIDTH]
            q_ref[0, blk, :] = _qk_norm_rope(y, ss, qk_off, 0, qg_ref[...], cos, sin_signed,
                                             first_half).astype(BF16)
            k_at = QK_WIDTH
        else:
            k_ref, vt_ref = out_refs
            k_at = 0
        k_ref[0, blk, :] = _qk_norm_rope(y, ss, qk_off + k_at, k_at, kg_ref[...], cos, sin_signed,
                                         first_half).astype(BF16)
        vt_ref[0, 0, :, blk] = y[:, qk_off + k_at + QK_WIDTH:].T.astype(BF16)

    _software_pipeline(x_ref.shape[1] // rows, project, epilogue)


def _inproj_call(x, mods, mod_row, g, w, bd, qg, kg, cos, sin_signed, with_q):
    b, s, d = x.shape
    tm = min(TM_PROJ, s)
    nt = s // tm
    n = w.shape[1]
    dv = N_HEADS * V_DIM
    out_shape = [jax.ShapeDtypeStruct((b, s, QK_WIDTH), BF16),
                 jax.ShapeDtypeStruct((b, nt, dv, tm), BF16)]
    out_specs = [pl.BlockSpec((1, tm, QK_WIDTH), lambda bi, i: (bi, i, 0)),
                 pl.BlockSpec((1, 1, dv, tm), lambda bi, i: (bi, i, 0, 0))]
    if with_q:
        out_shape = [jax.ShapeDtypeStruct((b, s, POOL_WIDTH), F32),
                     jax.ShapeDtypeStruct((b, s, QK_WIDTH), BF16)] + out_shape
        out_specs = [pl.BlockSpec((1, tm, POOL_WIDTH), lambda bi, i: (bi, i, 0)),
                     pl.BlockSpec((1, tm, QK_WIDTH), lambda bi, i: (bi, i, 0))] + out_specs
    return pl.pallas_call(
        functools.partial(_inproj_kernel, with_q=with_q),
        grid=(b, nt),
        in_specs=[
            pl.BlockSpec((1, tm, d), lambda bi, i: (bi, i, 0)),
            pl.BlockSpec((1, 1, mods.shape[-1]), lambda bi, i: (mod_row(bi), 0, 0)),
            _const_spec((1, d)),
            _const_spec((d, n)),
            _const_spec(bd.shape),
            _const_spec(qg.shape),
            _const_spec(kg.shape),
            pl.BlockSpec((tm, LANES), lambda bi, i: (i, 0)),
            pl.BlockSpec((tm, LANES), lambda bi, i: (i, 0)),
        ],
        out_specs=out_specs,
        out_shape=out_shape,
        scratch_shapes=[pltpu.VMEM((2, min(ROWS_PROJ, tm), n), F32),
                        pltpu.VMEM((2, min(ROWS_PROJ, tm), n - dv - (POOL_WIDTH if with_q else 0)), F32)],
        compiler_params=_params(("parallel", "parallel")),
        name="inproj_q" if with_q else "inproj_ctx",
    )(x, mods, g, w, bd, qg, kg, cos, sin_signed)


def _attn_kernel(bound_ref, lam_ref, subg_ref, q_ref, k_ref, vt_ref, kc_ref, vct_ref, o_ref, st_ref, *,
                 lam_init):
    tq = q_ref.shape[1]
    n_chunks = vt_ref.shape[1]
    tk = vt_ref.shape[-1]

    lam_rows = lam_ref[...]
    a1 = jnp.sum(lam_rows[0:1] * lam_rows[1:2], axis=1, keepdims=True)
    a2 = jnp.sum(lam_rows[2:3] * lam_rows[3:4], axis=1, keepdims=True)
    lam = jnp.exp(a1) - jnp.exp(a2) + lam_init

    q = q_ref[0]
    lane = lax.broadcasted_iota(jnp.int32, q.shape, 1)
    zero = jnp.zeros_like(q)
    qq = jnp.concatenate([jnp.where(lane < HEAD_DIM, q, zero),
                          jnp.where(lane >= HEAD_DIM, q, zero)], axis=0)
    bounded = bound_ref[0] <= MAX_UNSHIFTED_SCORE

    def chunks():
        for j in range(0, kc_ref.shape[1], ATTN_KEYS):
            yield kc_ref[0, j:j + ATTN_KEYS, :], vct_ref[0, 0, :, j:j + ATTN_KEYS]
        for c in range(n_chunks):
            for j in range(0, tk, ATTN_KEYS):
                yield k_ref[0, c * tk + j:c * tk + j + ATTN_KEYS, :], vt_ref[0, c, :, j:j + ATTN_KEYS]

    def scores(kb):
        return lax.dot_general(kb, qq, (((1,), (1,)), ((), ())), preferred_element_type=F32)

    def finish(l, acc):
        ot = acc[:, :tq] / l[:, :tq] - lam * (acc[:, tq:] / l[:, tq:])
        o = ot.T
        ms = jnp.mean(o * o, axis=-1, keepdims=True)
        o_ref[0] = (o * lax.rsqrt(ms + EPS) * (subg_ref[...] * (1.0 - lam_init))).astype(BF16)

    @pl.when(bounded)
    def _():
        l = jnp.zeros((1, 2 * tq), F32)
        acc = jnp.zeros((V_DIM, 2 * tq), F32)
        work = list(chunks())
        st_ref[0, :work[0][0].shape[0], :] = scores(work[0][0])
        for c, (kb, vtb) in enumerate(work):
            if c + 1 < len(work):
                nxt = work[c + 1][0]
                st_ref[(c + 1) % 2, :nxt.shape[0], :] = scores(nxt)
            p = jnp.exp2(st_ref[c % 2, :kb.shape[0], :])
            l = l + jnp.sum(p, axis=0, keepdims=True)
            acc = acc + jnp.dot(vtb, p.astype(BF16), preferred_element_type=F32)
        finish(l, acc)

    @pl.when(jnp.logical_not(bounded))
    def _():
        m = jnp.full((1, 2 * tq), -jnp.inf, F32)
        l = jnp.zeros((1, 2 * tq), F32)
        acc = jnp.zeros((V_DIM, 2 * tq), F32)
        for kb, vtb in chunks():
            st = scores(kb)
            m_new = jnp.maximum(m, jnp.max(st, axis=0, keepdims=True))
            alpha = jnp.exp2(m - m_new)
            p = jnp.exp2(st - m_new)
            l = alpha * l + jnp.sum(p, axis=0, keepdims=True)
            acc = alpha * acc + jnp.dot(vtb, p.astype(BF16), preferred_element_type=F32)
            m = m_new
        finish(l, acc)


def _score_bound(q_gain, k_gain):
    slack = 1.0 + 2.0 ** -5
    return (HEAD_DIM * slack * jnp.max(jnp.abs(q_gain)) * jnp.max(jnp.abs(k_gain))).reshape(1)


def _attn_call(score_bound, lam_rows, sub_g, q, k, vt, kc, vct, lam_init):
    b, s, _ = q.shape
    tq = min(TQ, s)
    n_chunks, tk = vt.shape[1], vt.shape[3]
    ctx_len = kc.shape[1]
    return pl.pallas_call(
        functools.partial(_attn_kernel, lam_init=lam_init),
        grid=(b, N_HEADS, s // tq),
        in_specs=[
            pl.BlockSpec(memory_space=pltpu.SMEM),
            _const_spec(lam_rows.shape),
            _const_spec(sub_g.shape),
            pl.BlockSpec((1, tq, V_DIM), lambda bi, h, i: (bi, i, h)),
            pl.BlockSpec((1, s, V_DIM), lambda bi, h, i: (bi, 0, h)),
            pl.BlockSpec((1, n_chunks, V_DIM, tk), lambda bi, h, i: (bi, 0, h, 0)),
            pl.BlockSpec((1, ctx_len, V_DIM), lambda bi, h, i: (bi, 0, h)),
            pl.BlockSpec((1, 1, V_DIM, ctx_len), lambda bi, h, i: (bi, 0, h, 0)),
        ],
        out_specs=pl.BlockSpec((1, tq, V_DIM), lambda bi, h, i: (bi, i, h)),
        out_shape=jax.ShapeDtypeStruct((b, s, N_HEADS * V_DIM), BF16),
        scratch_shapes=[pltpu.VMEM((2, ATTN_KEYS, 2 * tq), F32)],
        compiler_params=_params(("parallel", "parallel", "arbitrary")),
        name="diff_attn",
    )(score_bound, lam_rows, sub_g, q, k, vt, kc, vct)


def _pool_merge_rows(x_ref, mod, pm_ref, pp_ref, pn_ref, attn_ref, pw_ref, ps_ref, wo_ref, ext_ref, seq):
    d = x_ref.shape[-1]
    tm = x_ref.shape[1]
    i = pl.program_id(1)
    last = pl.num_programs(1) - 1
    ext_ref[0:POOL_HALO, :] = jnp.where(i > 0, pp_ref[0], 0.0)
    ext_ref[POOL_HALO:POOL_HALO + tm, :] = pm_ref[0]
    ext_ref[POOL_HALO + tm:2 * POOL_HALO + tm, :] = jnp.where(i < last, pn_ref[0], 0.0)

    def window_sum(lo, hi, half):
        acc = None
        for off in range(lo, hi + 1):
            piece = ext_ref[pl.ds(POOL_HALO + off, tm), LANES * half:LANES * (half + 1)]
            acc = piece if acc is None else acc + piece
        return acc

    t = i * tm + lax.broadcasted_iota(jnp.int32, (tm, LANES), 0)
    lane = lax.broadcasted_iota(jnp.int32, (tm, LANES), 1)
    low_group = lane < POOL_GROUP
    pooled = []
    for half in range(2):
        w_small, w_big = POOL_WINDOWS[2 * half], POOL_WINDOWS[2 * half + 1]
        s_small = window_sum(-(w_small // 2), w_small // 2 - 1, half)
        s_big = (s_small + window_sum(-(w_big // 2), -(w_small // 2) - 1, half)
                 + window_sum(w_small // 2, w_big // 2 - 1, half))
        reach = jnp.where(low_group, w_small // 2, w_big // 2)
        cnt = jnp.minimum(t + reach, seq) - jnp.maximum(t - reach, 0)
        mean = jnp.where(low_group, s_small, s_big) / cnt.astype(F32)
        pooled.append(mean - pm_ref[0, :, LANES * half:LANES * (half + 1)])
    pooled = jnp.concatenate(pooled, axis=1).astype(BF16)
    pool_y = (jnp.dot(pooled, pw_ref[...], preferred_element_type=F32) * ps_ref[...]).astype(BF16)
    y = (jnp.dot(pool_y, wo_ref[0:POOL_WIDTH, :], preferred_element_type=F32)
         + jnp.dot(attn_ref[0], wo_ref[POOL_WIDTH:, :], preferred_element_type=F32))
    return x_ref[0] + mod[:, 2 * d:3 * d] * y


def _ffn_rows(x, mod, g, w1_ref, w2_ref):
    d = x.shape[-1]
    h = _norm_mod(x, g, mod[:, 3 * d:4 * d], mod[:, 4 * d:5 * d]).astype(BF16)
    a = jnp.maximum(jnp.dot(h, w1_ref[...], preferred_element_type=F32), 0.0)
    a = (a * a).astype(BF16)
    y = jnp.dot(a, w2_ref[...], preferred_element_type=F32)
    return x + mod[:, 5 * d:6 * d] * y


def _ffn_blocks(x_block, o_ref, mod, g, w1_ref, w2_ref):
    tm = o_ref.shape[1]
    rows = min(ROWS_FFN, tm)
    for r in range(tm // rows):
        blk = slice(r * rows, (r + 1) * rows)
        o_ref[0, blk, :] = _ffn_rows(x_block(blk), mod, g, w1_ref, w2_ref)


def _ffn_kernel(x_ref, mod_ref, g_ref, w1_ref, w2_ref, o_ref):
    _ffn_blocks(lambda blk: x_ref[0, blk, :], o_ref, mod_ref[0], g_ref[...], w1_ref, w2_ref)


def _merge_ffn_kernel(x_ref, mod_ref, pm_ref, pp_ref, pn_ref, attn_ref, pw_ref, ps_ref, wo_ref,
                      g_ref, w1_ref, w2_ref, o_ref, ext_ref, *, seq):
    mod = mod_ref[0]
    x1 = _pool_merge_rows(x_ref, mod, pm_ref, pp_ref, pn_ref, attn_ref, pw_ref, ps_ref, wo_ref, ext_ref, seq)
    _ffn_blocks(lambda blk: x1[blk, :], o_ref, mod, g_ref[...], w1_ref, w2_ref)


def _merge_ffn_call(x, mods, mod_row, pool_in, attn, pool_w_bd, pool_scale, w_out, g, w1, w2):
    b, s, d = x.shape
    tm = min(TM_FFN, s)
    hb = tm // POOL_HALO
    n_halo = s // POOL_HALO
    return pl.pallas_call(
        functools.partial(_merge_ffn_kernel, seq=s),
        grid=(b, s // tm),
        in_specs=[
            pl.BlockSpec((1, tm, d), lambda bi, i: (bi, i, 0)),
            pl.BlockSpec((1, 1, mods.shape[-1]), lambda bi, i: (mod_row(bi), 0, 0)),
            pl.BlockSpec((1, tm, POOL_WIDTH), lambda bi, i: (bi, i, 0)),
            pl.BlockSpec((1, POOL_HALO, POOL_WIDTH),
                         lambda bi, i: (bi, jnp.maximum(i * hb - 1, 0), 0)),
            pl.BlockSpec((1, POOL_HALO, POOL_WIDTH),
                         lambda bi, i: (bi, jnp.minimum((i + 1) * hb, n_halo - 1), 0)),
            pl.BlockSpec((1, tm, attn.shape[-1]), lambda bi, i: (bi, i, 0)),
            _const_spec(pool_w_bd.shape),
            _const_spec(pool_scale.shape),
            _const_spec(w_out.shape),
            _const_spec((1, d)),
            _const_spec(w1.shape),
            _const_spec(w2.shape),
        ],
        out_specs=pl.BlockSpec((1, tm, d), lambda bi, i: (bi, i, 0)),
        out_shape=jax.ShapeDtypeStruct((b, s, d), F32),
        scratch_shapes=[pltpu.VMEM((tm + 2 * POOL_HALO, POOL_WIDTH), F32)],
        compiler_params=_params(("parallel", "parallel")),
        name="pool_merge_ffn",
    )(x, mods, pool_in, pool_in, pool_in, attn, pool_w_bd, pool_scale, w_out, g, w1, w2)


def _ffn_call(x, mods, mod_row, g, w1, w2):
    b, s, d = x.shape
    tm = min(TM_FFN, s)
    return pl.pallas_call(
        _ffn_kernel,
        grid=(b, s // tm),
        in_specs=[
            pl.BlockSpec((1, tm, d), lambda bi, i: (bi, i, 0)),
            pl.BlockSpec((1, 1, mods.shape[-1]), lambda bi, i: (mod_row(bi), 0, 0)),
            _const_spec((1, d)),
            _const_spec(w1.shape),
            _const_spec(w2.shape),
        ],
        out_specs=pl.BlockSpec((1, tm, d), lambda bi, i: (bi, i, 0)),
        out_shape=jax.ShapeDtypeStruct((b, s, d), F32),
        compiler_params=_params(("parallel", "parallel")),
        name="ffn",
    )(x, mods, g, w1, w2)


def _sgmlp_kernel(x_ref, mod_ref, g_ref, win_ref, lng_ref, lnb_ref, sgw_ref, sgb_ref, wout_ref,
                  o_ref, a_ref):
    d = x_ref.shape[-1]
    rows = a_ref.shape[1]
    width = win_ref.shape[1] // 2
    mod = mod_ref[0]
    gain = g_ref[...] * (1.0 + mod[:, d:2 * d])
    shift = mod[:, 0:d]
    gate = mod[:, 2 * d:3 * d]

    def project(j):
        x = x_ref[0, j * rows:(j + 1) * rows, :]
        ms = jnp.mean(x * x, axis=-1, keepdims=True)
        h = (x * lax.rsqrt(ms + EPS) * gain + shift).astype(BF16)
        a_ref[j % 2] = jnp.dot(h, win_ref[...], preferred_element_type=F32)

    def gelu(t):
        return 0.5 * t * (1.0 + lax.erf(t * (1.0 / math.sqrt(2.0))))

    def epilogue(j):
        blk = slice(j * rows, (j + 1) * rows)
        a = a_ref.at[j % 2]
        v = gelu(a[:, width:])
        mu = jnp.mean(v, axis=-1, keepdims=True)
        vc = v - mu
        var = jnp.mean(vc * vc, axis=-1, keepdims=True)
        vn = (vc * lax.rsqrt(var + EPS) * lng_ref[...] + lnb_ref[...]).astype(BF16)
        mixed = []
        for r in range(rows // CHUNK):
            cols = [jnp.dot(sgw_ref[gi],
                            vn[r * CHUNK:(r + 1) * CHUNK, gi * SG_GROUP_DIM:(gi + 1) * SG_GROUP_DIM],
                            preferred_element_type=F32) for gi in range(SG_GROUPS)]
            mixed.append(jnp.concatenate(cols, axis=1) + sgb_ref[...])
        z = (gelu(a[:, :width]) * jnp.concatenate(mixed, axis=0)).astype(BF16)
        y = jnp.dot(z, wout_ref[...], preferred_element_type=F32)
        o_ref[0, blk, :] = x_ref[0, blk, :] + gate * y

    _software_pipeline(x_ref.shape[1] // rows, project, epilogue)


def _sgmlp_call(x, mods, mod_row, g, w_in, ln_g, ln_b, sg_w, sg_b_full, w_out):
    b, s, d = x.shape
    tm = min(TM_SG, s)
    return pl.pallas_call(
        _sgmlp_kernel,
        grid=(b, s // tm),
        in_specs=[
            pl.BlockSpec((1, tm, d), lambda bi, i: (bi, i, 0)),
            pl.BlockSpec((1, 1, mods.shape[-1]), lambda bi, i: (mod_row(bi), 0, 0)),
            _const_spec((1, d)),
            _const_spec(w_in.shape),
            _const_spec(ln_g.shape),
            _const_spec(ln_b.shape),
            _const_spec(sg_w.shape),
            _const_spec(sg_b_full.shape),
            _const_spec(w_out.shape),
        ],
        out_specs=pl.BlockSpec((1, tm, d), lambda bi, i: (bi, i, 0)),
        out_shape=jax.ShapeDtypeStruct((b, s, d), F32),
        scratch_shapes=[pltpu.VMEM((2, min(ROWS_SG, tm), w_in.shape[1]), F32)],
        compiler_params=_params(("parallel", "parallel")),
        name="sg_mlp",
    )(x, mods, g, w_in, ln_g, ln_b, sg_w, sg_b_full, w_out)


def _rope_tables(n_tok):
    t = jnp.arange(n_tok, dtype=jnp.int32)
    rows = (t // GRID_W).astype(F32)
    cols = (t % GRID_W).astype(F32)
    pairs = HEAD_DIM // 4
    inv = ROPE_THETA ** (-jnp.arange(pairs, dtype=F32) / pairs)
    ang = jnp.concatenate([rows[:, None] * inv, cols[:, None] * inv], axis=-1)
    cos, sin = jnp.cos(ang), jnp.sin(ang)
    return (jnp.concatenate([cos, cos, cos, cos], axis=-1),
            jnp.concatenate([-sin, sin, -sin, sin], axis=-1))


def _block_diag(blocks):
    n = len(blocks)
    rows = []
    for i, blk in enumerate(blocks):
        z = jnp.zeros_like(blk)
        rows.append(jnp.concatenate([blk if j == i else z for j in range(n)], axis=1))
    return jnp.concatenate(rows, axis=0)


def kernel(x, c, ctx, c_ctx, ada_w, ada_b, norm_mix_g, norm_ffn_g, ffn_w1, ffn_w2, ev_w_in, ev_w_out,
           pool_w, pool_scale, q_norm_g, k_norm_g, lam_q1, lam_k1, lam_q2, lam_k2, sub_norm_g,
           od_w_in, sg_ln_g, sg_ln_b, sg_w, sg_b, od_w_out):
    b, s, d = x.shape
    depth = ada_w.shape[0]
    assert depth == 2 and b <= SUBLANES - 1

    cin = jnp.zeros((SUBLANES, d), F32).at[:b].set(c).at[b].set(c_ctx)
    mods = _ada_call(cin, ada_w, ada_b).reshape(depth * SUBLANES, 1, 6 * d)

    def lat_row(layer):
        return lambda bi: layer * SUBLANES + bi

    def ctx_row(layer):
        return lambda bi: layer * SUBLANES + b

    perm64 = np.concatenate([np.arange(0, HEAD_DIM, 2), np.arange(1, HEAD_DIM, 2)])
    perm = (np.arange(QK_WIDTH).reshape(-1, HEAD_DIM)[:, perm64]).reshape(-1)
    w_in = ev_w_in[0]
    w_q = w_in[:, POOL_WIDTH:POOL_WIDTH + QK_WIDTH][:, perm]
    w_k = w_in[:, POOL_WIDTH + QK_WIDTH:POOL_WIDTH + 2 * QK_WIDTH][:, perm]
    w_v = w_in[:, POOL_WIDTH + 2 * QK_WIDTH:]
    w_lat = jnp.concatenate([w_in[:, :POOL_WIDTH], w_q, w_k, w_v], axis=1).astype(BF16)
    w_ctx = jnp.concatenate([w_k, w_v], axis=1).astype(BF16)
    n_groups = QK_WIDTH // HEAD_DIM
    qg = jnp.tile(q_norm_g[0][perm64] * (HEAD_DIM ** -0.5 * math.log2(math.e)), n_groups)[None, :]
    kg = jnp.tile(k_norm_g[0][perm64], n_groups)[None, :]
    ones = jnp.ones((HEAD_DIM, HEAD_DIM), BF16)
    bd = _block_diag([ones] * (256 // HEAD_DIM))
    cos, sin_signed = _rope_tables(s)
    ctx_len = ctx.shape[1]
    cos_ctx = jnp.ones((ctx_len, LANES), F32)
    sin_ctx = jnp.zeros((ctx_len, LANES), F32)
    g_mix0 = norm_mix_g[0][None, :]

    pool_in, q, k, vt = _inproj_call(x, mods, lat_row(0), g_mix0, w_lat, bd, qg, kg, cos, sin_signed, True)
    kc, vct = _inproj_call(ctx, mods, ctx_row(0), g_mix0, w_ctx, bd, qg, kg, cos_ctx, sin_ctx, False)

    lam_init = 0.8 - 0.6 * math.exp(-0.3 * 0)
    lam_rows = jnp.zeros((SUBLANES, LANES), F32)
    for r, vec in enumerate((lam_q1[0], lam_k1[0], lam_q2[0], lam_k2[0])):
        lam_rows = lam_rows.at[r, :HEAD_DIM].set(vec)
    attn = _attn_call(_score_bound(qg, kg), lam_rows, sub_norm_g[0][None, :], q, k, vt, kc, vct, lam_init)

    pool_w_bd = _block_diag([pool_w[0, gi] for gi in range(len(POOL_WINDOWS))]).astype(BF16)
    x = _merge_ffn_call(x, mods, lat_row(0), pool_in, attn, pool_w_bd, pool_scale[0][None, :],
                        ev_w_out[0].astype(BF16), norm_ffn_g[0][None, :], ffn_w1[0].astype(BF16),
                        ffn_w2[0].astype(BF16))

    sg_b_full = jnp.repeat(sg_b[0].T, SG_GROUP_DIM, axis=1)
    x = _sgmlp_call(x, mods, lat_row(1), norm_mix_g[1][None, :], od_w_in[0].astype(BF16),
                    sg_ln_g[0][None, :], sg_ln_b[0][None, :], sg_w[0].astype(BF16), sg_b_full,
                    od_w_out[0].astype(BF16))
    x = _ffn_call(x, mods, lat_row(1), norm_ffn_g[1][None, :], ffn_w1[1].astype(BF16),
                  ffn_w2[1].astype(BF16))
    return x
```

```python
import functools
import math

import jax
import jax.numpy as jnp
from jax import lax
from jax.experimental import pallas as pl
from jax.experimental.pallas import tpu as pltpu

F32 = jnp.float32
BF16 = jnp.bfloat16

EPS = 1e-6
GRID_W = 64
ROPE_THETA = 10000.0
POOL_WINDOWS = (2, 4, 8, 16)
POOL_GROUP = 64
POOL_WIDTH = 256
HEAD_DIM = 64
V_DIM = 128
N_HEADS = 6
QK_WIDTH = N_HEADS * 2 * HEAD_DIM
CHUNK = 128
SG_GROUPS = 8
SG_GROUP_DIM = 256

LANES = 128
SUBLANES = 8
POOL_HALO = 8
VMEM_LIMIT = 56 * 1024 * 1024
STAGE_BYTES = 2 * 1024 * 1024

MAX_UNSHIFTED_SCORE = 60.0

TM_PROJ = 1024
ROWS_PROJ = 256
TQ = 512
ATTN_KEYS = 512
TM_FFN = 512
ROWS_FFN = 256
TM_SG = 1024
ROWS_SG = 256


def _params(sem):
    return pltpu.CompilerParams(dimension_semantics=sem, vmem_limit_bytes=VMEM_LIMIT)


def _const_spec(shape):
    zeros = (0,) * len(shape)
    return pl.BlockSpec(shape, lambda *_: zeros, pipeline_mode=pl.Buffered(1))


def _norm_mod(x, g, shift, scale):
    ms = jnp.mean(x * x, axis=-1, keepdims=True)
    return x * lax.rsqrt(ms + EPS) * (g * (1.0 + scale)) + shift


def _software_pipeline(n, produce, consume):
    produce(0)
    for j in range(n):
        if j + 1 < n:
            produce(j + 1)
        consume(j)


def _first_step():
    return jnp.logical_and(pl.program_id(0) == 0, pl.program_id(1) == 0)


def _stage_weight(w_hbm, dst_ref, stage_ref, sem, col0=0):
    k, n = dst_ref.shape
    slab = stage_ref.shape[1]
    n_slabs = k // slab

    def copy(i):
        return pltpu.make_async_copy(w_hbm.at[pl.ds(i * slab, slab), pl.ds(col0, n)],
                                     stage_ref.at[i % 2], sem.at[i % 2])

    copy(0).start()
    for i in range(n_slabs):
        if i + 1 < n_slabs:
            copy(i + 1).start()
        copy(i).wait()
        dst_ref[i * slab:(i + 1) * slab, :] = stage_ref[i % 2].astype(BF16)


def _stage_scratch(k, n):
    slab = max(r for r in range(SUBLANES, k + 1, SUBLANES) if k % r == 0 and (r == SUBLANES or 4 * r * n <= STAGE_BYTES))
    return [pltpu.VMEM((k, n), BF16), pltpu.VMEM((2, slab, n), F32), pltpu.SemaphoreType.DMA((2,))]


_HBM = pl.BlockSpec(memory_space=pl.ANY)


def _ada_kernel(c_ref, w_ref, b_ref, o_ref):
    c = c_ref[...]
    sc = (c * jax.nn.sigmoid(c)).astype(BF16)
    o_ref[0] = jnp.dot(sc, w_ref[0].astype(BF16), preferred_element_type=F32) + b_ref[0]


def _ada_call(cin, ada_w, ada_b):
    depth, d, n = ada_w.shape
    tn = 1536
    return pl.pallas_call(
        _ada_kernel,
        grid=(depth, n // tn),
        in_specs=[
            pl.BlockSpec((SUBLANES, d), lambda l, j: (0, 0)),
            pl.BlockSpec((1, d, tn), lambda l, j: (l, 0, j)),
            pl.BlockSpec((1, 1, tn), lambda l, j: (l, 0, j)),
        ],
        out_specs=pl.BlockSpec((1, SUBLANES, tn), lambda l, j: (l, 0, j)),
        out_shape=jax.ShapeDtypeStruct((depth, SUBLANES, n), F32),
        compiler_params=_params(("parallel", "parallel")),
        name="ada_ln",
    )(cin, ada_w, ada_b.reshape(depth, 1, n))


def _qk_norm_rope(y_ref, ss_ref, off, ss_off, g, cos, sin_signed, even_lane):
    outs = []
    for j in range(QK_WIDTH // LANES):
        cols = slice(LANES * j, LANES * (j + 1))
        ss = ss_ref[:, ss_off + LANES * j:ss_off + LANES * (j + 1)]
        xc = (y_ref[:, off + LANES * j:off + LANES * (j + 1)]
              * lax.rsqrt(ss * (1.0 / HEAD_DIM) + EPS) * g[:, cols])
        partner = jnp.where(even_lane, pltpu.roll(xc, LANES - 1, 1), pltpu.roll(xc, 1, 1))
        outs.append(xc * cos + partner * sin_signed)
    return jnp.concatenate(outs, axis=1)


def _inproj_kernel(x_ref, mod_ref, g_ref, w_hbm, bd_ref, qg_ref, kg_ref, cos_ref, sin_ref,
                   *refs, with_q):
    *out_refs, y_ref, ss_ref, w_ref, w_stage, w_sem = refs

    @pl.when(_first_step())
    def _():
        _stage_weight(w_hbm, w_ref, w_stage, w_sem, col0=0 if with_q else POOL_WIDTH + QK_WIDTH)

    d = x_ref.shape[-1]
    rows = y_ref.shape[1]
    mod = mod_ref[0]
    gain = g_ref[...] * (1.0 + mod[:, d:2 * d])
    shift = mod[:, 0:d]
    lane = lax.broadcasted_iota(jnp.int32, (rows, LANES), 1)
    even_lane = (lane % 2) == 0
    qk_off = POOL_WIDTH if with_q else 0
    qk_cols = ss_ref.shape[2]

    def project(j):
        x = x_ref[0, j * rows:(j + 1) * rows, :]
        ms = jnp.mean(x * x, axis=-1, keepdims=True)
        h = (x * lax.rsqrt(ms + EPS) * gain + shift).astype(BF16)
        y_ref[j % 2] = jnp.dot(h, w_ref[...], preferred_element_type=F32)
        for p in range(0, qk_cols, 256):
            t = y_ref[j % 2, :, qk_off + p:qk_off + p + 256]
            ss_ref[j % 2, :, p:p + 256] = jnp.dot((t * t).astype(BF16), bd_ref[...],
                                                  preferred_element_type=F32)

    def epilogue(j):
        blk = slice(j * rows, (j + 1) * rows)
        y = y_ref.at[j % 2]
        ss = ss_ref.at[j % 2]
        cos = cos_ref[blk, :]
        sin_signed = sin_ref[blk, :]
        if with_q:
            pool_ref, q_ref, k_ref, vt_ref = out_refs
            pool_ref[0, blk, :] = y[:, :POOL_WIDTH]
            q_ref[0, blk, :] = _qk_norm_rope(y, ss, qk_off, 0, qg_ref[...], cos, sin_signed,
                                             even_lane).astype(BF16)
            k_at = QK_WIDTH
        else:
            k_ref, vt_ref = out_refs
            k_at = 0
        k_ref[0, blk, :] = _qk_norm_rope(y, ss, qk_off + k_at, k_at, kg_ref[...], cos, sin_signed,
                                         even_lane).astype(BF16)
        vt_ref[0, 0, :, blk] = y[:, qk_off + k_at + QK_WIDTH:].T.astype(BF16)

    _software_pipeline(x_ref.shape[1] // rows, project, epilogue)


def _inproj_call(x, mods, mod_row, g, w, bd, qg, kg, cos, sin_signed, with_q):
    b, s, d = x.shape
    tm = min(TM_PROJ, s)
    nt = s // tm
    dv = N_HEADS * V_DIM
    n = w.shape[1] if with_q else QK_WIDTH + dv
    out_shape = [jax.ShapeDtypeStruct((b, s, QK_WIDTH), BF16),
                 jax.ShapeDtypeStruct((b, nt, dv, tm), BF16)]
    out_specs = [pl.BlockSpec((1, tm, QK_WIDTH), lambda bi, i: (bi, i, 0)),
                 pl.BlockSpec((1, 1, dv, tm), lambda bi, i: (bi, i, 0, 0))]
    if with_q:
        out_shape = [jax.ShapeDtypeStruct((b, s, POOL_WIDTH), F32),
                     jax.ShapeDtypeStruct((b, s, QK_WIDTH), BF16)] + out_shape
        out_specs = [pl.BlockSpec((1, tm, POOL_WIDTH), lambda bi, i: (bi, i, 0)),
                     pl.BlockSpec((1, tm, QK_WIDTH), lambda bi, i: (bi, i, 0))] + out_specs
    return pl.pallas_call(
        functools.partial(_inproj_kernel, with_q=with_q),
        grid=(b, nt),
        in_specs=[
            pl.BlockSpec((1, tm, d), lambda bi, i: (bi, i, 0)),
            pl.BlockSpec((1, 1, mods.shape[-1]), lambda bi, i: (mod_row(bi), 0, 0)),
            _const_spec((1, d)),
            _HBM,
            _const_spec(bd.shape),
            _const_spec(qg.shape),
            _const_spec(kg.shape),
            pl.BlockSpec((tm, LANES), lambda bi, i: (i, 0)),
            pl.BlockSpec((tm, LANES), lambda bi, i: (i, 0)),
        ],
        out_specs=out_specs,
        out_shape=out_shape,
        scratch_shapes=[pltpu.VMEM((2, min(ROWS_PROJ, tm), n), F32),
                        pltpu.VMEM((2, min(ROWS_PROJ, tm), n - dv - (POOL_WIDTH if with_q else 0)), F32),
                        *_stage_scratch(d, n)],
        compiler_params=_params(("arbitrary", "arbitrary")),
        name="inproj_q" if with_q else "inproj_ctx",
    )(x, mods, g, w, bd, qg, kg, cos, sin_signed)


def _attn_kernel(bound_ref, lam_ref, subg_ref, q_ref, k_ref, vt_ref, kc_ref, vct_ref, o_ref, st_ref, *,
                 lam_init):
    tq = q_ref.shape[1]
    n_chunks = vt_ref.shape[1]
    tk = vt_ref.shape[-1]

    lam_rows = lam_ref[...]
    a1 = jnp.sum(lam_rows[0:1] * lam_rows[1:2], axis=1, keepdims=True)
    a2 = jnp.sum(lam_rows[2:3] * lam_rows[3:4], axis=1, keepdims=True)
    lam = jnp.exp(a1) - jnp.exp(a2) + lam_init

    q = q_ref[0]
    lane = lax.broadcasted_iota(jnp.int32, q.shape, 1)
    zero = jnp.zeros_like(q)
    qq = jnp.concatenate([jnp.where(lane < HEAD_DIM, q, zero),
                          jnp.where(lane >= HEAD_DIM, q, zero)], axis=0)
    bounded = bound_ref[0] <= MAX_UNSHIFTED_SCORE

    def chunks():
        for j in range(0, kc_ref.shape[1], ATTN_KEYS):
            yield kc_ref[0, j:j + ATTN_KEYS, :], vct_ref[0, 0, :, j:j + ATTN_KEYS]
        for c in range(n_chunks):
            for j in range(0, tk, ATTN_KEYS):
                yield k_ref[0, c * tk + j:c * tk + j + ATTN_KEYS, :], vt_ref[0, c, :, j:j + ATTN_KEYS]

    def scores(kb):
        return lax.dot_general(kb, qq, (((1,), (1,)), ((), ())), preferred_element_type=F32)

    def finish(l, acc):
        ot = acc[:, :tq] / l[:, :tq] - lam * (acc[:, tq:] / l[:, tq:])
        o = ot.T
        ms = jnp.mean(o * o, axis=-1, keepdims=True)
        o_ref[0] = (o * lax.rsqrt(ms + EPS) * (subg_ref[...] * (1.0 - lam_init))).astype(BF16)

    @pl.when(bounded)
    def _():
        l = jnp.zeros((1, 2 * tq), F32)
        acc = jnp.zeros((V_DIM, 2 * tq), F32)
        work = list(chunks())
        st_ref[0, :work[0][0].shape[0], :] = scores(work[0][0])
        for c, (kb, vtb) in enumerate(work):
            if c + 1 < len(work):
                nxt = work[c + 1][0]
                st_ref[(c + 1) % 2, :nxt.shape[0], :] = scores(nxt)
            p = jnp.exp2(st_ref[c % 2, :kb.shape[0], :])
            l = l + jnp.sum(p, axis=0, keepdims=True)
            acc = acc + jnp.dot(vtb, p.astype(BF16), preferred_element_type=F32)
        finish(l, acc)

    @pl.when(jnp.logical_not(bounded))
    def _():
        m = jnp.full((1, 2 * tq), -jnp.inf, F32)
        l = jnp.zeros((1, 2 * tq), F32)
        acc = jnp.zeros((V_DIM, 2 * tq), F32)
        for kb, vtb in chunks():
            st = scores(kb)
            m_new = jnp.maximum(m, jnp.max(st, axis=0, keepdims=True))
            alpha = jnp.exp2(m - m_new)
            p = jnp.exp2(st - m_new)
            l = alpha * l + jnp.sum(p, axis=0, keepdims=True)
            acc = alpha * acc + jnp.dot(vtb, p.astype(BF16), preferred_element_type=F32)
            m = m_new
        finish(l, acc)


def _score_bound(q_gain, k_gain):
    slack = 1.0 + 2.0 ** -5
    return (HEAD_DIM * slack * jnp.max(jnp.abs(q_gain)) * jnp.max(jnp.abs(k_gain))).reshape(1)


def _attn_call(score_bound, lam_rows, sub_g, q, k, vt, kc, vct, lam_init):
    b, s, _ = q.shape
    tq = min(TQ, s)
    n_chunks, tk = vt.shape[1], vt.shape[3]
    ctx_len = kc.shape[1]
    return pl.pallas_call(
        functools.partial(_attn_kernel, lam_init=lam_init),
        grid=(b, N_HEADS, s // tq),
        in_specs=[
            pl.BlockSpec(memory_space=pltpu.SMEM),
            _const_spec(lam_rows.shape),
            _const_spec(sub_g.shape),
            pl.BlockSpec((1, tq, V_DIM), lambda bi, h, i: (bi, i, h)),
            pl.BlockSpec((1, s, V_DIM), lambda bi, h, i: (bi, 0, h)),
            pl.BlockSpec((1, n_chunks, V_DIM, tk), lambda bi, h, i: (bi, 0, h, 0)),
            pl.BlockSpec((1, ctx_len, V_DIM), lambda bi, h, i: (bi, 0, h)),
            pl.BlockSpec((1, 1, V_DIM, ctx_len), lambda bi, h, i: (bi, 0, h, 0)),
        ],
        out_specs=pl.BlockSpec((1, tq, V_DIM), lambda bi, h, i: (bi, i, h)),
        out_shape=jax.ShapeDtypeStruct((b, s, N_HEADS * V_DIM), BF16),
        scratch_shapes=[pltpu.VMEM((2, ATTN_KEYS, 2 * tq), F32)],
        compiler_params=_params(("parallel", "parallel", "arbitrary")),
        name="diff_attn",
    )(score_bound, lam_rows, sub_g, q, k, vt, kc, vct)


def _pool_merge_rows(x_ref, mod, pm_ref, pp_ref, pn_ref, attn_ref, pw_ref, ps_ref, wo_ref, ext_ref, seq):
    d = x_ref.shape[-1]
    tm = x_ref.shape[1]
    i = pl.program_id(1)
    last = pl.num_programs(1) - 1
    ext_ref[0:POOL_HALO, :] = jnp.where(i > 0, pp_ref[0], 0.0)
    ext_ref[POOL_HALO:POOL_HALO + tm, :] = pm_ref[0]
    ext_ref[POOL_HALO + tm:2 * POOL_HALO + tm, :] = jnp.where(i < last, pn_ref[0], 0.0)

    def window_sum(lo, hi, half):
        acc = None
        for off in range(lo, hi + 1):
            piece = ext_ref[pl.ds(POOL_HALO + off, tm), LANES * half:LANES * (half + 1)]
            acc = piece if acc is None else acc + piece
        return acc

    t = i * tm + lax.broadcasted_iota(jnp.int32, (tm, LANES), 0)
    lane = lax.broadcasted_iota(jnp.int32, (tm, LANES), 1)
    low_group = lane < POOL_GROUP
    pooled = []
    for half in range(2):
        w_small, w_big = POOL_WINDOWS[2 * half], POOL_WINDOWS[2 * half + 1]
        s_small = window_sum(-(w_small // 2), w_small // 2 - 1, half)
        s_big = (s_small + window_sum(-(w_big // 2), -(w_small // 2) - 1, half)
                 + window_sum(w_small // 2, w_big // 2 - 1, half))
        reach = jnp.where(low_group, w_small // 2, w_big // 2)
        cnt = jnp.minimum(t + reach, seq) - jnp.maximum(t - reach, 0)
        mean = jnp.where(low_group, s_small, s_big) / cnt.astype(F32)
        pooled.append(mean - pm_ref[0, :, LANES * half:LANES * (half + 1)])
    pooled = jnp.concatenate(pooled, axis=1).astype(BF16)
    pool_y = (jnp.dot(pooled, pw_ref[...], preferred_element_type=F32) * ps_ref[...]).astype(BF16)
    y = (jnp.dot(pool_y, wo_ref[0:POOL_WIDTH, :], preferred_element_type=F32)
         + jnp.dot(attn_ref[0], wo_ref[POOL_WIDTH:, :], preferred_element_type=F32))
    return x_ref[0] + mod[:, 2 * d:3 * d] * y


def _ffn_rows(x, mod, g, w1_ref, w2_ref):
    d = x.shape[-1]
    h = _norm_mod(x, g, mod[:, 3 * d:4 * d], mod[:, 4 * d:5 * d]).astype(BF16)
    a = jnp.maximum(jnp.dot(h, w1_ref[...], preferred_element_type=F32), 0.0)
    a = (a * a).astype(BF16)
    y = jnp.dot(a, w2_ref[...], preferred_element_type=F32)
    return x + mod[:, 5 * d:6 * d] * y


def _ffn_blocks(x_block, o_ref, mod, g, w1_ref, w2_ref):
    tm = o_ref.shape[1]
    rows = min(ROWS_FFN, tm)
    for r in range(tm // rows):
        blk = slice(r * rows, (r + 1) * rows)
        o_ref[0, blk, :] = _ffn_rows(x_block(blk), mod, g, w1_ref, w2_ref)


def _ffn_kernel(x_ref, mod_ref, g_ref, w1_hbm, w2_hbm, o_ref,
                w1_ref, w1_stage, w1_sem, w2_ref, w2_stage, w2_sem):
    @pl.when(_first_step())
    def _():
        _stage_weight(w1_hbm, w1_ref, w1_stage, w1_sem)
        _stage_weight(w2_hbm, w2_ref, w2_stage, w2_sem)

    _ffn_blocks(lambda blk: x_ref[0, blk, :], o_ref, mod_ref[0], g_ref[...], w1_ref, w2_ref)


def _merge_ffn_kernel(x_ref, mod_ref, pm_ref, pp_ref, pn_ref, attn_ref, pw_ref, ps_ref, wo_hbm,
                      g_ref, w1_hbm, w2_hbm, o_ref, ext_ref,
                      wo_ref, wo_stage, wo_sem, w1_ref, w1_stage, w1_sem, w2_ref, w2_stage, w2_sem, *, seq):
    @pl.when(_first_step())
    def _():
        _stage_weight(wo_hbm, wo_ref, wo_stage, wo_sem)
        _stage_weight(w1_hbm, w1_ref, w1_stage, w1_sem)
        _stage_weight(w2_hbm, w2_ref, w2_stage, w2_sem)

    mod = mod_ref[0]
    x1 = _pool_merge_rows(x_ref, mod, pm_ref, pp_ref, pn_ref, attn_ref, pw_ref, ps_ref, wo_ref, ext_ref, seq)
    _ffn_blocks(lambda blk: x1[blk, :], o_ref, mod, g_ref[...], w1_ref, w2_ref)


def _merge_ffn_call(x, mods, mod_row, pool_in, attn, pool_w_bd, pool_scale, w_out, g, w1, w2):
    b, s, d = x.shape
    tm = min(TM_FFN, s)
    hb = tm // POOL_HALO
    n_halo = s // POOL_HALO
    return pl.pallas_call(
        functools.partial(_merge_ffn_kernel, seq=s),
        grid=(b, s // tm),
        in_specs=[
            pl.BlockSpec((1, tm, d), lambda bi, i: (bi, i, 0)),
            pl.BlockSpec((1, 1, mods.shape[-1]), lambda bi, i: (mod_row(bi), 0, 0)),
            pl.BlockSpec((1, tm, POOL_WIDTH), lambda bi, i: (bi, i, 0)),
            pl.BlockSpec((1, POOL_HALO, POOL_WIDTH),
                         lambda bi, i: (bi, jnp.maximum(i * hb - 1, 0), 0)),
            pl.BlockSpec((1, POOL_HALO, POOL_WIDTH),
                         lambda bi, i: (bi, jnp.minimum((i + 1) * hb, n_halo - 1), 0)),
            pl.BlockSpec((1, tm, attn.shape[-1]), lambda bi, i: (bi, i, 0)),
            _const_spec(pool_w_bd.shape),
            _const_spec(pool_scale.shape),
            _HBM,
            _const_spec((1, d)),
            _HBM,
            _HBM,
        ],
        out_specs=pl.BlockSpec((1, tm, d), lambda bi, i: (bi, i, 0)),
        out_shape=jax.ShapeDtypeStruct((b, s, d), F32),
        scratch_shapes=[pltpu.VMEM((tm + 2 * POOL_HALO, POOL_WIDTH), F32), *_stage_scratch(*w_out.shape),
                        *_stage_scratch(*w1.shape), *_stage_scratch(*w2.shape)],
        compiler_params=_params(("arbitrary", "arbitrary")),
        name="pool_merge_ffn",
    )(x, mods, pool_in, pool_in, pool_in, attn, pool_w_bd, pool_scale, w_out, g, w1, w2)


def _ffn_call(x, mods, mod_row, g, w1, w2):
    b, s, d = x.shape
    tm = min(TM_FFN, s)
    return pl.pallas_call(
        _ffn_kernel,
        grid=(b, s // tm),
        in_specs=[
            pl.BlockSpec((1, tm, d), lambda bi, i: (bi, i, 0)),
            pl.BlockSpec((1, 1, mods.shape[-1]), lambda bi, i: (mod_row(bi), 0, 0)),
            _const_spec((1, d)),
            _HBM,
            _HBM,
        ],
        out_specs=pl.BlockSpec((1, tm, d), lambda bi, i: (bi, i, 0)),
        out_shape=jax.ShapeDtypeStruct((b, s, d), F32),
        scratch_shapes=[*_stage_scratch(*w1.shape), *_stage_scratch(*w2.shape)],
        compiler_params=_params(("arbitrary", "arbitrary")),
        name="ffn",
    )(x, mods, g, w1, w2)


def _sgmlp_kernel(x_ref, mod_ref, g_ref, win_hbm, lng_ref, lnb_ref, sgw_ref, sgb_ref, wout_hbm,
                  o_ref, a_ref, win_ref, win_stage, win_sem, wout_ref, wout_stage, wout_sem):
    @pl.when(_first_step())
    def _():
        _stage_weight(win_hbm, win_ref, win_stage, win_sem)
        _stage_weight(wout_hbm, wout_ref, wout_stage, wout_sem)

    d = x_ref.shape[-1]
    rows = a_ref.shape[1]
    width = win_ref.shape[1] // 2
    mod = mod_ref[0]
    gain = g_ref[...] * (1.0 + mod[:, d:2 * d])
    shift = mod[:, 0:d]
    gate = mod[:, 2 * d:3 * d]

    def project(j):
        x = x_ref[0, j * rows:(j + 1) * rows, :]
        ms = jnp.mean(x * x, axis=-1, keepdims=True)
        h = (x * lax.rsqrt(ms + EPS) * gain + shift).astype(BF16)
        a_ref[j % 2] = jnp.dot(h, win_ref[...], preferred_element_type=F32)

    def gelu(t):
        return 0.5 * t * (1.0 + lax.erf(t * (1.0 / math.sqrt(2.0))))

    def epilogue(j):
        blk = slice(j * rows, (j + 1) * rows)
        a = a_ref.at[j % 2]
        v = gelu(a[:, width:])
        mu = jnp.mean(v, axis=-1, keepdims=True)
        vc = v - mu
        var = jnp.mean(vc * vc, axis=-1, keepdims=True)
        vn = (vc * lax.rsqrt(var + EPS) * lng_ref[...] + lnb_ref[...]).astype(BF16)
        mixed = []
        for r in range(rows // CHUNK):
            cols = [jnp.dot(sgw_ref[gi],
                            vn[r * CHUNK:(r + 1) * CHUNK, gi * SG_GROUP_DIM:(gi + 1) * SG_GROUP_DIM],
                            preferred_element_type=F32) for gi in range(SG_GROUPS)]
            mixed.append(jnp.concatenate(cols, axis=1) + sgb_ref[...])
        z = (gelu(a[:, :width]) * jnp.concatenate(mixed, axis=0)).astype(BF16)
        y = jnp.dot(z, wout_ref[...], preferred_element_type=F32)
        o_ref[0, blk, :] = x_ref[0, blk, :] + gate * y

    _software_pipeline(x_ref.shape[1] // rows, project, epilogue)


def _sgmlp_call(x, mods, mod_row, g, w_in, ln_g, ln_b, sg_w, sg_b_full, w_out):
    b, s, d = x.shape
    tm = min(TM_SG, s)
    return pl.pallas_call(
        _sgmlp_kernel,
        grid=(b, s // tm),
        in_specs=[
            pl.BlockSpec((1, tm, d), lambda bi, i: (bi, i, 0)),
            pl.BlockSpec((1, 1, mods.shape[-1]), lambda bi, i: (mod_row(bi), 0, 0)),
            _const_spec((1, d)),
            _HBM,
            _const_spec(ln_g.shape),
            _const_spec(ln_b.shape),
            _const_spec(sg_w.shape),
            _const_spec(sg_b_full.shape),
            _HBM,
        ],
        out_specs=pl.BlockSpec((1, tm, d), lambda bi, i: (bi, i, 0)),
        out_shape=jax.ShapeDtypeStruct((b, s, d), F32),
        scratch_shapes=[pltpu.VMEM((2, min(ROWS_SG, tm), w_in.shape[1]), F32),
                        *_stage_scratch(*w_in.shape), *_stage_scratch(*w_out.shape)],
        compiler_params=_params(("arbitrary", "arbitrary")),
        name="sg_mlp",
    )(x, mods, g, w_in, ln_g, ln_b, sg_w, sg_b_full, w_out)


def _rope_tables(n_tok):
    t = jnp.arange(n_tok, dtype=jnp.int32)
    rows = (t // GRID_W).astype(F32)
    cols = (t % GRID_W).astype(F32)
    pairs = HEAD_DIM // 4
    inv = ROPE_THETA ** (-jnp.arange(pairs, dtype=F32) / pairs)
    ang = jnp.concatenate([rows[:, None] * inv, cols[:, None] * inv], axis=-1)
    cos, sin = jnp.cos(ang), jnp.sin(ang)
    cos2 = jnp.repeat(cos, 2, axis=-1)
    sin2 = jnp.stack([-sin, sin], axis=-1).reshape(n_tok, HEAD_DIM)
    return jnp.tile(cos2, (1, LANES // HEAD_DIM)), jnp.tile(sin2, (1, LANES // HEAD_DIM))


def _block_diag(blocks):
    n = len(blocks)
    rows = []
    for i, blk in enumerate(blocks):
        z = jnp.zeros_like(blk)
        rows.append(jnp.concatenate([blk if j == i else z for j in range(n)], axis=1))
    return jnp.concatenate(rows, axis=0)


def kernel(x, c, ctx, c_ctx, ada_w, ada_b, norm_mix_g, norm_ffn_g, ffn_w1, ffn_w2, ev_w_in, ev_w_out,
           pool_w, pool_scale, q_norm_g, k_norm_g, lam_q1, lam_k1, lam_q2, lam_k2, sub_norm_g,
           od_w_in, sg_ln_g, sg_ln_b, sg_w, sg_b, od_w_out):
    b, s, d = x.shape
    depth = ada_w.shape[0]
    assert depth == 2 and b <= SUBLANES - 1

    cin = jnp.zeros((SUBLANES, d), F32).at[:b].set(c).at[b].set(c_ctx)
    mods = _ada_call(cin, ada_w, ada_b).reshape(depth * SUBLANES, 1, 6 * d)

    def lat_row(layer):
        return lambda bi: layer * SUBLANES + bi

    def ctx_row(layer):
        return lambda bi: layer * SUBLANES + b

    w_in = ev_w_in[0]
    n_groups = QK_WIDTH // HEAD_DIM
    qg = jnp.tile(q_norm_g[0] * (HEAD_DIM ** -0.5 * math.log2(math.e)), n_groups)[None, :]
    kg = jnp.tile(k_norm_g[0], n_groups)[None, :]
    ones = jnp.ones((HEAD_DIM, HEAD_DIM), BF16)
    bd = _block_diag([ones] * (256 // HEAD_DIM))
    cos, sin_signed = _rope_tables(s)
    ctx_len = ctx.shape[1]
    cos_ctx = jnp.ones((ctx_len, LANES), F32)
    sin_ctx = jnp.zeros((ctx_len, LANES), F32)
    g_mix0 = norm_mix_g[0][None, :]

    pool_in, q, k, vt = _inproj_call(x, mods, lat_row(0), g_mix0, w_in, bd, qg, kg, cos, sin_signed, True)
    kc, vct = _inproj_call(ctx, mods, ctx_row(0), g_mix0, w_in, bd, qg, kg, cos_ctx, sin_ctx, False)

    lam_init = 0.8 - 0.6 * math.exp(-0.3 * 0)
    lam_rows = jnp.zeros((SUBLANES, LANES), F32)
    for r, vec in enumerate((lam_q1[0], lam_k1[0], lam_q2[0], lam_k2[0])):
        lam_rows = lam_rows.at[r, :HEAD_DIM].set(vec)
    attn = _attn_call(_score_bound(qg, kg), lam_rows, sub_norm_g[0][None, :], q, k, vt, kc, vct, lam_init)

    pool_w_bd = _block_diag([pool_w[0, gi] for gi in range(len(POOL_WINDOWS))]).astype(BF16)
    x = _merge_ffn_call(x, mods, lat_row(0), pool_in, attn, pool_w_bd, pool_scale[0][None, :],
                        ev_w_out[0], norm_ffn_g[0][None, :], ffn_w1[0], ffn_w2[0])

    sg_b_full = jnp.repeat(sg_b[0].T, SG_GROUP_DIM, axis=1)
    x = _sgmlp_call(x, mods, lat_row(1), norm_mix_g[1][None, :], od_w_in[0],
                    sg_ln_g[0][None, :], sg_ln_b[0][None, :], sg_w[0].astype(BF16), sg_b_full, od_w_out[0])
    x = _ffn_call(x, mods, lat_row(1), norm_ffn_g[1][None, :], ffn_w1[1], ffn_w2[1])
    return x
```

```python
import functools
import math

import jax
import jax.numpy as jnp
from jax import lax
from jax.experimental import pallas as pl
from jax.experimental.pallas import tpu as pltpu

F32 = jnp.float32
BF16 = jnp.bfloat16

EPS = 1e-6
GRID_W = 64
ROPE_THETA = 10000.0
POOL_WINDOWS = (2, 4, 8, 16)
POOL_GROUP = 64
POOL_WIDTH = 256
HEAD_DIM = 64
V_DIM = 128
N_HEADS = 6
QK_WIDTH = N_HEADS * 2 * HEAD_DIM
CHUNK = 128
SG_GROUPS = 8
SG_GROUP_DIM = 256

LANES = 128
SUBLANES = 8
POOL_HALO = 8
VMEM_LIMIT = 56 * 1024 * 1024
STAGE_BYTES = 2 * 1024 * 1024

MAX_UNSHIFTED_SCORE = 60.0

TM_PROJ = 1024
ROWS_PROJ = 256
TQ = 512
ATTN_KEYS = 512
TM_FFN = 512
ROWS_FFN = 256
TM_SG = 1024
ROWS_SG = 256


def _params(sem):
    return pltpu.CompilerParams(dimension_semantics=sem, vmem_limit_bytes=VMEM_LIMIT)


def _const_spec(shape):
    zeros = (0,) * len(shape)
    return pl.BlockSpec(shape, lambda *_: zeros, pipeline_mode=pl.Buffered(1))


def _norm_mod(x, g, shift, scale):
    ms = jnp.mean(x * x, axis=-1, keepdims=True)
    return x * lax.rsqrt(ms + EPS) * (g * (1.0 + scale)) + shift


def _software_pipeline(n, produce, consume):
    produce(0)
    for j in range(n):
        if j + 1 < n:
            produce(j + 1)
        consume(j)


def _first_step():
    return jnp.logical_and(pl.program_id(0) == 0, pl.program_id(1) == 0)


def _stage_weight(w_hbm, layer, dst_ref, stage_ref, sem, col0=0):
    k, n = dst_ref.shape
    slab = stage_ref.shape[1]
    n_slabs = k // slab

    def copy(i):
        return pltpu.make_async_copy(w_hbm.at[layer, pl.ds(i * slab, slab), pl.ds(col0, n)],
                                     stage_ref.at[i % 2], sem.at[i % 2])

    copy(0).start()
    for i in range(n_slabs):
        if i + 1 < n_slabs:
            copy(i + 1).start()
        copy(i).wait()
        dst_ref[i * slab:(i + 1) * slab, :] = stage_ref[i % 2].astype(BF16)


def _stage_scratch(k, n):
    slab = max(r for r in range(SUBLANES, k + 1, SUBLANES) if k % r == 0 and (r == SUBLANES or 4 * r * n <= STAGE_BYTES))
    return [pltpu.VMEM((k, n), BF16), pltpu.VMEM((2, slab, n), F32), pltpu.SemaphoreType.DMA((2,))]


_HBM = pl.BlockSpec(memory_space=pl.ANY)


def _ada_kernel(c_ref, w_ref, b_ref, o_ref):
    c = c_ref[...]
    sc = (c * jax.nn.sigmoid(c)).astype(BF16)
    o_ref[0] = jnp.dot(sc, w_ref[0].astype(BF16), preferred_element_type=F32) + b_ref[0]


def _ada_call(cin, ada_w, ada_b):
    depth, d, n = ada_w.shape
    tn = 1536
    return pl.pallas_call(
        _ada_kernel,
        grid=(depth, n // tn),
        in_specs=[
            pl.BlockSpec((SUBLANES, d), lambda l, j: (0, 0)),
            pl.BlockSpec((1, d, tn), lambda l, j: (l, 0, j)),
            pl.BlockSpec((1, 1, tn), lambda l, j: (l, 0, j)),
        ],
        out_specs=pl.BlockSpec((1, SUBLANES, tn), lambda l, j: (l, 0, j)),
        out_shape=jax.ShapeDtypeStruct((depth, SUBLANES, n), F32),
        compiler_params=_params(("parallel", "parallel")),
        name="ada_ln",
    )(cin, ada_w, ada_b.reshape(depth, 1, n))


def _qk_norm_rope(y_ref, ss_ref, off, ss_off, g, cos, sin_signed, even_lane):
    outs = []
    for j in range(QK_WIDTH // LANES):
        cols = slice(LANES * j, LANES * (j + 1))
        ss = ss_ref[:, ss_off + LANES * j:ss_off + LANES * (j + 1)]
        xc = (y_ref[:, off + LANES * j:off + LANES * (j + 1)]
              * lax.rsqrt(ss * (1.0 / HEAD_DIM) + EPS) * g[:, cols])
        partner = jnp.where(even_lane, pltpu.roll(xc, LANES - 1, 1), pltpu.roll(xc, 1, 1))
        outs.append(xc * cos + partner * sin_signed)
    return jnp.concatenate(outs, axis=1)


def _inproj_kernel(x_ref, mod_ref, g_ref, w_hbm, bd_ref, qg_ref, kg_ref, cos_ref, sin_ref,
                   *refs, with_q):
    *out_refs, y_ref, ss_ref, w_ref, w_stage, w_sem = refs

    @pl.when(_first_step())
    def _():
        _stage_weight(w_hbm, 0, w_ref, w_stage, w_sem, col0=0 if with_q else POOL_WIDTH + QK_WIDTH)

    d = x_ref.shape[-1]
    rows = y_ref.shape[1]
    mod = mod_ref[0]
    gain = g_ref[...] * (1.0 + mod[:, d:2 * d])
    shift = mod[:, 0:d]
    lane = lax.broadcasted_iota(jnp.int32, (rows, LANES), 1)
    even_lane = (lane % 2) == 0
    qk_off = POOL_WIDTH if with_q else 0
    qk_cols = ss_ref.shape[2]

    def project(j):
        x = x_ref[0, j * rows:(j + 1) * rows, :]
        ms = jnp.mean(x * x, axis=-1, keepdims=True)
        h = (x * lax.rsqrt(ms + EPS) * gain + shift).astype(BF16)
        y_ref[j % 2] = jnp.dot(h, w_ref[...], preferred_element_type=F32)
        for p in range(0, qk_cols, 256):
            t = y_ref[j % 2, :, qk_off + p:qk_off + p + 256]
            ss_ref[j % 2, :, p:p + 256] = jnp.dot((t * t).astype(BF16), bd_ref[...],
                                                  preferred_element_type=F32)

    def epilogue(j):
        blk = slice(j * rows, (j + 1) * rows)
        y = y_ref.at[j % 2]
        ss = ss_ref.at[j % 2]
        cos = cos_ref[blk, :]
        sin_signed = sin_ref[blk, :]
        if with_q:
            pool_ref, q_ref, k_ref, vt_ref = out_refs
            pool_ref[0, blk, :] = y[:, :POOL_WIDTH]
            q_ref[0, blk, :] = _qk_norm_rope(y, ss, qk_off, 0, qg_ref[...], cos, sin_signed,
                                             even_lane).astype(BF16)
            k_at = QK_WIDTH
        else:
            k_ref, vt_ref = out_refs
            k_at = 0
        k_ref[0, blk, :] = _qk_norm_rope(y, ss, qk_off + k_at, k_at, kg_ref[...], cos, sin_signed,
                                         even_lane).astype(BF16)
        vt_ref[0, 0, :, blk] = y[:, qk_off + k_at + QK_WIDTH:].T.astype(BF16)

    _software_pipeline(x_ref.shape[1] // rows, project, epilogue)


def _inproj_call(x, mods, mod_row, g, w, bd, qg, kg, cos, sin_signed, with_q):
    b, s, d = x.shape
    tm = min(TM_PROJ, s)
    nt = s // tm
    dv = N_HEADS * V_DIM
    n = w.shape[2] if with_q else QK_WIDTH + dv
    out_shape = [jax.ShapeDtypeStruct((b, s, QK_WIDTH), BF16),
                 jax.ShapeDtypeStruct((b, nt, dv, tm), BF16)]
    out_specs = [pl.BlockSpec((1, tm, QK_WIDTH), lambda bi, i: (bi, i, 0)),
                 pl.BlockSpec((1, 1, dv, tm), lambda bi, i: (bi, i, 0, 0))]
    if with_q:
        out_shape = [jax.ShapeDtypeStruct((b, s, POOL_WIDTH), F32),
                     jax.ShapeDtypeStruct((b, s, QK_WIDTH), BF16)] + out_shape
        out_specs = [pl.BlockSpec((1, tm, POOL_WIDTH), lambda bi, i: (bi, i, 0)),
                     pl.BlockSpec((1, tm, QK_WIDTH), lambda bi, i: (bi, i, 0))] + out_specs
    return pl.pallas_call(
        functools.partial(_inproj_kernel, with_q=with_q),
        grid=(b, nt),
        in_specs=[
            pl.BlockSpec((1, tm, d), lambda bi, i: (bi, i, 0)),
            pl.BlockSpec((1, 1, mods.shape[-1]), lambda bi, i: (mod_row(bi), 0, 0)),
            _const_spec((1, d)),
            _HBM,
            _const_spec(bd.shape),
            _const_spec(qg.shape),
            _const_spec(kg.shape),
            pl.BlockSpec((tm, LANES), lambda bi, i: (i, 0)),
            pl.BlockSpec((tm, LANES), lambda bi, i: (i, 0)),
        ],
        out_specs=out_specs,
        out_shape=out_shape,
        scratch_shapes=[pltpu.VMEM((2, min(ROWS_PROJ, tm), n), F32),
                        pltpu.VMEM((2, min(ROWS_PROJ, tm), n - dv - (POOL_WIDTH if with_q else 0)), F32),
                        *_stage_scratch(d, n)],
        compiler_params=_params(("arbitrary", "arbitrary")),
        name="inproj_q" if with_q else "inproj_ctx",
    )(x, mods, g, w, bd, qg, kg, cos, sin_signed)


def _attn_kernel(bound_ref, lam_ref, subg_ref, q_ref, k_ref, vt_ref, kc_ref, vct_ref, o_ref, st_ref, *,
                 lam_init):
    tq = q_ref.shape[1]
    n_chunks = vt_ref.shape[1]
    tk = vt_ref.shape[-1]

    lam_rows = lam_ref[...]
    a1 = jnp.sum(lam_rows[0:1] * lam_rows[1:2], axis=1, keepdims=True)
    a2 = jnp.sum(lam_rows[2:3] * lam_rows[3:4], axis=1, keepdims=True)
    lam = jnp.exp(a1) - jnp.exp(a2) + lam_init

    q = q_ref[0]
    lane = lax.broadcasted_iota(jnp.int32, q.shape, 1)
    zero = jnp.zeros_like(q)
    qq = jnp.concatenate([jnp.where(lane < HEAD_DIM, q, zero),
                          jnp.where(lane >= HEAD_DIM, q, zero)], axis=0)
    bounded = bound_ref[0] <= MAX_UNSHIFTED_SCORE

    def chunks():
        for j in range(0, kc_ref.shape[1], ATTN_KEYS):
            yield kc_ref[0, j:j + ATTN_KEYS, :], vct_ref[0, 0, :, j:j + ATTN_KEYS]
        for c in range(n_chunks):
            for j in range(0, tk, ATTN_KEYS):
                yield k_ref[0, c * tk + j:c * tk + j + ATTN_KEYS, :], vt_ref[0, c, :, j:j + ATTN_KEYS]

    def scores(kb):
        return lax.dot_general(kb, qq, (((1,), (1,)), ((), ())), preferred_element_type=F32)

    def finish(l, acc):
        ot = acc[:, :tq] / l[:, :tq] - lam * (acc[:, tq:] / l[:, tq:])
        o = ot.T
        ms = jnp.mean(o * o, axis=-1, keepdims=True)
        o_ref[0] = (o * lax.rsqrt(ms + EPS) * (subg_ref[...] * (1.0 - lam_init))).astype(BF16)

    @pl.when(bounded)
    def _():
        l = jnp.zeros((1, 2 * tq), F32)
        acc = jnp.zeros((V_DIM, 2 * tq), F32)
        work = list(chunks())
        st_ref[0, :work[0][0].shape[0], :] = scores(work[0][0])
        for c, (kb, vtb) in enumerate(work):
            if c + 1 < len(work):
                nxt = work[c + 1][0]
                st_ref[(c + 1) % 2, :nxt.shape[0], :] = scores(nxt)
            p = jnp.exp2(st_ref[c % 2, :kb.shape[0], :])
            l = l + jnp.sum(p, axis=0, keepdims=True)
            acc = acc + jnp.dot(vtb, p.astype(BF16), preferred_element_type=F32)
        finish(l, acc)

    @pl.when(jnp.logical_not(bounded))
    def _():
        m = jnp.full((1, 2 * tq), -jnp.inf, F32)
        l = jnp.zeros((1, 2 * tq), F32)
        acc = jnp.zeros((V_DIM, 2 * tq), F32)
        for kb, vtb in chunks():
            st = scores(kb)
            m_new = jnp.maximum(m, jnp.max(st, axis=0, keepdims=True))
            alpha = jnp.exp2(m - m_new)
            p = jnp.exp2(st - m_new)
            l = alpha * l + jnp.sum(p, axis=0, keepdims=True)
            acc = alpha * acc + jnp.dot(vtb, p.astype(BF16), preferred_element_type=F32)
            m = m_new
        finish(l, acc)


def _score_bound(q_gain, k_gain):
    slack = 1.0 + 2.0 ** -5
    return (HEAD_DIM * slack * jnp.max(jnp.abs(q_gain)) * jnp.max(jnp.abs(k_gain))).reshape(1)


def _attn_call(score_bound, lam_rows, sub_g, q, k, vt, kc, vct, lam_init):
    b, s, _ = q.shape
    tq = min(TQ, s)
    n_chunks, tk = vt.shape[1], vt.shape[3]
    ctx_len = kc.shape[1]
    return pl.pallas_call(
        functools.partial(_attn_kernel, lam_init=lam_init),
        grid=(b, N_HEADS, s // tq),
        in_specs=[
            pl.BlockSpec(memory_space=pltpu.SMEM),
            _const_spec(lam_rows.shape),
            _const_spec(sub_g.shape),
            pl.BlockSpec((1, tq, V_DIM), lambda bi, h, i: (bi, i, h)),
            pl.BlockSpec((1, s, V_DIM), lambda bi, h, i: (bi, 0, h)),
            pl.BlockSpec((1, n_chunks, V_DIM, tk), lambda bi, h, i: (bi, 0, h, 0)),
            pl.BlockSpec((1, ctx_len, V_DIM), lambda bi, h, i: (bi, 0, h)),
            pl.BlockSpec((1, 1, V_DIM, ctx_len), lambda bi, h, i: (bi, 0, h, 0)),
        ],
        out_specs=pl.BlockSpec((1, tq, V_DIM), lambda bi, h, i: (bi, i, h)),
        out_shape=jax.ShapeDtypeStruct((b, s, N_HEADS * V_DIM), BF16),
        scratch_shapes=[pltpu.VMEM((2, ATTN_KEYS, 2 * tq), F32)],
        compiler_params=_params(("parallel", "parallel", "arbitrary")),
        name="diff_attn",
    )(score_bound, lam_rows, sub_g, q, k, vt, kc, vct)


def _pool_merge_rows(x_ref, mod, pm_ref, pp_ref, pn_ref, attn_ref, pw_ref, ps_ref, wo_ref, ext_ref, seq):
    d = x_ref.shape[-1]
    tm = x_ref.shape[1]
    i = pl.program_id(1)
    last = pl.num_programs(1) - 1
    ext_ref[0:POOL_HALO, :] = jnp.where(i > 0, pp_ref[0], 0.0)
    ext_ref[POOL_HALO:POOL_HALO + tm, :] = pm_ref[0]
    ext_ref[POOL_HALO + tm:2 * POOL_HALO + tm, :] = jnp.where(i < last, pn_ref[0], 0.0)

    def window_sum(lo, hi, half):
        acc = None
        for off in range(lo, hi + 1):
            piece = ext_ref[pl.ds(POOL_HALO + off, tm), LANES * half:LANES * (half + 1)]
            acc = piece if acc is None else acc + piece
        return acc

    t = i * tm + lax.broadcasted_iota(jnp.int32, (tm, LANES), 0)
    lane = lax.broadcasted_iota(jnp.int32, (tm, LANES), 1)
    low_group = lane < POOL_GROUP
    pooled = []
    for half in range(2):
        w_small, w_big = POOL_WINDOWS[2 * half], POOL_WINDOWS[2 * half + 1]
        s_small = window_sum(-(w_small // 2), w_small // 2 - 1, half)
        s_big = (s_small + window_sum(-(w_big // 2), -(w_small // 2) - 1, half)
                 + window_sum(w_small // 2, w_big // 2 - 1, half))
        reach = jnp.where(low_group, w_small // 2, w_big // 2)
        cnt = jnp.minimum(t + reach, seq) - jnp.maximum(t - reach, 0)
        mean = jnp.where(low_group, s_small, s_big) / cnt.astype(F32)
        pooled.append(mean - pm_ref[0, :, LANES * half:LANES * (half + 1)])
    pooled = jnp.concatenate(pooled, axis=1).astype(BF16)
    pool_y = (jnp.dot(pooled, pw_ref[...], preferred_element_type=F32) * ps_ref[...]).astype(BF16)
    y = (jnp.dot(pool_y, wo_ref[0:POOL_WIDTH, :], preferred_element_type=F32)
         + jnp.dot(attn_ref[0], wo_ref[POOL_WIDTH:, :], preferred_element_type=F32))
    return x_ref[0] + mod[:, 2 * d:3 * d] * y


def _ffn_rows(x, mod, g, w1_ref, w2_ref):
    d = x.shape[-1]
    h = _norm_mod(x, g, mod[:, 3 * d:4 * d], mod[:, 4 * d:5 * d]).astype(BF16)
    a = jnp.maximum(jnp.dot(h, w1_ref[...], preferred_element_type=F32), 0.0)
    a = (a * a).astype(BF16)
    y = jnp.dot(a, w2_ref[...], preferred_element_type=F32)
    return x + mod[:, 5 * d:6 * d] * y


def _ffn_blocks(x_block, o_ref, mod, g, w1_ref, w2_ref):
    tm = o_ref.shape[1]
    rows = min(ROWS_FFN, tm)
    for r in range(tm // rows):
        blk = slice(r * rows, (r + 1) * rows)
        o_ref[0, blk, :] = _ffn_rows(x_block(blk), mod, g, w1_ref, w2_ref)


def _ffn_kernel(x_ref, mod_ref, g_ref, w1_hbm, w2_hbm, o_ref,
                w1_ref, w1_stage, w1_sem, w2_ref, w2_stage, w2_sem, *, layer):
    @pl.when(_first_step())
    def _():
        _stage_weight(w1_hbm, layer, w1_ref, w1_stage, w1_sem)
        _stage_weight(w2_hbm, layer, w2_ref, w2_stage, w2_sem)

    _ffn_blocks(lambda blk: x_ref[0, blk, :], o_ref, mod_ref[0], g_ref[...], w1_ref, w2_ref)


def _merge_ffn_kernel(x_ref, mod_ref, pm_ref, pp_ref, pn_ref, attn_ref, pw_ref, ps_ref, wo_hbm,
                      g_ref, w1_hbm, w2_hbm, o_ref, ext_ref,
                      wo_ref, wo_stage, wo_sem, w1_ref, w1_stage, w1_sem, w2_ref, w2_stage, w2_sem, *,
                      seq, layer):
    @pl.when(_first_step())
    def _():
        _stage_weight(wo_hbm, layer // 2, wo_ref, wo_stage, wo_sem)
        _stage_weight(w1_hbm, layer, w1_ref, w1_stage, w1_sem)
        _stage_weight(w2_hbm, layer, w2_ref, w2_stage, w2_sem)

    mod = mod_ref[0]
    x1 = _pool_merge_rows(x_ref, mod, pm_ref, pp_ref, pn_ref, attn_ref, pw_ref, ps_ref, wo_ref, ext_ref, seq)
    _ffn_blocks(lambda blk: x1[blk, :], o_ref, mod, g_ref[...], w1_ref, w2_ref)


def _merge_ffn_call(x, mods, mod_row, pool_in, attn, pool_w_bd, pool_scale, w_out, g, w1, w2, layer):
    b, s, d = x.shape
    tm = min(TM_FFN, s)
    hb = tm // POOL_HALO
    n_halo = s // POOL_HALO
    return pl.pallas_call(
        functools.partial(_merge_ffn_kernel, seq=s, layer=layer),
        grid=(b, s // tm),
        in_specs=[
            pl.BlockSpec((1, tm, d), lambda bi, i: (bi, i, 0)),
            pl.BlockSpec((1, 1, mods.shape[-1]), lambda bi, i: (mod_row(bi), 0, 0)),
            pl.BlockSpec((1, tm, POOL_WIDTH), lambda bi, i: (bi, i, 0)),
            pl.BlockSpec((1, POOL_HALO, POOL_WIDTH),
                         lambda bi, i: (bi, jnp.maximum(i * hb - 1, 0), 0)),
            pl.BlockSpec((1, POOL_HALO, POOL_WIDTH),
                         lambda bi, i: (bi, jnp.minimum((i + 1) * hb, n_halo - 1), 0)),
            pl.BlockSpec((1, tm, attn.shape[-1]), lambda bi, i: (bi, i, 0)),
            _const_spec(pool_w_bd.shape),
            _const_spec(pool_scale.shape),
            _HBM,
            _const_spec((1, d)),
            _HBM,
            _HBM,
        ],
        out_specs=pl.BlockSpec((1, tm, d), lambda bi, i: (bi, i, 0)),
        out_shape=jax.ShapeDtypeStruct((b, s, d), F32),
        scratch_shapes=[pltpu.VMEM((tm + 2 * POOL_HALO, POOL_WIDTH), F32), *_stage_scratch(*w_out.shape[1:]),
                        *_stage_scratch(*w1.shape[1:]), *_stage_scratch(*w2.shape[1:])],
        compiler_params=_params(("arbitrary", "arbitrary")),
        name="pool_merge_ffn",
    )(x, mods, pool_in, pool_in, pool_in, attn, pool_w_bd, pool_scale, w_out, g, w1, w2)


def _ffn_call(x, mods, mod_row, g, w1, w2, layer):
    b, s, d = x.shape
    tm = min(TM_FFN, s)
    return pl.pallas_call(
        functools.partial(_ffn_kernel, layer=layer),
        grid=(b, s // tm),
        in_specs=[
            pl.BlockSpec((1, tm, d), lambda bi, i: (bi, i, 0)),
            pl.BlockSpec((1, 1, mods.shape[-1]), lambda bi, i: (mod_row(bi), 0, 0)),
            _const_spec((1, d)),
            _HBM,
            _HBM,
        ],
        out_specs=pl.BlockSpec((1, tm, d), lambda bi, i: (bi, i, 0)),
        out_shape=jax.ShapeDtypeStruct((b, s, d), F32),
        scratch_shapes=[*_stage_scratch(*w1.shape[1:]), *_stage_scratch(*w2.shape[1:])],
        compiler_params=_params(("arbitrary", "arbitrary")),
        name="ffn",
    )(x, mods, g, w1, w2)


def _sgmlp_kernel(x_ref, mod_ref, g_ref, win_hbm, lng_ref, lnb_ref, sgw_ref, sgb_ref, wout_hbm,
                  o_ref, a_ref, win_ref, win_stage, win_sem, wout_ref, wout_stage, wout_sem):
    @pl.when(_first_step())
    def _():
        _stage_weight(win_hbm, 0, win_ref, win_stage, win_sem)
        _stage_weight(wout_hbm, 0, wout_ref, wout_stage, wout_sem)

    d = x_ref.shape[-1]
    rows = a_ref.shape[1]
    width = win_ref.shape[1] // 2
    mod = mod_ref[0]
    gain = g_ref[...] * (1.0 + mod[:, d:2 * d])
    shift = mod[:, 0:d]
    gate = mod[:, 2 * d:3 * d]

    def project(j):
        x = x_ref[0, j * rows:(j + 1) * rows, :]
        ms = jnp.mean(x * x, axis=-1, keepdims=True)
        h = (x * lax.rsqrt(ms + EPS) * gain + shift).astype(BF16)
        a_ref[j % 2] = jnp.dot(h, win_ref[...], preferred_element_type=F32)

    def gelu(t):
        return 0.5 * t * (1.0 + lax.erf(t * (1.0 / math.sqrt(2.0))))

    def epilogue(j):
        blk = slice(j * rows, (j + 1) * rows)
        a = a_ref.at[j % 2]
        v = gelu(a[:, width:])
        mu = jnp.mean(v, axis=-1, keepdims=True)
        vc = v - mu
        var = jnp.mean(vc * vc, axis=-1, keepdims=True)
        vn = (vc * lax.rsqrt(var + EPS) * lng_ref[...] + lnb_ref[...]).astype(BF16)
        mixed = []
        for r in range(rows // CHUNK):
            cols = [jnp.dot(sgw_ref[gi],
                            vn[r * CHUNK:(r + 1) * CHUNK, gi * SG_GROUP_DIM:(gi + 1) * SG_GROUP_DIM],
                            preferred_element_type=F32) for gi in range(SG_GROUPS)]
            mixed.append(jnp.concatenate(cols, axis=1) + sgb_ref[...])
        z = (gelu(a[:, :width]) * jnp.concatenate(mixed, axis=0)).astype(BF16)
        y = jnp.dot(z, wout_ref[...], preferred_element_type=F32)
        o_ref[0, blk, :] = x_ref[0, blk, :] + gate * y

    _software_pipeline(x_ref.shape[1] // rows, project, epilogue)


def _sgmlp_call(x, mods, mod_row, g, w_in, ln_g, ln_b, sg_w, sg_b_full, w_out):
    b, s, d = x.shape
    tm = min(TM_SG, s)
    return pl.pallas_call(
        _sgmlp_kernel,
        grid=(b, s // tm),
        in_specs=[
            pl.BlockSpec((1, tm, d), lambda bi, i: (bi, i, 0)),
            pl.BlockSpec((1, 1, mods.shape[-1]), lambda bi, i: (mod_row(bi), 0, 0)),
            _const_spec((1, d)),
            _HBM,
            _const_spec(ln_g.shape),
            _const_spec(ln_b.shape),
            _const_spec(sg_w.shape),
            _const_spec(sg_b_full.shape),
            _HBM,
        ],
        out_specs=pl.BlockSpec((1, tm, d), lambda bi, i: (bi, i, 0)),
        out_shape=jax.ShapeDtypeStruct((b, s, d), F32),
        scratch_shapes=[pltpu.VMEM((2, min(ROWS_SG, tm), w_in.shape[2]), F32),
                        *_stage_scratch(*w_in.shape[1:]), *_stage_scratch(*w_out.shape[1:])],
        compiler_params=_params(("arbitrary", "arbitrary")),
        name="sg_mlp",
    )(x, mods, g, w_in, ln_g, ln_b, sg_w, sg_b_full, w_out)


def _rope_tables(n_tok):
    t = jnp.arange(n_tok, dtype=jnp.int32)
    rows = (t // GRID_W).astype(F32)
    cols = (t % GRID_W).astype(F32)
    pairs = HEAD_DIM // 4
    inv = ROPE_THETA ** (-jnp.arange(pairs, dtype=F32) / pairs)
    ang = jnp.concatenate([rows[:, None] * inv, cols[:, None] * inv], axis=-1)
    cos, sin = jnp.cos(ang), jnp.sin(ang)
    cos2 = jnp.repeat(cos, 2, axis=-1)
    sin2 = jnp.stack([-sin, sin], axis=-1).reshape(n_tok, HEAD_DIM)
    return jnp.tile(cos2, (1, LANES // HEAD_DIM)), jnp.tile(sin2, (1, LANES // HEAD_DIM))


def _block_diag(blocks):
    n = len(blocks)
    rows = []
    for i, blk in enumerate(blocks):
        z = jnp.zeros_like(blk)
        rows.append(jnp.concatenate([blk if j == i else z for j in range(n)], axis=1))
    return jnp.concatenate(rows, axis=0)


def kernel(x, c, ctx, c_ctx, ada_w, ada_b, norm_mix_g, norm_ffn_g, ffn_w1, ffn_w2, ev_w_in, ev_w_out,
           pool_w, pool_scale, q_norm_g, k_norm_g, lam_q1, lam_k1, lam_q2, lam_k2, sub_norm_g,
           od_w_in, sg_ln_g, sg_ln_b, sg_w, sg_b, od_w_out):
    b, s, d = x.shape
    depth = ada_w.shape[0]
    assert depth == 2 and b <= SUBLANES - 1

    cin = jnp.zeros((SUBLANES, d), F32).at[:b].set(c).at[b].set(c_ctx)
    mods = _ada_call(cin, ada_w, ada_b).reshape(depth * SUBLANES, 1, 6 * d)

    def lat_row(layer):
        return lambda bi: layer * SUBLANES + bi

    def ctx_row(layer):
        return lambda bi: layer * SUBLANES + b

    n_groups = QK_WIDTH // HEAD_DIM
    qg = jnp.tile(q_norm_g[0] * (HEAD_DIM ** -0.5 * math.log2(math.e)), n_groups)[None, :]
    kg = jnp.tile(k_norm_g[0], n_groups)[None, :]
    ones = jnp.ones((HEAD_DIM, HEAD_DIM), BF16)
    bd = _block_diag([ones] * (256 // HEAD_DIM))
    cos, sin_signed = _rope_tables(s)
    ctx_len = ctx.shape[1]
    cos_ctx = jnp.ones((ctx_len, LANES), F32)
    sin_ctx = jnp.zeros((ctx_len, LANES), F32)
    g_mix0 = norm_mix_g[0][None, :]

    pool_in, q, k, vt = _inproj_call(x, mods, lat_row(0), g_mix0, ev_w_in, bd, qg, kg, cos, sin_signed, True)
    kc, vct = _inproj_call(ctx, mods, ctx_row(0), g_mix0, ev_w_in, bd, qg, kg, cos_ctx, sin_ctx, False)

    lam_init = 0.8 - 0.6 * math.exp(-0.3 * 0)
    lam_rows = jnp.zeros((SUBLANES, LANES), F32)
    for r, vec in enumerate((lam_q1[0], lam_k1[0], lam_q2[0], lam_k2[0])):
        lam_rows = lam_rows.at[r, :HEAD_DIM].set(vec)
    attn = _attn_call(_score_bound(qg, kg), lam_rows, sub_norm_g[0][None, :], q, k, vt, kc, vct, lam_init)

    pool_w_bd = _block_diag([pool_w[0, gi] for gi in range(len(POOL_WINDOWS))]).astype(BF16)
    x = _merge_ffn_call(x, mods, lat_row(0), pool_in, attn, pool_w_bd, pool_scale[0][None, :],
                        ev_w_out, norm_ffn_g[0][None, :], ffn_w1, ffn_w2, 0)

    sg_b_full = jnp.repeat(sg_b[0].T, SG_GROUP_DIM, axis=1)
    x = _sgmlp_call(x, mods, lat_row(1), norm_mix_g[1][None, :], od_w_in,
                    sg_ln_g[0][None, :], sg_ln_b[0][None, :], sg_w[0].astype(BF16), sg_b_full, od_w_out)
    x = _ffn_call(x, mods, lat_row(1), norm_ffn_g[1][None, :], ffn_w1, ffn_w2, 1)
    return x
```

```python
import functools
import math

import jax
import jax.numpy as jnp
from jax import lax
from jax.experimental import pallas as pl
from jax.experimental.pallas import tpu as pltpu

F32 = jnp.float32
BF16 = jnp.bfloat16

EPS = 1e-6
GRID_W = 64
ROPE_THETA = 10000.0
POOL_WINDOWS = (2, 4, 8, 16)
POOL_GROUP = 64
POOL_WIDTH = 256
HEAD_DIM = 64
V_DIM = 128
N_HEADS = 6
QK_WIDTH = N_HEADS * 2 * HEAD_DIM
CHUNK = 128
SG_GROUPS = 8
SG_GROUP_DIM = 256

LANES = 128
SUBLANES = 8
POOL_HALO = 8
VMEM_LIMIT = 56 * 1024 * 1024
STAGE_BYTES = 2 * 1024 * 1024

MAX_UNSHIFTED_SCORE = 60.0

TM_PROJ = 1024
ROWS_PROJ = 256
TQ = 512
ATTN_KEYS = 512
TM_FFN = 512
ROWS_FFN = 256
TM_SG = 1024
ROWS_SG = 256


def _params(sem):
    return pltpu.CompilerParams(dimension_semantics=sem, vmem_limit_bytes=VMEM_LIMIT)


def _const_spec(shape):
    zeros = (0,) * len(shape)
    return pl.BlockSpec(shape, lambda *_: zeros, pipeline_mode=pl.Buffered(1))


def _norm_mod(x, g, shift, scale):
    ms = jnp.mean(x * x, axis=-1, keepdims=True)
    return x * lax.rsqrt(ms + EPS) * (g * (1.0 + scale)) + shift


def _software_pipeline(n, produce, consume):
    produce(0)
    for j in range(n):
        if j + 1 < n:
            produce(j + 1)
        consume(j)


def _first_step():
    return jnp.logical_and(pl.program_id(0) == 0, pl.program_id(1) == 0)


def _stage_weight(w_hbm, layer, dst_ref, stage_ref, sem, col0=0):
    k, n = dst_ref.shape
    slab = stage_ref.shape[1]
    n_slabs = k // slab

    def copy(i):
        return pltpu.make_async_copy(w_hbm.at[layer, pl.ds(i * slab, slab), pl.ds(col0, n)],
                                     stage_ref.at[i % 2], sem.at[i % 2])

    copy(0).start()
    for i in range(n_slabs):
        if i + 1 < n_slabs:
            copy(i + 1).start()
        copy(i).wait()
        dst_ref[i * slab:(i + 1) * slab, :] = stage_ref[i % 2].astype(BF16)


def _stage_scratch(k, n):
    slab = max(r for r in range(SUBLANES, k + 1, SUBLANES) if k % r == 0 and (r == SUBLANES or 4 * r * n <= STAGE_BYTES))
    return [pltpu.VMEM((k, n), BF16), pltpu.VMEM((2, slab, n), F32), pltpu.SemaphoreType.DMA((2,))]


_HBM = pl.BlockSpec(memory_space=pl.ANY)


def _ada_kernel(c_ref, w_ref, b_ref, o_ref):
    c = c_ref[...]
    sc = (c * jax.nn.sigmoid(c)).astype(BF16)
    o_ref[0] = jnp.dot(sc, w_ref[0].astype(BF16), preferred_element_type=F32) + b_ref[0]


def _ada_call(cin, ada_w, ada_b):
    depth, d, n = ada_w.shape
    tn = 1536
    return pl.pallas_call(
        _ada_kernel,
        grid=(depth, n // tn),
        in_specs=[
            pl.BlockSpec((SUBLANES, d), lambda l, j: (0, 0)),
            pl.BlockSpec((1, d, tn), lambda l, j: (l, 0, j)),
            pl.BlockSpec((1, 1, tn), lambda l, j: (l, 0, j)),
        ],
        out_specs=pl.BlockSpec((1, SUBLANES, tn), lambda l, j: (l, 0, j)),
        out_shape=jax.ShapeDtypeStruct((depth, SUBLANES, n), F32),
        compiler_params=_params(("parallel", "parallel")),
        name="ada_ln",
    )(cin, ada_w, ada_b.reshape(depth, 1, n))


def _qk_norm_rope(y_ref, ss_ref, off, ss_off, g, cos, sin_signed, even_lane):
    outs = []
    for j in range(QK_WIDTH // LANES):
        cols = slice(LANES * j, LANES * (j + 1))
        ss = ss_ref[:, ss_off + LANES * j:ss_off + LANES * (j + 1)]
        xc = (y_ref[:, off + LANES * j:off + LANES * (j + 1)]
              * lax.rsqrt(ss * (1.0 / HEAD_DIM) + EPS) * g[:, cols])
        partner = jnp.where(even_lane, pltpu.roll(xc, LANES - 1, 1), pltpu.roll(xc, 1, 1))
        outs.append(xc * cos + partner * sin_signed)
    return jnp.concatenate(outs, axis=1)


def _inproj_kernel(x_ref, mod_ref, g_ref, w_hbm, bd_ref, qg_ref, kg_ref, cos_ref, sin_ref,
                   *refs, with_q):
    *out_refs, y_ref, ss_ref, w_ref, w_stage, w_sem = refs

    @pl.when(_first_step())
    def _():
        _stage_weight(w_hbm, 0, w_ref, w_stage, w_sem, col0=0 if with_q else POOL_WIDTH + QK_WIDTH)

    d = x_ref.shape[-1]
    rows = y_ref.shape[1]
    mod = mod_ref[0]
    gain = g_ref[...] * (1.0 + mod[:, d:2 * d])
    shift = mod[:, 0:d]
    lane = lax.broadcasted_iota(jnp.int32, (rows, LANES), 1)
    even_lane = (lane % 2) == 0
    qk_off = POOL_WIDTH if with_q else 0
    qk_cols = ss_ref.shape[2]

    def project(j):
        x = x_ref[0, j * rows:(j + 1) * rows, :]
        ms = jnp.mean(x * x, axis=-1, keepdims=True)
        h = (x * lax.rsqrt(ms + EPS) * gain + shift).astype(BF16)
        y_ref[j % 2] = jnp.dot(h, w_ref[...], preferred_element_type=F32)
        for p in range(0, qk_cols, 256):
            t = y_ref[j % 2, :, qk_off + p:qk_off + p + 256]
            ss_ref[j % 2, :, p:p + 256] = jnp.dot((t * t).astype(BF16), bd_ref[...],
                                                  preferred_element_type=F32)

    def epilogue(j):
        blk = slice(j * rows, (j + 1) * rows)
        y = y_ref.at[j % 2]
        ss = ss_ref.at[j % 2]
        cos = cos_ref[blk, :]
        sin_signed = sin_ref[blk, :]
        if with_q:
            pool_ref, q_ref, k_ref, vt_ref = out_refs
            pool_ref[0, blk, :] = y[:, :POOL_WIDTH]
            q_ref[0, blk, :] = _qk_norm_rope(y, ss, qk_off, 0, qg_ref[...], cos, sin_signed,
                                             even_lane).astype(BF16)
            k_at = QK_WIDTH
        else:
            k_ref, vt_ref = out_refs
            k_at = 0
        k_ref[0, blk, :] = _qk_norm_rope(y, ss, qk_off + k_at, k_at, kg_ref[...], cos, sin_signed,
                                         even_lane).astype(BF16)
        vt_ref[0, 0, :, blk] = y[:, qk_off + k_at + QK_WIDTH:].T.astype(BF16)

    _software_pipeline(x_ref.shape[1] // rows, project, epilogue)


def _inproj_call(x, mods, mod_row, g, w, bd, qg, kg, cos, sin_signed, with_q):
    b, s, d = x.shape
    tm = min(TM_PROJ, s)
    nt = s // tm
    dv = N_HEADS * V_DIM
    n = w.shape[2] if with_q else QK_WIDTH + dv
    out_shape = [jax.ShapeDtypeStruct((b, s, QK_WIDTH), BF16),
                 jax.ShapeDtypeStruct((b, nt, dv, tm), BF16)]
    out_specs = [pl.BlockSpec((1, tm, QK_WIDTH), lambda bi, i: (bi, i, 0)),
                 pl.BlockSpec((1, 1, dv, tm), lambda bi, i: (bi, i, 0, 0))]
    if with_q:
        out_shape = [jax.ShapeDtypeStruct((b, s, POOL_WIDTH), F32),
                     jax.ShapeDtypeStruct((b, s, QK_WIDTH), BF16)] + out_shape
        out_specs = [pl.BlockSpec((1, tm, POOL_WIDTH), lambda bi, i: (bi, i, 0)),
                     pl.BlockSpec((1, tm, QK_WIDTH), lambda bi, i: (bi, i, 0))] + out_specs
    return pl.pallas_call(
        functools.partial(_inproj_kernel, with_q=with_q),
        grid=(b, nt),
        in_specs=[
            pl.BlockSpec((1, tm, d), lambda bi, i: (bi, i, 0)),
            pl.BlockSpec((1, 1, mods.shape[-1]), lambda bi, i: (mod_row(bi), 0, 0)),
            _const_spec((1, d)),
            _HBM,
            _const_spec(bd.shape),
            _const_spec(qg.shape),
            _const_spec(kg.shape),
            pl.BlockSpec((tm, LANES), lambda bi, i: (i, 0)),
            pl.BlockSpec((tm, LANES), lambda bi, i: (i, 0)),
        ],
        out_specs=out_specs,
        out_shape=out_shape,
        scratch_shapes=[pltpu.VMEM((2, min(ROWS_PROJ, tm), n), F32),
                        pltpu.VMEM((2, min(ROWS_PROJ, tm), n - dv - (POOL_WIDTH if with_q else 0)), F32),
                        *_stage_scratch(d, n)],
        compiler_params=_params(("arbitrary", "arbitrary")),
        name="inproj_q" if with_q else "inproj_ctx",
    )(x, mods, g, w, bd, qg, kg, cos, sin_signed)


def _attn_kernel(bound_ref, lam_ref, subg_ref, q_ref, k_ref, vt_ref, kc_ref, vct_ref, o_ref,
                 st_ref, l_ref, acc_ref, *, lam_init):
    tq = q_ref.shape[1]
    n_chunks = vt_ref.shape[1]
    tk = vt_ref.shape[-1]

    @pl.when(pl.program_id(0) == 0)
    def _():
        l_ref[...] = jnp.ones(l_ref.shape, F32)
        acc_ref[...] = jnp.zeros(acc_ref.shape, F32)

    lam_rows = lam_ref[...]
    a1 = jnp.sum(lam_rows[0:1] * lam_rows[1:2], axis=1, keepdims=True)
    a2 = jnp.sum(lam_rows[2:3] * lam_rows[3:4], axis=1, keepdims=True)
    lam = jnp.exp(a1) - jnp.exp(a2) + lam_init

    q = q_ref[0]
    lane = lax.broadcasted_iota(jnp.int32, q.shape, 1)
    zero = jnp.zeros_like(q)
    qq = jnp.concatenate([jnp.where(lane < HEAD_DIM, q, zero),
                          jnp.where(lane >= HEAD_DIM, q, zero)], axis=0)
    bounded = bound_ref[0] <= MAX_UNSHIFTED_SCORE

    def chunks():
        for c in range(n_chunks):
            for j in range(0, tk, ATTN_KEYS):
                yield k_ref[0, c * tk + j:c * tk + j + ATTN_KEYS, :], vt_ref[0, c, :, j:j + ATTN_KEYS]
        for j in range(0, kc_ref.shape[1], ATTN_KEYS):
            yield kc_ref[0, j:j + ATTN_KEYS, :], vct_ref[0, 0, :, j:j + ATTN_KEYS]

    def scores(kb):
        return lax.dot_general(kb, qq, (((1,), (1,)), ((), ())), preferred_element_type=F32)

    def write_previous_tile():
        l = l_ref[...]
        acc = acc_ref[...]
        ot = acc[:, :tq] / l[:, :tq] - lam * (acc[:, tq:] / l[:, tq:])
        o = ot.T
        ms = jnp.mean(o * o, axis=-1, keepdims=True)
        o_ref[0] = (o * lax.rsqrt(ms + EPS) * (subg_ref[...] * (1.0 - lam_init))).astype(BF16)

    @pl.when(bounded)
    def _():
        l = jnp.zeros((1, 2 * tq), F32)
        acc = jnp.zeros((V_DIM, 2 * tq), F32)
        work = list(chunks())
        st_ref[0, :work[0][0].shape[0], :] = scores(work[0][0])
        write_previous_tile()
        for c, (kb, vtb) in enumerate(work):
            if c + 1 < len(work):
                nxt = work[c + 1][0]
                st_ref[(c + 1) % 2, :nxt.shape[0], :] = scores(nxt)
            p = jnp.exp2(st_ref[c % 2, :kb.shape[0], :])
            l = l + jnp.sum(p, axis=0, keepdims=True)
            acc = acc + jnp.dot(vtb, p.astype(BF16), preferred_element_type=F32)
        l_ref[...] = l
        acc_ref[...] = acc

    @pl.when(jnp.logical_not(bounded))
    def _():
        m = jnp.full((1, 2 * tq), -jnp.inf, F32)
        l = jnp.zeros((1, 2 * tq), F32)
        acc = jnp.zeros((V_DIM, 2 * tq), F32)
        write_previous_tile()
        for kb, vtb in chunks():
            st = scores(kb)
            m_new = jnp.maximum(m, jnp.max(st, axis=0, keepdims=True))
            alpha = jnp.exp2(m - m_new)
            p = jnp.exp2(st - m_new)
            l = alpha * l + jnp.sum(p, axis=0, keepdims=True)
            acc = alpha * acc + jnp.dot(vtb, p.astype(BF16), preferred_element_type=F32)
            m = m_new
        l_ref[...] = l
        acc_ref[...] = acc


def _score_bound(q_gain, k_gain):
    slack = 1.0 + 2.0 ** -5
    return (HEAD_DIM * slack * jnp.max(jnp.abs(q_gain)) * jnp.max(jnp.abs(k_gain))).reshape(1)


def _attn_call(score_bound, lam_rows, sub_g, q, k, vt, kc, vct, lam_init):
    b, s, _ = q.shape
    tq = min(TQ, s)
    n_chunks, tk = vt.shape[1], vt.shape[3]
    ctx_len = kc.shape[1]
    nq = s // tq
    n_tiles = b * N_HEADS * nq

    def tile(t):
        t = jnp.minimum(t, n_tiles - 1)
        return t // (N_HEADS * nq), (t // nq) % N_HEADS, t % nq

    def q_map(t):
        bi, h, i = tile(t)
        return bi, i, h

    def out_map(t):
        bi, h, i = tile(jnp.maximum(t - 1, 0))
        return bi, i, h

    def kv_map(t):
        bi, h, _ = tile(t)
        return bi, 0, h

    def vt_map(t):
        bi, h, _ = tile(t)
        return bi, 0, h, 0

    return pl.pallas_call(
        functools.partial(_attn_kernel, lam_init=lam_init),
        grid=(n_tiles + 1,),
        in_specs=[
            pl.BlockSpec(memory_space=pltpu.SMEM),
            _const_spec(lam_rows.shape),
            _const_spec(sub_g.shape),
            pl.BlockSpec((1, tq, V_DIM), q_map),
            pl.BlockSpec((1, s, V_DIM), kv_map),
            pl.BlockSpec((1, n_chunks, V_DIM, tk), vt_map),
            pl.BlockSpec((1, ctx_len, V_DIM), kv_map),
            pl.BlockSpec((1, 1, V_DIM, ctx_len), vt_map),
        ],
        out_specs=pl.BlockSpec((1, tq, V_DIM), out_map),
        out_shape=jax.ShapeDtypeStruct((b, s, N_HEADS * V_DIM), BF16),
        scratch_shapes=[pltpu.VMEM((2, ATTN_KEYS, 2 * tq), F32), pltpu.VMEM((1, 2 * tq), F32),
                        pltpu.VMEM((V_DIM, 2 * tq), F32)],
        compiler_params=_params(("arbitrary",)),
        name="diff_attn",
    )(score_bound, lam_rows, sub_g, q, k, vt, kc, vct)


def _pool_merge_rows(x_ref, mod, pm_ref, pp_ref, pn_ref, attn_ref, pw_ref, ps_ref, wo_ref, ext_ref, seq):
    d = x_ref.shape[-1]
    tm = x_ref.shape[1]
    i = pl.program_id(1)
    last = pl.num_programs(1) - 1
    ext_ref[0:POOL_HALO, :] = jnp.where(i > 0, pp_ref[0], 0.0)
    ext_ref[POOL_HALO:POOL_HALO + tm, :] = pm_ref[0]
    ext_ref[POOL_HALO + tm:2 * POOL_HALO + tm, :] = jnp.where(i < last, pn_ref[0], 0.0)

    def window_sum(lo, hi, half):
        acc = None
        for off in range(lo, hi + 1):
            piece = ext_ref[pl.ds(POOL_HALO + off, tm), LANES * half:LANES * (half + 1)]
            acc = piece if acc is None else acc + piece
        return acc

    t = i * tm + lax.broadcasted_iota(jnp.int32, (tm, LANES), 0)
    lane = lax.broadcasted_iota(jnp.int32, (tm, LANES), 1)
    low_group = lane < POOL_GROUP
    pooled = []
    for half in range(2):
        w_small, w_big = POOL_WINDOWS[2 * half], POOL_WINDOWS[2 * half + 1]
        s_small = window_sum(-(w_small // 2), w_small // 2 - 1, half)
        s_big = (s_small + window_sum(-(w_big // 2), -(w_small // 2) - 1, half)
                 + window_sum(w_small // 2, w_big // 2 - 1, half))
        reach = jnp.where(low_group, w_small // 2, w_big // 2)
        cnt = jnp.minimum(t + reach, seq) - jnp.maximum(t - reach, 0)
        mean = jnp.where(low_group, s_small, s_big) / cnt.astype(F32)
        pooled.append(mean - pm_ref[0, :, LANES * half:LANES * (half + 1)])
    pooled = jnp.concatenate(pooled, axis=1).astype(BF16)
    pool_y = (jnp.dot(pooled, pw_ref[...], preferred_element_type=F32) * ps_ref[...]).astype(BF16)
    y = (jnp.dot(pool_y, wo_ref[0:POOL_WIDTH, :], preferred_element_type=F32)
         + jnp.dot(attn_ref[0], wo_ref[POOL_WIDTH:, :], preferred_element_type=F32))
    return x_ref[0] + mod[:, 2 * d:3 * d] * y


def _ffn_rows(x, mod, g, w1_ref, w2_ref):
    d = x.shape[-1]
    h = _norm_mod(x, g, mod[:, 3 * d:4 * d], mod[:, 4 * d:5 * d]).astype(BF16)
    a = jnp.maximum(jnp.dot(h, w1_ref[...], preferred_element_type=F32), 0.0)
    a = (a * a).astype(BF16)
    y = jnp.dot(a, w2_ref[...], preferred_element_type=F32)
    return x + mod[:, 5 * d:6 * d] * y


def _ffn_blocks(x_block, o_ref, mod, g, w1_ref, w2_ref):
    tm = o_ref.shape[1]
    rows = min(ROWS_FFN, tm)
    for r in range(tm // rows):
        blk = slice(r * rows, (r + 1) * rows)
        o_ref[0, blk, :] = _ffn_rows(x_block(blk), mod, g, w1_ref, w2_ref)


def _ffn_kernel(x_ref, mod_ref, g_ref, w1_hbm, w2_hbm, o_ref,
                w1_ref, w1_stage, w1_sem, w2_ref, w2_stage, w2_sem, *, layer):
    @pl.when(_first_step())
    def _():
        _stage_weight(w1_hbm, layer, w1_ref, w1_stage, w1_sem)
        _stage_weight(w2_hbm, layer, w2_ref, w2_stage, w2_sem)

    _ffn_blocks(lambda blk: x_ref[0, blk, :], o_ref, mod_ref[0], g_ref[...], w1_ref, w2_ref)


def _merge_ffn_kernel(x_ref, mod_ref, pm_ref, pp_ref, pn_ref, attn_ref, pw_ref, ps_ref, wo_hbm,
                      g_ref, w1_hbm, w2_hbm, o_ref, ext_ref,
                      wo_ref, wo_stage, wo_sem, w1_ref, w1_stage, w1_sem, w2_ref, w2_stage, w2_sem, *,
                      seq, layer):
    @pl.when(_first_step())
    def _():
        _stage_weight(wo_hbm, layer // 2, wo_ref, wo_stage, wo_sem)
        _stage_weight(w1_hbm, layer, w1_ref, w1_stage, w1_sem)
        _stage_weight(w2_hbm, layer, w2_ref, w2_stage, w2_sem)

    mod = mod_ref[0]
    x1 = _pool_merge_rows(x_ref, mod, pm_ref, pp_ref, pn_ref, attn_ref, pw_ref, ps_ref, wo_ref, ext_ref, seq)
    _ffn_blocks(lambda blk: x1[blk, :], o_ref, mod, g_ref[...], w1_ref, w2_ref)


def _merge_ffn_call(x, mods, mod_row, pool_in, attn, pool_w_bd, pool_scale, w_out, g, w1, w2, layer):
    b, s, d = x.shape
    tm = min(TM_FFN, s)
    hb = tm // POOL_HALO
    n_halo = s // POOL_HALO
    return pl.pallas_call(
        functools.partial(_merge_ffn_kernel, seq=s, layer=layer),
        grid=(b, s // tm),
        in_specs=[
            pl.BlockSpec((1, tm, d), lambda bi, i: (bi, i, 0)),
            pl.BlockSpec((1, 1, mods.shape[-1]), lambda bi, i: (mod_row(bi), 0, 0)),
            pl.BlockSpec((1, tm, POOL_WIDTH), lambda bi, i: (bi, i, 0)),
            pl.BlockSpec((1, POOL_HALO, POOL_WIDTH),
                         lambda bi, i: (bi, jnp.maximum(i * hb - 1, 0), 0)),
            pl.BlockSpec((1, POOL_HALO, POOL_WIDTH),
                         lambda bi, i: (bi, jnp.minimum((i + 1) * hb, n_halo - 1), 0)),
            pl.BlockSpec((1, tm, attn.shape[-1]), lambda bi, i: (bi, i, 0)),
            _const_spec(pool_w_bd.shape),
            _const_spec(pool_scale.shape),
            _HBM,
            _const_spec((1, d)),
            _HBM,
            _HBM,
        ],
        out_specs=pl.BlockSpec((1, tm, d), lambda bi, i: (bi, i, 0)),
        out_shape=jax.ShapeDtypeStruct((b, s, d), F32),
        scratch_shapes=[pltpu.VMEM((tm + 2 * POOL_HALO, POOL_WIDTH), F32), *_stage_scratch(*w_out.shape[1:]),
                        *_stage_scratch(*w1.shape[1:]), *_stage_scratch(*w2.shape[1:])],
        compiler_params=_params(("arbitrary", "arbitrary")),
        name="pool_merge_ffn",
    )(x, mods, pool_in, pool_in, pool_in, attn, pool_w_bd, pool_scale, w_out, g, w1, w2)


def _ffn_call(x, mods, mod_row, g, w1, w2, layer):
    b, s, d = x.shape
    tm = min(TM_FFN, s)
    return pl.pallas_call(
        functools.partial(_ffn_kernel, layer=layer),
        grid=(b, s // tm),
        in_specs=[
            pl.BlockSpec((1, tm, d), lambda bi, i: (bi, i, 0)),
            pl.BlockSpec((1, 1, mods.shape[-1]), lambda bi, i: (mod_row(bi), 0, 0)),
            _const_spec((1, d)),
            _HBM,
            _HBM,
        ],
        out_specs=pl.BlockSpec((1, tm, d), lambda bi, i: (bi, i, 0)),
        out_shape=jax.ShapeDtypeStruct((b, s, d), F32),
        scratch_shapes=[*_stage_scratch(*w1.shape[1:]), *_stage_scratch(*w2.shape[1:])],
        compiler_params=_params(("arbitrary", "arbitrary")),
        name="ffn",
    )(x, mods, g, w1, w2)


def _sgmlp_kernel(x_ref, mod_ref, g_ref, win_hbm, lng_ref, lnb_ref, sgw_ref, sgb_ref, wout_hbm,
                  o_ref, a_ref, win_ref, win_stage, win_sem, wout_ref, wout_stage, wout_sem):
    @pl.when(_first_step())
    def _():
        _stage_weight(win_hbm, 0, win_ref, win_stage, win_sem)
        _stage_weight(wout_hbm, 0, wout_ref, wout_stage, wout_sem)

    d = x_ref.shape[-1]
    rows = a_ref.shape[1]
    width = win_ref.shape[1] // 2
    mod = mod_ref[0]
    gain = g_ref[...] * (1.0 + mod[:, d:2 * d])
    shift = mod[:, 0:d]
    gate = mod[:, 2 * d:3 * d]

    def project(j):
        x = x_ref[0, j * rows:(j + 1) * rows, :]
        ms = jnp.mean(x * x, axis=-1, keepdims=True)
        h = (x * lax.rsqrt(ms + EPS) * gain + shift).astype(BF16)
        a_ref[j % 2] = jnp.dot(h, win_ref[...], preferred_element_type=F32)

    def gelu(t):
        return 0.5 * t * (1.0 + lax.erf(t * (1.0 / math.sqrt(2.0))))

    def epilogue(j):
        blk = slice(j * rows, (j + 1) * rows)
        a = a_ref.at[j % 2]
        v = gelu(a[:, width:])
        mu = jnp.mean(v, axis=-1, keepdims=True)
        vc = v - mu
        var = jnp.mean(vc * vc, axis=-1, keepdims=True)
        vn = (vc * lax.rsqrt(var + EPS) * lng_ref[...] + lnb_ref[...]).astype(BF16)
        mixed = []
        for r in range(rows // CHUNK):
            cols = [jnp.dot(sgw_ref[gi],
                            vn[r * CHUNK:(r + 1) * CHUNK, gi * SG_GROUP_DIM:(gi + 1) * SG_GROUP_DIM],
                            preferred_element_type=F32) for gi in range(SG_GROUPS)]
            mixed.append(jnp.concatenate(cols, axis=1) + sgb_ref[...])
        z = (gelu(a[:, :width]) * jnp.concatenate(mixed, axis=0)).astype(BF16)
        y = jnp.dot(z, wout_ref[...], preferred_element_type=F32)
        o_ref[0, blk, :] = x_ref[0, blk, :] + gate * y

    _software_pipeline(x_ref.shape[1] // rows, project, epilogue)


def _sgmlp_call(x, mods, mod_row, g, w_in, ln_g, ln_b, sg_w, sg_b_full, w_out):
    b, s, d = x.shape
    tm = min(TM_SG, s)
    return pl.pallas_call(
        _sgmlp_kernel,
        grid=(b, s // tm),
        in_specs=[
            pl.BlockSpec((1, tm, d), lambda bi, i: (bi, i, 0)),
            pl.BlockSpec((1, 1, mods.shape[-1]), lambda bi, i: (mod_row(bi), 0, 0)),
            _const_spec((1, d)),
            _HBM,
            _const_spec(ln_g.shape),
            _const_spec(ln_b.shape),
            _const_spec(sg_w.shape),
            _const_spec(sg_b_full.shape),
            _HBM,
        ],
        out_specs=pl.BlockSpec((1, tm, d), lambda bi, i: (bi, i, 0)),
        out_shape=jax.ShapeDtypeStruct((b, s, d), F32),
        scratch_shapes=[pltpu.VMEM((2, min(ROWS_SG, tm), w_in.shape[2]), F32),
                        *_stage_scratch(*w_in.shape[1:]), *_stage_scratch(*w_out.shape[1:])],
        compiler_params=_params(("arbitrary", "arbitrary")),
        name="sg_mlp",
    )(x, mods, g, w_in, ln_g, ln_b, sg_w, sg_b_full, w_out)


def _rope_tables(n_tok):
    t = jnp.arange(n_tok, dtype=jnp.int32)
    rows = (t // GRID_W).astype(F32)
    cols = (t % GRID_W).astype(F32)
    pairs = HEAD_DIM // 4
    inv = ROPE_THETA ** (-jnp.arange(pairs, dtype=F32) / pairs)
    ang = jnp.concatenate([rows[:, None] * inv, cols[:, None] * inv], axis=-1)
    cos, sin = jnp.cos(ang), jnp.sin(ang)
    cos2 = jnp.repeat(cos, 2, axis=-1)
    sin2 = jnp.stack([-sin, sin], axis=-1).reshape(n_tok, HEAD_DIM)
    return jnp.tile(cos2, (1, LANES // HEAD_DIM)), jnp.tile(sin2, (1, LANES // HEAD_DIM))


def _block_diag(blocks):
    n = len(blocks)
    rows = []
    for i, blk in enumerate(blocks):
        z = jnp.zeros_like(blk)
        rows.append(jnp.concatenate([blk if j == i else z for j in range(n)], axis=1))
    return jnp.concatenate(rows, axis=0)


def kernel(x, c, ctx, c_ctx, ada_w, ada_b, norm_mix_g, norm_ffn_g, ffn_w1, ffn_w2, ev_w_in, ev_w_out,
           pool_w, pool_scale, q_norm_g, k_norm_g, lam_q1, lam_k1, lam_q2, lam_k2, sub_norm_g,
           od_w_in, sg_ln_g, sg_ln_b, sg_w, sg_b, od_w_out):
    b, s, d = x.shape
    depth = ada_w.shape[0]
    assert depth == 2 and b <= SUBLANES - 1

    cin = jnp.zeros((SUBLANES, d), F32).at[:b].set(c).at[b].set(c_ctx)
    mods = _ada_call(cin, ada_w, ada_b).reshape(depth * SUBLANES, 1, 6 * d)

    def lat_row(layer):
        return lambda bi: layer * SUBLANES + bi

    def ctx_row(layer):
        return lambda bi: layer * SUBLANES + b

    n_groups = QK_WIDTH // HEAD_DIM
    qg = jnp.tile(q_norm_g[0] * (HEAD_DIM ** -0.5 * math.log2(math.e)), n_groups)[None, :]
    kg = jnp.tile(k_norm_g[0], n_groups)[None, :]
    ones = jnp.ones((HEAD_DIM, HEAD_DIM), BF16)
    bd = _block_diag([ones] * (256 // HEAD_DIM))
    cos, sin_signed = _rope_tables(s)
    ctx_len = ctx.shape[1]
    cos_ctx = jnp.ones((ctx_len, LANES), F32)
    sin_ctx = jnp.zeros((ctx_len, LANES), F32)
    g_mix0 = norm_mix_g[0][None, :]

    pool_in, q, k, vt = _inproj_call(x, mods, lat_row(0), g_mix0, ev_w_in, bd, qg, kg, cos, sin_signed, True)
    kc, vct = _inproj_call(ctx, mods, ctx_row(0), g_mix0, ev_w_in, bd, qg, kg, cos_ctx, sin_ctx, False)

    lam_init = 0.8 - 0.6 * math.exp(-0.3 * 0)
    lam_rows = jnp.zeros((SUBLANES, LANES), F32)
    for r, vec in enumerate((lam_q1[0], lam_k1[0], lam_q2[0], lam_k2[0])):
        lam_rows = lam_rows.at[r, :HEAD_DIM].set(vec)
    attn = _attn_call(_score_bound(qg, kg), lam_rows, sub_norm_g[0][None, :], q, k, vt, kc, vct, lam_init)

    pool_w_bd = _block_diag([pool_w[0, gi] for gi in range(len(POOL_WINDOWS))]).astype(BF16)
    x = _merge_ffn_call(x, mods, lat_row(0), pool_in, attn, pool_w_bd, pool_scale[0][None, :],
                        ev_w_out, norm_ffn_g[0][None, :], ffn_w1, ffn_w2, 0)

    sg_b_full = jnp.repeat(sg_b[0].T, SG_GROUP_DIM, axis=1)
    x = _sgmlp_call(x, mods, lat_row(1), norm_mix_g[1][None, :], od_w_in,
                    sg_ln_g[0][None, :], sg_ln_b[0][None, :], sg_w[0].astype(BF16), sg_b_full, od_w_out)
    x = _ffn_call(x, mods, lat_row(1), norm_ffn_g[1][None, :], ffn_w1, ffn_w2, 1)
    return x
```

```python
import functools
import math

import jax
import jax.numpy as jnp
from jax import lax
from jax.experimental import pallas as pl
from jax.experimental.pallas import tpu as pltpu

F32 = jnp.float32
BF16 = jnp.bfloat16

EPS = 1e-6
GRID_W = 64
ROPE_THETA = 10000.0
POOL_WINDOWS = (2, 4, 8, 16)
POOL_GROUP = 64
POOL_WIDTH = 256
HEAD_DIM = 64
V_DIM = 128
N_HEADS = 6
QK_WIDTH = N_HEADS * 2 * HEAD_DIM
CHUNK = 128
SG_GROUPS = 8
SG_GROUP_DIM = 256

LANES = 128
SUBLANES = 8
POOL_HALO = 8
VMEM_LIMIT = 56 * 1024 * 1024
STAGE_BYTES = 2 * 1024 * 1024

MAX_UNSHIFTED_SCORE = 60.0

TM_PROJ = 1024
ROWS_PROJ = 256
TQ = 512
ATTN_KEYS = 512
TM_FFN = 512
ROWS_FFN = 256
TM_SG = 1024
ROWS_SG = 256


def _params(sem):
    return pltpu.CompilerParams(dimension_semantics=sem, vmem_limit_bytes=VMEM_LIMIT)


def _const_spec(shape):
    zeros = (0,) * len(shape)
    return pl.BlockSpec(shape, lambda *_: zeros, pipeline_mode=pl.Buffered(1))


def _norm_mod(x, g, shift, scale):
    ms = jnp.mean(x * x, axis=-1, keepdims=True)
    return x * lax.rsqrt(ms + EPS) * (g * (1.0 + scale)) + shift


def _software_pipeline(n, produce, consume):
    produce(0)
    for j in range(n):
        if j + 1 < n:
            produce(j + 1)
        consume(j)


def _first_step():
    return jnp.logical_and(pl.program_id(0) == 0, pl.program_id(1) == 0)


def _stage_weight(w_hbm, layer, dst_ref, stage_ref, sem, col0=0):
    k, n = dst_ref.shape
    slab = stage_ref.shape[1]
    n_slabs = k // slab

    def copy(i):
        return pltpu.make_async_copy(w_hbm.at[layer, pl.ds(i * slab, slab), pl.ds(col0, n)],
                                     stage_ref.at[i % 2], sem.at[i % 2])

    copy(0).start()
    for i in range(n_slabs):
        if i + 1 < n_slabs:
            copy(i + 1).start()
        copy(i).wait()
        dst_ref[i * slab:(i + 1) * slab, :] = stage_ref[i % 2].astype(BF16)


def _stage_scratch(k, n):
    slab = max(r for r in range(SUBLANES, k + 1, SUBLANES) if k % r == 0 and (r == SUBLANES or 4 * r * n <= STAGE_BYTES))
    return [pltpu.VMEM((k, n), BF16), pltpu.VMEM((2, slab, n), F32), pltpu.SemaphoreType.DMA((2,))]


_HBM = pl.BlockSpec(memory_space=pl.ANY)


def _ada_kernel(c_ref, w_ref, b_ref, o_ref):
    c = c_ref[...]
    sc = (c * jax.nn.sigmoid(c)).astype(BF16)
    o_ref[0] = jnp.dot(sc, w_ref[0].astype(BF16), preferred_element_type=F32) + b_ref[0]


def _ada_call(cin, ada_w, ada_b):
    depth, d, n = ada_w.shape
    tn = 1536
    return pl.pallas_call(
        _ada_kernel,
        grid=(depth, n // tn),
        in_specs=[
            pl.BlockSpec((SUBLANES, d), lambda l, j: (0, 0)),
            pl.BlockSpec((1, d, tn), lambda l, j: (l, 0, j)),
            pl.BlockSpec((1, 1, tn), lambda l, j: (l, 0, j)),
        ],
        out_specs=pl.BlockSpec((1, SUBLANES, tn), lambda l, j: (l, 0, j)),
        out_shape=jax.ShapeDtypeStruct((depth, SUBLANES, n), F32),
        compiler_params=_params(("parallel", "parallel")),
        name="ada_ln",
    )(cin, ada_w, ada_b.reshape(depth, 1, n))


def _qk_norm_rope(y_ref, ss_ref, off, ss_off, g, cos, sin_signed, even_lane):
    outs = []
    for j in range(QK_WIDTH // LANES):
        cols = slice(LANES * j, LANES * (j + 1))
        ss = ss_ref[:, ss_off + LANES * j:ss_off + LANES * (j + 1)]
        xc = (y_ref[:, off + LANES * j:off + LANES * (j + 1)]
              * lax.rsqrt(ss * (1.0 / HEAD_DIM) + EPS) * g[:, cols])
        partner = jnp.where(even_lane, pltpu.roll(xc, LANES - 1, 1), pltpu.roll(xc, 1, 1))
        outs.append(xc * cos + partner * sin_signed)
    return jnp.concatenate(outs, axis=1)


def _inproj_kernel(x_ref, mod_ref, g_ref, w_hbm, bd_ref, qg_ref, kg_ref, cos_ref, sin_ref,
                   *refs, with_q):
    *out_refs, y_ref, ss_ref, w_ref, w_stage, w_sem = refs

    @pl.when(_first_step())
    def _():
        _stage_weight(w_hbm, 0, w_ref, w_stage, w_sem, col0=0 if with_q else POOL_WIDTH + QK_WIDTH)

    d = x_ref.shape[-1]
    rows = y_ref.shape[1]
    mod = mod_ref[0]
    gain = g_ref[...] * (1.0 + mod[:, d:2 * d])
    shift = mod[:, 0:d]
    lane = lax.broadcasted_iota(jnp.int32, (rows, LANES), 1)
    even_lane = (lane % 2) == 0
    qk_off = POOL_WIDTH if with_q else 0
    qk_cols = ss_ref.shape[2]

    def project(j):
        x = x_ref[0, j * rows:(j + 1) * rows, :]
        ms = jnp.mean(x * x, axis=-1, keepdims=True)
        h = (x * lax.rsqrt(ms + EPS) * gain + shift).astype(BF16)
        y_ref[j % 2] = jnp.dot(h, w_ref[...], preferred_element_type=F32)
        for p in range(0, qk_cols, 256):
            t = y_ref[j % 2, :, qk_off + p:qk_off + p + 256]
            ss_ref[j % 2, :, p:p + 256] = jnp.dot((t * t).astype(BF16), bd_ref[...],
                                                  preferred_element_type=F32)

    def epilogue(j):
        blk = slice(j * rows, (j + 1) * rows)
        y = y_ref.at[j % 2]
        ss = ss_ref.at[j % 2]
        cos = cos_ref[blk, :]
        sin_signed = sin_ref[blk, :]
        if with_q:
            pool_ref, q_ref, k_ref, vt_ref = out_refs
            pool_ref[0, blk, :] = y[:, :POOL_WIDTH]
            q_ref[0, blk, :] = _qk_norm_rope(y, ss, qk_off, 0, qg_ref[...], cos, sin_signed,
                                             even_lane).astype(BF16)
            k_at = QK_WIDTH
        else:
            k_ref, vt_ref = out_refs
            k_at = 0
        k_ref[0, blk, :] = _qk_norm_rope(y, ss, qk_off + k_at, k_at, kg_ref[...], cos, sin_signed,
                                         even_lane).astype(BF16)
        vt_ref[0, 0, :, blk] = y[:, qk_off + k_at + QK_WIDTH:].T.astype(BF16)

    _software_pipeline(x_ref.shape[1] // rows, project, epilogue)


def _inproj_call(x, mods, mod_row, g, w, bd, qg, kg, cos, sin_signed, with_q):
    b, s, d = x.shape
    tm = min(TM_PROJ, s)
    nt = s // tm
    dv = N_HEADS * V_DIM
    n = w.shape[2] if with_q else QK_WIDTH + dv
    out_shape = [jax.ShapeDtypeStruct((b, s, QK_WIDTH), BF16),
                 jax.ShapeDtypeStruct((b, nt, dv, tm), BF16)]
    out_specs = [pl.BlockSpec((1, tm, QK_WIDTH), lambda bi, i: (bi, i, 0)),
                 pl.BlockSpec((1, 1, dv, tm), lambda bi, i: (bi, i, 0, 0))]
    if with_q:
        out_shape = [jax.ShapeDtypeStruct((b, s, POOL_WIDTH), F32),
                     jax.ShapeDtypeStruct((b, s, QK_WIDTH), BF16)] + out_shape
        out_specs = [pl.BlockSpec((1, tm, POOL_WIDTH), lambda bi, i: (bi, i, 0)),
                     pl.BlockSpec((1, tm, QK_WIDTH), lambda bi, i: (bi, i, 0))] + out_specs
    return pl.pallas_call(
        functools.partial(_inproj_kernel, with_q=with_q),
        grid=(b, nt),
        in_specs=[
            pl.BlockSpec((1, tm, d), lambda bi, i: (bi, i, 0)),
            pl.BlockSpec((1, 1, mods.shape[-1]), lambda bi, i: (mod_row(bi), 0, 0)),
            _const_spec((1, d)),
            _HBM,
            _const_spec(bd.shape),
            _const_spec(qg.shape),
            _const_spec(kg.shape),
            pl.BlockSpec((tm, LANES), lambda bi, i: (i, 0)),
            pl.BlockSpec((tm, LANES), lambda bi, i: (i, 0)),
        ],
        out_specs=out_specs,
        out_shape=out_shape,
        scratch_shapes=[pltpu.VMEM((2, min(ROWS_PROJ, tm), n), F32),
                        pltpu.VMEM((2, min(ROWS_PROJ, tm), n - dv - (POOL_WIDTH if with_q else 0)), F32),
                        *_stage_scratch(d, n)],
        compiler_params=_params(("arbitrary", "arbitrary")),
        name="inproj_q" if with_q else "inproj_ctx",
    )(x, mods, g, w, bd, qg, kg, cos, sin_signed)


def _attn_kernel(bound_ref, lam_ref, subg_ref, q_ref, k_ref, vt_ref, kc_ref, vct_ref, o_ref,
                 st_ref, l_ref, acc_ref, *, lam_init):
    tq = q_ref.shape[1]
    n_chunks = vt_ref.shape[1]
    tk = vt_ref.shape[-1]

    @pl.when(pl.program_id(0) == 0)
    def _():
        l_ref[...] = jnp.ones(l_ref.shape, F32)
        acc_ref[...] = jnp.zeros(acc_ref.shape, F32)

    bounded = bound_ref[0] <= MAX_UNSHIFTED_SCORE

    def stacked_queries():
        q = q_ref[0]
        lane = lax.broadcasted_iota(jnp.int32, q.shape, 1)
        zero = jnp.zeros_like(q)
        return jnp.concatenate([jnp.where(lane < HEAD_DIM, q, zero),
                                jnp.where(lane >= HEAD_DIM, q, zero)], axis=0)

    def chunks():
        for c in range(n_chunks):
            for j in range(0, tk, ATTN_KEYS):
                yield k_ref[0, c * tk + j:c * tk + j + ATTN_KEYS, :], vt_ref[0, c, :, j:j + ATTN_KEYS]
        for j in range(0, kc_ref.shape[1], ATTN_KEYS):
            yield kc_ref[0, j:j + ATTN_KEYS, :], vct_ref[0, 0, :, j:j + ATTN_KEYS]

    def scores(kb, qq):
        return lax.dot_general(kb, qq, (((1,), (1,)), ((), ())), preferred_element_type=F32)

    def write_previous_tile():
        lam_rows = lam_ref[...]
        a1 = jnp.sum(lam_rows[0:1] * lam_rows[1:2], axis=1, keepdims=True)
        a2 = jnp.sum(lam_rows[2:3] * lam_rows[3:4], axis=1, keepdims=True)
        lam = jnp.exp(a1) - jnp.exp(a2) + lam_init
        l = l_ref[...]
        acc = acc_ref[...]
        ot = acc[:, :tq] / l[:, :tq] - lam * (acc[:, tq:] / l[:, tq:])
        o = ot.T
        ms = jnp.mean(o * o, axis=-1, keepdims=True)
        o_ref[0] = (o * lax.rsqrt(ms + EPS) * (subg_ref[...] * (1.0 - lam_init))).astype(BF16)

    @pl.when(bounded)
    def _():
        l = jnp.zeros((1, 2 * tq), F32)
        acc = jnp.zeros((V_DIM, 2 * tq), F32)
        work = list(chunks())
        qq = stacked_queries()
        st_ref[0, :work[0][0].shape[0], :] = scores(work[0][0], qq)
        write_previous_tile()
        for c, (kb, vtb) in enumerate(work):
            if c + 1 < len(work):
                nxt = work[c + 1][0]
                st_ref[(c + 1) % 2, :nxt.shape[0], :] = scores(nxt, qq)
            p = jnp.exp2(st_ref[c % 2, :kb.shape[0], :])
            l = l + jnp.sum(p, axis=0, keepdims=True)
            acc = acc + jnp.dot(vtb, p.astype(BF16), preferred_element_type=F32)
        l_ref[...] = l
        acc_ref[...] = acc

    @pl.when(jnp.logical_not(bounded))
    def _():
        m = jnp.full((1, 2 * tq), -jnp.inf, F32)
        l = jnp.zeros((1, 2 * tq), F32)
        acc = jnp.zeros((V_DIM, 2 * tq), F32)
        qq = stacked_queries()
        write_previous_tile()
        for kb, vtb in chunks():
            st = scores(kb, qq)
            m_new = jnp.maximum(m, jnp.max(st, axis=0, keepdims=True))
            alpha = jnp.exp2(m - m_new)
            p = jnp.exp2(st - m_new)
            l = alpha * l + jnp.sum(p, axis=0, keepdims=True)
            acc = alpha * acc + jnp.dot(vtb, p.astype(BF16), preferred_element_type=F32)
            m = m_new
        l_ref[...] = l
        acc_ref[...] = acc


def _score_bound(q_gain, k_gain):
    slack = 1.0 + 2.0 ** -5
    return (HEAD_DIM * slack * jnp.max(jnp.abs(q_gain)) * jnp.max(jnp.abs(k_gain))).reshape(1)


def _attn_call(score_bound, lam_rows, sub_g, q, k, vt, kc, vct, lam_init):
    b, s, _ = q.shape
    tq = min(TQ, s)
    n_chunks, tk = vt.shape[1], vt.shape[3]
    ctx_len = kc.shape[1]
    nq = s // tq
    n_tiles = b * N_HEADS * nq

    def tile(t):
        t = jnp.minimum(t, n_tiles - 1)
        return t // (N_HEADS * nq), (t // nq) % N_HEADS, t % nq

    def q_map(t):
        bi, h, i = tile(t)
        return bi, i, h

    def out_map(t):
        bi, h, i = tile(jnp.maximum(t - 1, 0))
        return bi, i, h

    def kv_map(t):
        bi, h, _ = tile(t)
        return bi, 0, h

    def vt_map(t):
        bi, h, _ = tile(t)
        return bi, 0, h, 0

    return pl.pallas_call(
        functools.partial(_attn_kernel, lam_init=lam_init),
        grid=(n_tiles + 1,),
        in_specs=[
            pl.BlockSpec(memory_space=pltpu.SMEM),
            _const_spec(lam_rows.shape),
            _const_spec(sub_g.shape),
            pl.BlockSpec((1, tq, V_DIM), q_map),
            pl.BlockSpec((1, s, V_DIM), kv_map),
            pl.BlockSpec((1, n_chunks, V_DIM, tk), vt_map),
            pl.BlockSpec((1, ctx_len, V_DIM), kv_map),
            pl.BlockSpec((1, 1, V_DIM, ctx_len), vt_map),
        ],
        out_specs=pl.BlockSpec((1, tq, V_DIM), out_map),
        out_shape=jax.ShapeDtypeStruct((b, s, N_HEADS * V_DIM), BF16),
        scratch_shapes=[pltpu.VMEM((2, ATTN_KEYS, 2 * tq), F32), pltpu.VMEM((1, 2 * tq), F32),
                        pltpu.VMEM((V_DIM, 2 * tq), F32)],
        compiler_params=_params(("arbitrary",)),
        name="diff_attn",
    )(score_bound, lam_rows, sub_g, q, k, vt, kc, vct)


def _pool_merge_rows(x_ref, mod, pm_ref, pp_ref, pn_ref, attn_ref, pw_ref, ps_ref, wo_ref, ext_ref, seq):
    d = x_ref.shape[-1]
    tm = x_ref.shape[1]
    i = pl.program_id(1)
    last = pl.num_programs(1) - 1
    ext_ref[0:POOL_HALO, :] = jnp.where(i > 0, pp_ref[0], 0.0)
    ext_ref[POOL_HALO:POOL_HALO + tm, :] = pm_ref[0]
    ext_ref[POOL_HALO + tm:2 * POOL_HALO + tm, :] = jnp.where(i < last, pn_ref[0], 0.0)

    def window_sum(lo, hi, half):
        acc = None
        for off in range(lo, hi + 1):
            piece = ext_ref[pl.ds(POOL_HALO + off, tm), LANES * half:LANES * (half + 1)]
            acc = piece if acc is None else acc + piece
        return acc

    t = i * tm + lax.broadcasted_iota(jnp.int32, (tm, LANES), 0)
    lane = lax.broadcasted_iota(jnp.int32, (tm, LANES), 1)
    low_group = lane < POOL_GROUP
    pooled = []
    for half in range(2):
        w_small, w_big = POOL_WINDOWS[2 * half], POOL_WINDOWS[2 * half + 1]
        s_small = window_sum(-(w_small // 2), w_small // 2 - 1, half)
        s_big = (s_small + window_sum(-(w_big // 2), -(w_small // 2) - 1, half)
                 + window_sum(w_small // 2, w_big // 2 - 1, half))
        reach = jnp.where(low_group, w_small // 2, w_big // 2)
        cnt = jnp.minimum(t + reach, seq) - jnp.maximum(t - reach, 0)
        mean = jnp.where(low_group, s_small, s_big) / cnt.astype(F32)
        pooled.append(mean - pm_ref[0, :, LANES * half:LANES * (half + 1)])
    pooled = jnp.concatenate(pooled, axis=1).astype(BF16)
    pool_y = (jnp.dot(pooled, pw_ref[...], preferred_element_type=F32) * ps_ref[...]).astype(BF16)
    y = (jnp.dot(pool_y, wo_ref[0:POOL_WIDTH, :], preferred_element_type=F32)
         + jnp.dot(attn_ref[0], wo_ref[POOL_WIDTH:, :], preferred_element_type=F32))
    return x_ref[0] + mod[:, 2 * d:3 * d] * y


def _ffn_rows(x, mod, g, w1_ref, w2_ref):
    d = x.shape[-1]
    h = _norm_mod(x, g, mod[:, 3 * d:4 * d], mod[:, 4 * d:5 * d]).astype(BF16)
    a = jnp.maximum(jnp.dot(h, w1_ref[...], preferred_element_type=F32), 0.0)
    a = (a * a).astype(BF16)
    y = jnp.dot(a, w2_ref[...], preferred_element_type=F32)
    return x + mod[:, 5 * d:6 * d] * y


def _ffn_blocks(x_block, o_ref, mod, g, w1_ref, w2_ref):
    tm = o_ref.shape[1]
    rows = min(ROWS_FFN, tm)
    for r in range(tm // rows):
        blk = slice(r * rows, (r + 1) * rows)
        o_ref[0, blk, :] = _ffn_rows(x_block(blk), mod, g, w1_ref, w2_ref)


def _ffn_kernel(x_ref, mod_ref, g_ref, w1_hbm, w2_hbm, o_ref,
                w1_ref, w1_stage, w1_sem, w2_ref, w2_stage, w2_sem, *, layer):
    @pl.when(_first_step())
    def _():
        _stage_weight(w1_hbm, layer, w1_ref, w1_stage, w1_sem)
        _stage_weight(w2_hbm, layer, w2_ref, w2_stage, w2_sem)

    _ffn_blocks(lambda blk: x_ref[0, blk, :], o_ref, mod_ref[0], g_ref[...], w1_ref, w2_ref)


def _merge_ffn_kernel(x_ref, mod_ref, pm_ref, pp_ref, pn_ref, attn_ref, pw_ref, ps_ref, wo_hbm,
                      g_ref, w1_hbm, w2_hbm, o_ref, ext_ref,
                      wo_ref, wo_stage, wo_sem, w1_ref, w1_stage, w1_sem, w2_ref, w2_stage, w2_sem, *,
                      seq, layer):
    @pl.when(_first_step())
    def _():
        _stage_weight(wo_hbm, layer // 2, wo_ref, wo_stage, wo_sem)
        _stage_weight(w1_hbm, layer, w1_ref, w1_stage, w1_sem)
        _stage_weight(w2_hbm, layer, w2_ref, w2_stage, w2_sem)

    mod = mod_ref[0]
    x1 = _pool_merge_rows(x_ref, mod, pm_ref, pp_ref, pn_ref, attn_ref, pw_ref, ps_ref, wo_ref, ext_ref, seq)
    _ffn_blocks(lambda blk: x1[blk, :], o_ref, mod, g_ref[...], w1_ref, w2_ref)


def _merge_ffn_call(x, mods, mod_row, pool_in, attn, pool_w_bd, pool_scale, w_out, g, w1, w2, layer):
    b, s, d = x.shape
    tm = min(TM_FFN, s)
    hb = tm // POOL_HALO
    n_halo = s // POOL_HALO
    return pl.pallas_call(
        functools.partial(_merge_ffn_kernel, seq=s, layer=layer),
        grid=(b, s // tm),
        in_specs=[
            pl.BlockSpec((1, tm, d), lambda bi, i: (bi, i, 0)),
            pl.BlockSpec((1, 1, mods.shape[-1]), lambda bi, i: (mod_row(bi), 0, 0)),
            pl.BlockSpec((1, tm, POOL_WIDTH), lambda bi, i: (bi, i, 0)),
            pl.BlockSpec((1, POOL_HALO, POOL_WIDTH),
                         lambda bi, i: (bi, jnp.maximum(i * hb - 1, 0), 0)),
            pl.BlockSpec((1, POOL_HALO, POOL_WIDTH),
                         lambda bi, i: (bi, jnp.minimum((i + 1) * hb, n_halo - 1), 0)),
            pl.BlockSpec((1, tm, attn.shape[-1]), lambda bi, i: (bi, i, 0)),
            _const_spec(pool_w_bd.shape),
            _const_spec(pool_scale.shape),
            _HBM,
            _const_spec((1, d)),
            _HBM,
            _HBM,
        ],
        out_specs=pl.BlockSpec((1, tm, d), lambda bi, i: (bi, i, 0)),
        out_shape=jax.ShapeDtypeStruct((b, s, d), F32),
        scratch_shapes=[pltpu.VMEM((tm + 2 * POOL_HALO, POOL_WIDTH), F32), *_stage_scratch(*w_out.shape[1:]),
                        *_stage_scratch(*w1.shape[1:]), *_stage_scratch(*w2.shape[1:])],
        compiler_params=_params(("arbitrary", "arbitrary")),
        name="pool_merge_ffn",
    )(x, mods, pool_in, pool_in, pool_in, attn, pool_w_bd, pool_scale, w_out, g, w1, w2)


def _ffn_call(x, mods, mod_row, g, w1, w2, layer):
    b, s, d = x.shape
    tm = min(TM_FFN, s)
    return pl.pallas_call(
        functools.partial(_ffn_kernel, layer=layer),
        grid=(b, s // tm),
        in_specs=[
            pl.BlockSpec((1, tm, d), lambda bi, i: (bi, i, 0)),
            pl.BlockSpec((1, 1, mods.shape[-1]), lambda bi, i: (mod_row(bi), 0, 0)),
            _const_spec((1, d)),
            _HBM,
            _HBM,
        ],
        out_specs=pl.BlockSpec((1, tm, d), lambda bi, i: (bi, i, 0)),
        out_shape=jax.ShapeDtypeStruct((b, s, d), F32),
        scratch_shapes=[*_stage_scratch(*w1.shape[1:]), *_stage_scratch(*w2.shape[1:])],
        compiler_params=_params(("arbitrary", "arbitrary")),
        name="ffn",
    )(x, mods, g, w1, w2)


def _sgmlp_kernel(x_ref, mod_ref, g_ref, win_hbm, lng_ref, lnb_ref, sgw_ref, sgb_ref, wout_hbm,
                  o_ref, a_ref, win_ref, win_stage, win_sem, wout_ref, wout_stage, wout_sem):
    @pl.when(_first_step())
    def _():
        _stage_weight(win_hbm, 0, win_ref, win_stage, win_sem)
        _stage_weight(wout_hbm, 0, wout_ref, wout_stage, wout_sem)

    d = x_ref.shape[-1]
    rows = a_ref.shape[1]
    width = win_ref.shape[1] // 2
    mod = mod_ref[0]
    gain = g_ref[...] * (1.0 + mod[:, d:2 * d])
    shift = mod[:, 0:d]
    gate = mod[:, 2 * d:3 * d]

    def project(j):
        x = x_ref[0, j * rows:(j + 1) * rows, :]
        ms = jnp.mean(x * x, axis=-1, keepdims=True)
        h = (x * lax.rsqrt(ms + EPS) * gain + shift).astype(BF16)
        a_ref[j % 2] = jnp.dot(h, win_ref[...], preferred_element_type=F32)

    def gelu(t):
        return 0.5 * t * (1.0 + lax.erf(t * (1.0 / math.sqrt(2.0))))

    def epilogue(j):
        blk = slice(j * rows, (j + 1) * rows)
        a = a_ref.at[j % 2]
        v = gelu(a[:, width:])
        mu = jnp.mean(v, axis=-1, keepdims=True)
        vc = v - mu
        var = jnp.mean(vc * vc, axis=-1, keepdims=True)
        vn = (vc * lax.rsqrt(var + EPS) * lng_ref[...] + lnb_ref[...]).astype(BF16)
        mixed = []
        for r in range(rows // CHUNK):
            cols = [jnp.dot(sgw_ref[gi],
                            vn[r * CHUNK:(r + 1) * CHUNK, gi * SG_GROUP_DIM:(gi + 1) * SG_GROUP_DIM],
                            preferred_element_type=F32) for gi in range(SG_GROUPS)]
            mixed.append(jnp.concatenate(cols, axis=1) + sgb_ref[...])
        z = (gelu(a[:, :width]) * jnp.concatenate(mixed, axis=0)).astype(BF16)
        y = jnp.dot(z, wout_ref[...], preferred_element_type=F32)
        o_ref[0, blk, :] = x_ref[0, blk, :] + gate * y

    _software_pipeline(x_ref.shape[1] // rows, project, epilogue)


def _sgmlp_call(x, mods, mod_row, g, w_in, ln_g, ln_b, sg_w, sg_b_full, w_out):
    b, s, d = x.shape
    tm = min(TM_SG, s)
    return pl.pallas_call(
        _sgmlp_kernel,
        grid=(b, s // tm),
        in_specs=[
            pl.BlockSpec((1, tm, d), lambda bi, i: (bi, i, 0)),
            pl.BlockSpec((1, 1, mods.shape[-1]), lambda bi, i: (mod_row(bi), 0, 0)),
            _const_spec((1, d)),
            _HBM,
            _const_spec(ln_g.shape),
            _const_spec(ln_b.shape),
            _const_spec(sg_w.shape),
            _const_spec(sg_b_full.shape),
            _HBM,
        ],
        out_specs=pl.BlockSpec((1, tm, d), lambda bi, i: (bi, i, 0)),
        out_shape=jax.ShapeDtypeStruct((b, s, d), F32),
        scratch_shapes=[pltpu.VMEM((2, min(ROWS_SG, tm), w_in.shape[2]), F32),
                        *_stage_scratch(*w_in.shape[1:]), *_stage_scratch(*w_out.shape[1:])],
        compiler_params=_params(("arbitrary", "arbitrary")),
        name="sg_mlp",
    )(x, mods, g, w_in, ln_g, ln_b, sg_w, sg_b_full, w_out)


def _rope_tables(n_tok):
    t = jnp.arange(n_tok, dtype=jnp.int32)
    rows = (t // GRID_W).astype(F32)
    cols = (t % GRID_W).astype(F32)
    pairs = HEAD_DIM // 4
    inv = ROPE_THETA ** (-jnp.arange(pairs, dtype=F32) / pairs)
    ang = jnp.concatenate([rows[:, None] * inv, cols[:, None] * inv], axis=-1)
    cos, sin = jnp.cos(ang), jnp.sin(ang)
    cos2 = jnp.repeat(cos, 2, axis=-1)
    sin2 = jnp.stack([-sin, sin], axis=-1).reshape(n_tok, HEAD_DIM)
    return jnp.tile(cos2, (1, LANES // HEAD_DIM)), jnp.tile(sin2, (1, LANES // HEAD_DIM))


def _block_diag(blocks):
    n = len(blocks)
    rows = []
    for i, blk in enumerate(blocks):
        z = jnp.zeros_like(blk)
        rows.append(jnp.concatenate([blk if j == i else z for j in range(n)], axis=1))
    return jnp.concatenate(rows, axis=0)


def kernel(x, c, ctx, c_ctx, ada_w, ada_b, norm_mix_g, norm_ffn_g, ffn_w1, ffn_w2, ev_w_in, ev_w_out,
           pool_w, pool_scale, q_norm_g, k_norm_g, lam_q1, lam_k1, lam_q2, lam_k2, sub_norm_g,
           od_w_in, sg_ln_g, sg_ln_b, sg_w, sg_b, od_w_out):
    b, s, d = x.shape
    depth = ada_w.shape[0]
    assert depth == 2 and b <= SUBLANES - 1

    cin = jnp.zeros((SUBLANES, d), F32).at[:b].set(c).at[b].set(c_ctx)
    mods = _ada_call(cin, ada_w, ada_b).reshape(depth * SUBLANES, 1, 6 * d)

    def lat_row(layer):
        return lambda bi: layer * SUBLANES + bi

    def ctx_row(layer):
        return lambda bi: layer * SUBLANES + b

    n_groups = QK_WIDTH // HEAD_DIM
    qg = jnp.tile(q_norm_g[0] * (HEAD_DIM ** -0.5 * math.log2(math.e)), n_groups)[None, :]
    kg = jnp.tile(k_norm_g[0], n_groups)[None, :]
    ones = jnp.ones((HEAD_DIM, HEAD_DIM), BF16)
    bd = _block_diag([ones] * (256 // HEAD_DIM))
    cos, sin_signed = _rope_tables(s)
    ctx_len = ctx.shape[1]
    cos_ctx = jnp.ones((ctx_len, LANES), F32)
    sin_ctx = jnp.zeros((ctx_len, LANES), F32)
    g_mix0 = norm_mix_g[0][None, :]

    pool_in, q, k, vt = _inproj_call(x, mods, lat_row(0), g_mix0, ev_w_in, bd, qg, kg, cos, sin_signed, True)
    kc, vct = _inproj_call(ctx, mods, ctx_row(0), g_mix0, ev_w_in, bd, qg, kg, cos_ctx, sin_ctx, False)

    lam_init = 0.8 - 0.6 * math.exp(-0.3 * 0)
    lam_rows = jnp.zeros((SUBLANES, LANES), F32)
    for r, vec in enumerate((lam_q1[0], lam_k1[0], lam_q2[0], lam_k2[0])):
        lam_rows = lam_rows.at[r, :HEAD_DIM].set(vec)
    attn = _attn_call(_score_bound(qg, kg), lam_rows, sub_norm_g[0][None, :], q, k, vt, kc, vct, lam_init)

    pool_w_bd = _block_diag([pool_w[0, gi] for gi in range(len(POOL_WINDOWS))]).astype(BF16)
    x = _merge_ffn_call(x, mods, lat_row(0), pool_in, attn, pool_w_bd, pool_scale[0][None, :],
                        ev_w_out, norm_ffn_g[0][None, :], ffn_w1, ffn_w2, 0)

    sg_b_full = jnp.repeat(sg_b[0].T, SG_GROUP_DIM, axis=1)
    x = _sgmlp_call(x, mods, lat_row(1), norm_mix_g[1][None, :], od_w_in,
                    sg_ln_g[0][None, :], sg_ln_b[0][None, :], sg_w[0].astype(BF16), sg_b_full, od_w_out)
    x = _ffn_call(x, mods, lat_row(1), norm_ffn_g[1][None, :], ffn_w1, ffn_w2, 1)
    return x
```

```python
import functools
import math

import jax
import jax.numpy as jnp
import numpy as np
from jax import lax
from jax.experimental import pallas as pl
from jax.experimental.pallas import tpu as pltpu

F32 = jnp.float32
BF16 = jnp.bfloat16

EPS = 1e-6
GRID_W = 64
ROPE_THETA = 10000.0
POOL_WINDOWS = (2, 4, 8, 16)
POOL_GROUP = 64
POOL_WIDTH = 256
HEAD_DIM = 64
V_DIM = 128
N_HEADS = 6
QK_WIDTH = N_HEADS * 2 * HEAD_DIM
CHUNK = 128
SG_GROUPS = 8
SG_GROUP_DIM = 256

LANES = 128
SUBLANES = 8
POOL_HALO = 8
VMEM_LIMIT = 56 * 1024 * 1024
STAGE_BYTES = 2 * 1024 * 1024

MAX_UNSHIFTED_SCORE = 60.0

TM_PROJ = 1024
ROWS_PROJ = 256
TQ = 512
ATTN_KEYS = 1024
TM_FFN = 512
ROWS_FFN = 256
TM_SG = 1024
ROWS_SG = 256


def _params(sem):
    return pltpu.CompilerParams(dimension_semantics=sem, vmem_limit_bytes=VMEM_LIMIT)


def _const_spec(shape):
    zeros = (0,) * len(shape)
    return pl.BlockSpec(shape, lambda *_: zeros, pipeline_mode=pl.Buffered(1))


def _norm_mod(x, g, shift, scale):
    ms = jnp.mean(x * x, axis=-1, keepdims=True)
    return x * lax.rsqrt(ms + EPS) * (g * (1.0 + scale)) + shift


def _mod_row(mod_ref, row=None):
    if row is None:
        row = pl.program_id(0)
    return mod_ref[0, pl.ds(row, 1), :]


def _mod_spec(mods, layer):
    return pl.BlockSpec((1,) + mods.shape[1:], lambda *_: (layer, 0, 0))


def _software_pipeline(n, produce, consume):
    produce(0)
    for j in range(n):
        if j + 1 < n:
            produce(j + 1)
        consume(j)


def _first_step():
    return jnp.logical_and(pl.program_id(0) == 0, pl.program_id(1) == 0)


def _stage_weight(w_hbm, layer, dst_ref, stage_ref, sem, col0=0):
    k, n = dst_ref.shape
    slab = stage_ref.shape[1]
    n_slabs = k // slab

    def copy(i):
        return pltpu.make_async_copy(w_hbm.at[layer, pl.ds(i * slab, slab), pl.ds(col0, n)],
                                     stage_ref.at[i % 2], sem.at[i % 2])

    copy(0).start()
    for i in range(n_slabs):
        if i + 1 < n_slabs:
            copy(i + 1).start()
        copy(i).wait()
        dst_ref[i * slab:(i + 1) * slab, :] = stage_ref[i % 2].astype(BF16)


def _stage_scratch(k, n):
    slab = max(r for r in range(SUBLANES, k + 1, SUBLANES) if k % r == 0 and (r == SUBLANES or 4 * r * n <= STAGE_BYTES))
    return [pltpu.VMEM((k, n), BF16), pltpu.VMEM((2, slab, n), F32), pltpu.SemaphoreType.DMA((2,))]


_HBM = pl.BlockSpec(memory_space=pl.ANY)


def _ada_kernel(c_ref, cctx_ref, w_ref, b_ref, o_ref):
    pad = jnp.zeros((SUBLANES - c_ref.shape[0] - 1, c_ref.shape[1]), F32)
    c = jnp.concatenate([c_ref[...], cctx_ref[...], pad], axis=0)
    sc = (c * jax.nn.sigmoid(c)).astype(BF16)
    o_ref[0] = jnp.dot(sc, w_ref[0].astype(BF16), preferred_element_type=F32) + b_ref[0]


def _ada_call(c, c_ctx, ada_w, ada_b):
    depth, d, n = ada_w.shape
    tn = 1536
    return pl.pallas_call(
        _ada_kernel,
        grid=(depth, n // tn),
        in_specs=[
            pl.BlockSpec(c.shape, lambda l, j: (0, 0)),
            pl.BlockSpec((1, d), lambda l, j: (0, 0)),
            pl.BlockSpec((1, d, tn), lambda l, j: (l, 0, j)),
            pl.BlockSpec((1, 1, tn), lambda l, j: (l, 0, j)),
        ],
        out_specs=pl.BlockSpec((1, SUBLANES, tn), lambda l, j: (l, 0, j)),
        out_shape=jax.ShapeDtypeStruct((depth, SUBLANES, n), F32),
        compiler_params=_params(("parallel", "parallel")),
        name="ada_ln",
    )(c, c_ctx.reshape(1, d), ada_w, ada_b.reshape(depth, 1, n))


def _qk_norm_rope(y_ref, ss_ref, off, ss_off, g, cos, sin_signed, even_lane):
    outs = []
    for j in range(QK_WIDTH // LANES):
        cols = slice(LANES * j, LANES * (j + 1))
        ss = ss_ref[:, ss_off + LANES * j:ss_off + LANES * (j + 1)]
        xc = (y_ref[:, off + LANES * j:off + LANES * (j + 1)]
              * lax.rsqrt(ss * (1.0 / HEAD_DIM) + EPS) * g[:, cols])
        partner = jnp.where(even_lane, pltpu.roll(xc, LANES - 1, 1), pltpu.roll(xc, 1, 1))
        outs.append(xc * cos + partner * sin_signed)
    return jnp.concatenate(outs, axis=1)


def _inproj_kernel(x_ref, mod_ref, g_ref, w_hbm, bd_ref, qg_ref, kg_ref, cos_ref, sin_ref,
                   *refs, with_q, mod_row):
    *out_refs, y_ref, ss_ref, w_ref, w_stage, w_sem = refs

    @pl.when(_first_step())
    def _():
        _stage_weight(w_hbm, 0, w_ref, w_stage, w_sem, col0=0 if with_q else POOL_WIDTH + QK_WIDTH)

    d = x_ref.shape[-1]
    rows = y_ref.shape[1]
    mod = _mod_row(mod_ref, mod_row)
    gain = g_ref[...] * (1.0 + mod[:, d:2 * d])
    shift = mod[:, 0:d]
    lane = lax.broadcasted_iota(jnp.int32, (rows, LANES), 1)
    even_lane = (lane % 2) == 0
    qk_off = POOL_WIDTH if with_q else 0
    qk_cols = ss_ref.shape[2]

    def project(j):
        x = x_ref[0, j * rows:(j + 1) * rows, :]
        ms = jnp.mean(x * x, axis=-1, keepdims=True)
        h = (x * lax.rsqrt(ms + EPS) * gain + shift).astype(BF16)
        y_ref[j % 2] = jnp.dot(h, w_ref[...], preferred_element_type=F32)
        for p in range(0, qk_cols, 256):
            t = y_ref[j % 2, :, qk_off + p:qk_off + p + 256]
            ss_ref[j % 2, :, p:p + 256] = jnp.dot((t * t).astype(BF16), bd_ref[...],
                                                  preferred_element_type=F32)

    def epilogue(j):
        blk = slice(j * rows, (j + 1) * rows)
        y = y_ref.at[j % 2]
        ss = ss_ref.at[j % 2]
        cos = cos_ref[blk, :]
        sin_signed = sin_ref[blk, :]
        if with_q:
            pool_ref, q_ref, k_ref, vt_ref = out_refs
            pool_ref[0, blk, :] = y[:, :POOL_WIDTH]
            q_ref[0, blk, :] = _qk_norm_rope(y, ss, qk_off, 0, qg_ref[...], cos, sin_signed,
                                             even_lane).astype(BF16)
            k_at = QK_WIDTH
        else:
            k_ref, vt_ref = out_refs
            k_at = 0
        k_ref[0, blk, :] = _qk_norm_rope(y, ss, qk_off + k_at, k_at, kg_ref[...], cos, sin_signed,
                                         even_lane).astype(BF16)
        vt_ref[0, 0, :, blk] = y[:, qk_off + k_at + QK_WIDTH:].T.astype(BF16)

    _software_pipeline(x_ref.shape[1] // rows, project, epilogue)


def _inproj_call(x, mods, layer, mod_row, g, w, bd, qg, kg, cos, sin_signed, with_q):
    b, s, d = x.shape
    tm = min(TM_PROJ, s)
    nt = s // tm
    dv = N_HEADS * V_DIM
    n = w.shape[2] if with_q else QK_WIDTH + dv
    out_shape = [jax.ShapeDtypeStruct((b, s, QK_WIDTH), BF16),
                 jax.ShapeDtypeStruct((b, nt, dv, tm), BF16)]
    out_specs = [pl.BlockSpec((1, tm, QK_WIDTH), lambda bi, i: (bi, i, 0)),
                 pl.BlockSpec((1, 1, dv, tm), lambda bi, i: (bi, i, 0, 0))]
    if with_q:
        out_shape = [jax.ShapeDtypeStruct((b, s, POOL_WIDTH), F32),
                     jax.ShapeDtypeStruct((b, s, QK_WIDTH), BF16)] + out_shape
        out_specs = [pl.BlockSpec((1, tm, POOL_WIDTH), lambda bi, i: (bi, i, 0)),
                     pl.BlockSpec((1, tm, QK_WIDTH), lambda bi, i: (bi, i, 0))] + out_specs
    return pl.pallas_call(
        functools.partial(_inproj_kernel, with_q=with_q, mod_row=mod_row),
        grid=(b, nt),
        in_specs=[
            pl.BlockSpec((1, tm, d), lambda bi, i: (bi, i, 0)),
            _mod_spec(mods, layer),
            _const_spec((1, d)),
            _HBM,
            _const_spec(bd.shape),
            _const_spec(qg.shape),
            _const_spec(kg.shape),
            pl.BlockSpec((tm, LANES), lambda bi, i: (i, 0)),
            pl.BlockSpec((tm, LANES), lambda bi, i: (i, 0)),
        ],
        out_specs=out_specs,
        out_shape=out_shape,
        scratch_shapes=[pltpu.VMEM((2, min(ROWS_PROJ, tm), n), F32),
                        pltpu.VMEM((2, min(ROWS_PROJ, tm), n - dv - (POOL_WIDTH if with_q else 0)), F32),
                        *_stage_scratch(d, n)],
        compiler_params=_params(("arbitrary", "arbitrary")),
        name="inproj_q" if with_q else "inproj_ctx",
    )(x, mods, g, w, bd, qg, kg, cos, sin_signed)


def _attn_kernel(bound_ref, lam_ref, subg_ref, q_ref, k_ref, vt_ref, kc_ref, vct_ref, o_ref,
                 st_ref, l_ref, acc_ref, *, lam_init):
    tq = q_ref.shape[1]
    n_chunks = vt_ref.shape[1]
    tk = vt_ref.shape[-1]

    @pl.when(pl.program_id(0) == 0)
    def _():
        l_ref[...] = jnp.ones(l_ref.shape, F32)
        acc_ref[...] = jnp.zeros(acc_ref.shape, F32)

    bounded = bound_ref[0] <= MAX_UNSHIFTED_SCORE

    def stacked_queries():
        q = q_ref[0]
        lane = lax.broadcasted_iota(jnp.int32, q.shape, 1)
        zero = jnp.zeros_like(q)
        return jnp.concatenate([jnp.where(lane < HEAD_DIM, q, zero),
                                jnp.where(lane >= HEAD_DIM, q, zero)], axis=0)

    def chunks():
        for c in range(n_chunks):
            for j in range(0, tk, ATTN_KEYS):
                yield k_ref[0, c * tk + j:c * tk + j + ATTN_KEYS, :], vt_ref[0, c, :, j:j + ATTN_KEYS]
        for j in range(0, kc_ref.shape[1], ATTN_KEYS):
            yield kc_ref[0, j:j + ATTN_KEYS, :], vct_ref[0, 0, :, j:j + ATTN_KEYS]

    def scores(kb, qq):
        return lax.dot_general(kb, qq, (((1,), (1,)), ((), ())), preferred_element_type=F32)

    def write_previous_tile():
        lam_rows = lam_ref[...]
        a1 = jnp.sum(lam_rows[0:1] * lam_rows[1:2], axis=1, keepdims=True)
        a2 = jnp.sum(lam_rows[2:3] * lam_rows[3:4], axis=1, keepdims=True)
        lam = jnp.exp(a1) - jnp.exp(a2) + lam_init
        l = l_ref[...]
        acc = acc_ref[...]
        ot = acc[:, :tq] / l[:, :tq] - lam * (acc[:, tq:] / l[:, tq:])
        o = ot.T
        ms = jnp.mean(o * o, axis=-1, keepdims=True)
        o_ref[0] = (o * lax.rsqrt(ms + EPS) * (subg_ref[...] * (1.0 - lam_init))).astype(BF16)

    @pl.when(bounded)
    def _():
        l = jnp.zeros((1, 2 * tq), F32)
        acc = jnp.zeros((V_DIM, 2 * tq), F32)
        work = list(chunks())
        qq = stacked_queries()
        st_ref[0, :work[0][0].shape[0], :] = scores(work[0][0], qq)
        write_previous_tile()
        for c, (kb, vtb) in enumerate(work):
            if c + 1 < len(work):
                nxt = work[c + 1][0]
                st_ref[(c + 1) % 2, :nxt.shape[0], :] = scores(nxt, qq)
            p = jnp.exp2(st_ref[c % 2, :kb.shape[0], :])
            l = l + jnp.sum(p, axis=0, keepdims=True)
            acc = acc + jnp.dot(vtb, p.astype(BF16), preferred_element_type=F32)
        l_ref[...] = l
        acc_ref[...] = acc

    @pl.when(jnp.logical_not(bounded))
    def _():
        m = jnp.full((1, 2 * tq), -jnp.inf, F32)
        l = jnp.zeros((1, 2 * tq), F32)
        acc = jnp.zeros((V_DIM, 2 * tq), F32)
        qq = stacked_queries()
        write_previous_tile()
        for kb, vtb in chunks():
            st = scores(kb, qq)
            m_new = jnp.maximum(m, jnp.max(st, axis=0, keepdims=True))
            alpha = jnp.exp2(m - m_new)
            p = jnp.exp2(st - m_new)
            l = alpha * l + jnp.sum(p, axis=0, keepdims=True)
            acc = alpha * acc + jnp.dot(vtb, p.astype(BF16), preferred_element_type=F32)
            m = m_new
        l_ref[...] = l
        acc_ref[...] = acc


def _score_bound(q_gain, k_gain):
    slack = 1.0 + 2.0 ** -5
    return (HEAD_DIM * slack * jnp.max(jnp.abs(q_gain)) * jnp.max(jnp.abs(k_gain))).reshape(1)


def _attn_call(score_bound, lam_rows, sub_g, q, k, vt, kc, vct, lam_init):
    b, s, _ = q.shape
    tq = min(TQ, s)
    n_chunks, tk = vt.shape[1], vt.shape[3]
    ctx_len = kc.shape[1]
    nq = s // tq
    n_tiles = b * N_HEADS * nq

    def tile(t):
        t = jnp.minimum(t, n_tiles - 1)
        return t // (N_HEADS * nq), (t // nq) % N_HEADS, t % nq

    def q_map(t):
        bi, h, i = tile(t)
        return bi, i, h

    def out_map(t):
        bi, h, i = tile(jnp.maximum(t - 1, 0))
        return bi, i, h

    def kv_map(t):
        bi, h, _ = tile(t)
        return bi, 0, h

    def vt_map(t):
        bi, h, _ = tile(t)
        return bi, 0, h, 0

    return pl.pallas_call(
        functools.partial(_attn_kernel, lam_init=lam_init),
        grid=(n_tiles + 1,),
        in_specs=[
            pl.BlockSpec(memory_space=pltpu.SMEM),
            _const_spec(lam_rows.shape),
            _const_spec(sub_g.shape),
            pl.BlockSpec((1, tq, V_DIM), q_map),
            pl.BlockSpec((1, s, V_DIM), kv_map),
            pl.BlockSpec((1, n_chunks, V_DIM, tk), vt_map),
            pl.BlockSpec((1, ctx_len, V_DIM), kv_map),
            pl.BlockSpec((1, 1, V_DIM, ctx_len), vt_map),
        ],
        out_specs=pl.BlockSpec((1, tq, V_DIM), out_map),
        out_shape=jax.ShapeDtypeStruct((b, s, N_HEADS * V_DIM), BF16),
        scratch_shapes=[pltpu.VMEM((2, ATTN_KEYS, 2 * tq), F32), pltpu.VMEM((1, 2 * tq), F32),
                        pltpu.VMEM((V_DIM, 2 * tq), F32)],
        compiler_params=_params(("arbitrary",)),
        name="diff_attn",
    )(score_bound, lam_rows, sub_g, q, k, vt, kc, vct)


def _fill_pool_halo(pm_ref, pp_ref, pn_ref, ext_ref):
    tm = pm_ref.shape[1]
    i = pl.program_id(1)
    last = pl.num_programs(1) - 1
    ext_ref[0:POOL_HALO, :] = jnp.where(i > 0, pp_ref[0], 0.0)
    ext_ref[POOL_HALO:POOL_HALO + tm, :] = pm_ref[0]
    ext_ref[POOL_HALO + tm:2 * POOL_HALO + tm, :] = jnp.where(i < last, pn_ref[0], 0.0)


def _pool_merge_rows(blk, x_ref, mod, attn_ref, pw_ref, ps_ref, wo_ref, ext_ref, seq):
    d = x_ref.shape[-1]
    tm = x_ref.shape[1]
    rows = blk.stop - blk.start

    def window_sum(lo, hi, half):
        acc = None
        for off in range(lo, hi + 1):
            piece = ext_ref[pl.ds(POOL_HALO + blk.start + off, rows), LANES * half:LANES * (half + 1)]
            acc = piece if acc is None else acc + piece
        return acc

    t = pl.program_id(1) * tm + blk.start + lax.broadcasted_iota(jnp.int32, (rows, LANES), 0)
    lane = lax.broadcasted_iota(jnp.int32, (rows, LANES), 1)
    low_group = lane < POOL_GROUP
    pooled = []
    for half in range(2):
        w_small, w_big = POOL_WINDOWS[2 * half], POOL_WINDOWS[2 * half + 1]
        s_small = window_sum(-(w_small // 2), w_small // 2 - 1, half)
        s_big = (s_small + window_sum(-(w_big // 2), -(w_small // 2) - 1, half)
                 + window_sum(w_small // 2, w_big // 2 - 1, half))
        reach = jnp.where(low_group, w_small // 2, w_big // 2)
        cnt = jnp.minimum(t + reach, seq) - jnp.maximum(t - reach, 0)
        mean = jnp.where(low_group, s_small, s_big) / cnt.astype(F32)
        pooled.append(mean - ext_ref[pl.ds(POOL_HALO + blk.start, rows), LANES * half:LANES * (half + 1)])
    pooled = jnp.concatenate(pooled, axis=1).astype(BF16)
    pool_y = (jnp.dot(pooled, pw_ref[...], preferred_element_type=F32) * ps_ref[...]).astype(BF16)
    y = (jnp.dot(pool_y, wo_ref[0:POOL_WIDTH, :], preferred_element_type=F32)
         + jnp.dot(attn_ref[0, blk, :], wo_ref[POOL_WIDTH:, :], preferred_element_type=F32))
    return x_ref[0, blk, :] + mod[:, 2 * d:3 * d] * y


def _ffn_rows(x, mod, g, w1_ref, w2_ref):
    d = x.shape[-1]
    h = _norm_mod(x, g, mod[:, 3 * d:4 * d], mod[:, 4 * d:5 * d]).astype(BF16)
    a = jnp.maximum(jnp.dot(h, w1_ref[...], preferred_element_type=F32), 0.0)
    a = (a * a).astype(BF16)
    y = jnp.dot(a, w2_ref[...], preferred_element_type=F32)
    return x + mod[:, 5 * d:6 * d] * y


def _ffn_blocks(x_block, o_ref, mod, g, w1_ref, w2_ref):
    tm = o_ref.shape[1]
    rows = min(ROWS_FFN, tm)
    for r in range(tm // rows):
        blk = slice(r * rows, (r + 1) * rows)
        o_ref[0, blk, :] = _ffn_rows(x_block(blk), mod, g, w1_ref, w2_ref)


def _ffn_kernel(x_ref, mod_ref, g_ref, w1_hbm, w2_hbm, o_ref,
                w1_ref, w1_stage, w1_sem, w2_ref, w2_stage, w2_sem, *, layer):
    @pl.when(_first_step())
    def _():
        _stage_weight(w1_hbm, layer, w1_ref, w1_stage, w1_sem)
        _stage_weight(w2_hbm, layer, w2_ref, w2_stage, w2_sem)

    _ffn_blocks(lambda blk: x_ref[0, blk, :], o_ref, _mod_row(mod_ref), g_ref[...], w1_ref, w2_ref)


def _merge_ffn_kernel(x_ref, mod_ref, pm_ref, pp_ref, pn_ref, attn_ref, pw_ref, ps_ref, wo_hbm,
                      g_ref, w1_hbm, w2_hbm, o_ref, ext_ref, x1_ref,
                      wo_ref, wo_stage, wo_sem, w1_ref, w1_stage, w1_sem, w2_ref, w2_stage, w2_sem, *,
                      seq, layer):
    @pl.when(_first_step())
    def _():
        _stage_weight(wo_hbm, layer // 2, wo_ref, wo_stage, wo_sem)
        _stage_weight(w1_hbm, layer, w1_ref, w1_stage, w1_sem)
        _stage_weight(w2_hbm, layer, w2_ref, w2_stage, w2_sem)

    mod = _mod_row(mod_ref)
    rows = x1_ref.shape[1]
    _fill_pool_halo(pm_ref, pp_ref, pn_ref, ext_ref)

    def merge(j):
        blk = slice(j * rows, (j + 1) * rows)
        x1_ref[j % 2] = _pool_merge_rows(blk, x_ref, mod, attn_ref, pw_ref, ps_ref, wo_ref, ext_ref, seq)

    def ffn(j):
        o_ref[0, j * rows:(j + 1) * rows, :] = _ffn_rows(x1_ref[j % 2], mod, g_ref[...], w1_ref, w2_ref)

    _software_pipeline(x_ref.shape[1] // rows, merge, ffn)


def _merge_ffn_call(x, mods, pool_in, attn, pool_w_bd, pool_scale, w_out, g, w1, w2, layer):
    b, s, d = x.shape
    tm = min(TM_FFN, s)
    hb = tm // POOL_HALO
    n_halo = s // POOL_HALO
    return pl.pallas_call(
        functools.partial(_merge_ffn_kernel, seq=s, layer=layer),
        grid=(b, s // tm),
        in_specs=[
            pl.BlockSpec((1, tm, d), lambda bi, i: (bi, i, 0)),
            _mod_spec(mods, layer),
            pl.BlockSpec((1, tm, POOL_WIDTH), lambda bi, i: (bi, i, 0)),
            pl.BlockSpec((1, POOL_HALO, POOL_WIDTH),
                         lambda bi, i: (bi, jnp.maximum(i * hb - 1, 0), 0)),
            pl.BlockSpec((1, POOL_HALO, POOL_WIDTH),
                         lambda bi, i: (bi, jnp.minimum((i + 1) * hb, n_halo - 1), 0)),
            pl.BlockSpec((1, tm, attn.shape[-1]), lambda bi, i: (bi, i, 0)),
            _const_spec(pool_w_bd.shape),
            _const_spec(pool_scale.shape),
            _HBM,
            _const_spec((1, d)),
            _HBM,
            _HBM,
        ],
        out_specs=pl.BlockSpec((1, tm, d), lambda bi, i: (bi, i, 0)),
        out_shape=jax.ShapeDtypeStruct((b, s, d), F32),
        scratch_shapes=[pltpu.VMEM((tm + 2 * POOL_HALO, POOL_WIDTH), F32),
                        pltpu.VMEM((2, min(ROWS_FFN, tm), d), F32), *_stage_scratch(*w_out.shape[1:]),
                        *_stage_scratch(*w1.shape[1:]), *_stage_scratch(*w2.shape[1:])],
        compiler_params=_params(("arbitrary", "arbitrary")),
        name="pool_merge_ffn",
    )(x, mods, pool_in, pool_in, pool_in, attn, pool_w_bd, pool_scale, w_out, g, w1, w2)


def _ffn_call(x, mods, g, w1, w2, layer):
    b, s, d = x.shape
    tm = min(TM_FFN, s)
    return pl.pallas_call(
        functools.partial(_ffn_kernel, layer=layer),
        grid=(b, s // tm),
        in_specs=[
            pl.BlockSpec((1, tm, d), lambda bi, i: (bi, i, 0)),
            _mod_spec(mods, layer),
            _const_spec((1, d)),
            _HBM,
            _HBM,
        ],
        out_specs=pl.BlockSpec((1, tm, d), lambda bi, i: (bi, i, 0)),
        out_shape=jax.ShapeDtypeStruct((b, s, d), F32),
        scratch_shapes=[*_stage_scratch(*w1.shape[1:]), *_stage_scratch(*w2.shape[1:])],
        compiler_params=_params(("arbitrary", "arbitrary")),
        name="ffn",
    )(x, mods, g, w1, w2)


def _sgmlp_kernel(x_ref, mod_ref, g_ref, win_hbm, lng_ref, lnb_ref, sgw_ref, sgb_ref, wout_hbm,
                  o_ref, a_ref, win_ref, win_stage, win_sem, wout_ref, wout_stage, wout_sem):
    @pl.when(_first_step())
    def _():
        _stage_weight(win_hbm, 0, win_ref, win_stage, win_sem)
        _stage_weight(wout_hbm, 0, wout_ref, wout_stage, wout_sem)

    d = x_ref.shape[-1]
    rows = a_ref.shape[1]
    width = win_ref.shape[1] // 2
    mod = _mod_row(mod_ref)
    gain = g_ref[...] * (1.0 + mod[:, d:2 * d])
    shift = mod[:, 0:d]
    gate = mod[:, 2 * d:3 * d]

    def project(j):
        x = x_ref[0, j * rows:(j + 1) * rows, :]
        ms = jnp.mean(x * x, axis=-1, keepdims=True)
        h = (x * lax.rsqrt(ms + EPS) * gain + shift).astype(BF16)
        a_ref[j % 2] = jnp.dot(h, win_ref[...], preferred_element_type=F32)

    def gelu(t):
        return 0.5 * t * (1.0 + lax.erf(t * (1.0 / math.sqrt(2.0))))

    def epilogue(j):
        blk = slice(j * rows, (j + 1) * rows)
        a = a_ref.at[j % 2]
        v = gelu(a[:, width:])
        mu = jnp.mean(v, axis=-1, keepdims=True)
        vc = v - mu
        var = jnp.mean(vc * vc, axis=-1, keepdims=True)
        vn = (vc * lax.rsqrt(var + EPS) * lng_ref[...] + lnb_ref[...]).astype(BF16)
        mixed = []
        for r in range(rows // CHUNK):
            cols = [jnp.dot(sgw_ref[gi],
                            vn[r * CHUNK:(r + 1) * CHUNK, gi * SG_GROUP_DIM:(gi + 1) * SG_GROUP_DIM],
                            preferred_element_type=F32) for gi in range(SG_GROUPS)]
            mixed.append(jnp.concatenate(cols, axis=1) + sgb_ref[...])
        z = (gelu(a[:, :width]) * jnp.concatenate(mixed, axis=0)).astype(BF16)
        y = jnp.dot(z, wout_ref[...], preferred_element_type=F32)
        o_ref[0, blk, :] = x_ref[0, blk, :] + gate * y

    _software_pipeline(x_ref.shape[1] // rows, project, epilogue)


def _sgmlp_call(x, mods, layer, g, w_in, ln_g, ln_b, sg_w, sg_b_full, w_out):
    b, s, d = x.shape
    tm = min(TM_SG, s)
    return pl.pallas_call(
        _sgmlp_kernel,
        grid=(b, s // tm),
        in_specs=[
            pl.BlockSpec((1, tm, d), lambda bi, i: (bi, i, 0)),
            _mod_spec(mods, layer),
            _const_spec((1, d)),
            _HBM,
            _const_spec(ln_g.shape),
            _const_spec(ln_b.shape),
            _const_spec(sg_w.shape),
            _const_spec(sg_b_full.shape),
            _HBM,
        ],
        out_specs=pl.BlockSpec((1, tm, d), lambda bi, i: (bi, i, 0)),
        out_shape=jax.ShapeDtypeStruct((b, s, d), F32),
        scratch_shapes=[pltpu.VMEM((2, min(ROWS_SG, tm), w_in.shape[2]), F32),
                        *_stage_scratch(*w_in.shape[1:]), *_stage_scratch(*w_out.shape[1:])],
        compiler_params=_params(("arbitrary", "arbitrary")),
        name="sg_mlp",
    )(x, mods, g, w_in, ln_g, ln_b, sg_w, sg_b_full, w_out)


def _rope_tables(n_tok):
    t = np.arange(n_tok, dtype=np.int32)
    rows = (t // GRID_W).astype(np.float32)
    cols = (t % GRID_W).astype(np.float32)
    pairs = HEAD_DIM // 4
    inv = (np.float32(ROPE_THETA) ** (-np.arange(pairs, dtype=np.float32) / np.float32(pairs))).astype(np.float32)
    ang = np.concatenate([rows[:, None] * inv, cols[:, None] * inv], axis=-1)
    cos, sin = np.cos(ang), np.sin(ang)
    cos2 = np.repeat(cos, 2, axis=-1)
    sin2 = np.stack([-sin, sin], axis=-1).reshape(n_tok, HEAD_DIM)
    return (jnp.asarray(np.tile(cos2, (1, LANES // HEAD_DIM)), F32),
            jnp.asarray(np.tile(sin2, (1, LANES // HEAD_DIM)), F32))


def _block_diag(blocks):
    n = len(blocks)
    rows = []
    for i, blk in enumerate(blocks):
        z = jnp.zeros_like(blk)
        rows.append(jnp.concatenate([blk if j == i else z for j in range(n)], axis=1))
    return jnp.concatenate(rows, axis=0)


def kernel(x, c, ctx, c_ctx, ada_w, ada_b, norm_mix_g, norm_ffn_g, ffn_w1, ffn_w2, ev_w_in, ev_w_out,
           pool_w, pool_scale, q_norm_g, k_norm_g, lam_q1, lam_k1, lam_q2, lam_k2, sub_norm_g,
           od_w_in, sg_ln_g, sg_ln_b, sg_w, sg_b, od_w_out):
    b, s, d = x.shape
    depth = ada_w.shape[0]
    assert depth == 2 and b <= SUBLANES - 1

    mods = _ada_call(c, c_ctx, ada_w, ada_b)

    n_groups = QK_WIDTH // HEAD_DIM
    qg = jnp.tile(q_norm_g[0] * (HEAD_DIM ** -0.5 * math.log2(math.e)), n_groups)[None, :]
    kg = jnp.tile(k_norm_g[0], n_groups)[None, :]
    ones = jnp.ones((HEAD_DIM, HEAD_DIM), BF16)
    bd = _block_diag([ones] * (256 // HEAD_DIM))
    cos, sin_signed = _rope_tables(s)
    ctx_len = ctx.shape[1]
    cos_ctx = jnp.ones((ctx_len, LANES), F32)
    sin_ctx = jnp.zeros((ctx_len, LANES), F32)
    g_mix0 = norm_mix_g[0][None, :]

    pool_in, q, k, vt = _inproj_call(x, mods, 0, None, g_mix0, ev_w_in, bd, qg, kg, cos, sin_signed, True)
    kc, vct = _inproj_call(ctx, mods, 0, b, g_mix0, ev_w_in, bd, qg, kg, cos_ctx, sin_ctx, False)

    lam_init = 0.8 - 0.6 * math.exp(-0.3 * 0)
    lam_rows = jnp.pad(jnp.stack([lam_q1[0], lam_k1[0], lam_q2[0], lam_k2[0]]),
                       ((0, SUBLANES - 4), (0, LANES - HEAD_DIM)))
    attn = _attn_call(_score_bound(qg, kg), lam_rows, sub_norm_g[0][None, :], q, k, vt, kc, vct, lam_init)

    pool_w_bd = _block_diag([pool_w[0, gi] for gi in range(len(POOL_WINDOWS))]).astype(BF16)
    x = _merge_ffn_call(x, mods, pool_in, attn, pool_w_bd, pool_scale[0][None, :],
                        ev_w_out, norm_ffn_g[0][None, :], ffn_w1, ffn_w2, 0)

    sg_b_full = jnp.repeat(sg_b[0].T, SG_GROUP_DIM, axis=1)
    x = _sgmlp_call(x, mods, 1, norm_mix_g[1][None, :], od_w_in,
                    sg_ln_g[0][None, :], sg_ln_b[0][None, :], sg_w[0].astype(BF16), sg_b_full, od_w_out)
    x = _ffn_call(x, mods, norm_ffn_g[1][None, :], ffn_w1, ffn_w2, 1)
    return x
```

```python
import functools
import math

import jax
import jax.numpy as jnp
import numpy as np
from jax import lax
from jax.experimental import pallas as pl
from jax.experimental.pallas import tpu as pltpu

F32 = jnp.float32
BF16 = jnp.bfloat16

EPS = 1e-6
GRID_W = 64
ROPE_THETA = 10000.0
POOL_WINDOWS = (2, 4, 8, 16)
POOL_GROUP = 64
POOL_WIDTH = 256
HEAD_DIM = 64
V_DIM = 128
N_HEADS = 6
QK_WIDTH = N_HEADS * 2 * HEAD_DIM
CHUNK = 128
SG_GROUPS = 8
SG_GROUP_DIM = 256

LANES = 128
SUBLANES = 8
POOL_HALO = 8
VMEM_LIMIT = 56 * 1024 * 1024
STAGE_BYTES = 2 * 1024 * 1024

MAX_UNSHIFTED_SCORE = 60.0

TM_PROJ = 1024
ROWS_PROJ = 256
TQ = 512
TILES_PER_STEP = 2
ATTN_KEYS = 1024
TM_FFN = 512
ROWS_FFN = 256
TM_SG = 1024
ROWS_SG = 256


def _params(sem):
    return pltpu.CompilerParams(dimension_semantics=sem, vmem_limit_bytes=VMEM_LIMIT)


def _const_spec(shape):
    zeros = (0,) * len(shape)
    return pl.BlockSpec(shape, lambda *_: zeros, pipeline_mode=pl.Buffered(1))


def _norm_mod(x, g, shift, scale):
    ms = jnp.mean(x * x, axis=-1, keepdims=True)
    return x * lax.rsqrt(ms + EPS) * (g * (1.0 + scale)) + shift


def _mod_row(mod_ref, row=None):
    if row is None:
        row = pl.program_id(0)
    return mod_ref[0, pl.ds(row, 1), :]


def _mod_spec(mods, layer):
    return pl.BlockSpec((1,) + mods.shape[1:], lambda *_: (layer, 0, 0))


def _software_pipeline(n, produce, consume):
    produce(0)
    for j in range(n):
        if j + 1 < n:
            produce(j + 1)
        consume(j)


def _first_step():
    return jnp.logical_and(pl.program_id(0) == 0, pl.program_id(1) == 0)


def _stage_weight(w_hbm, layer, dst_ref, stage_ref, sem, col0=0):
    k, n = dst_ref.shape
    slab = stage_ref.shape[1]
    n_slabs = k // slab

    def copy(i):
        return pltpu.make_async_copy(w_hbm.at[layer, pl.ds(i * slab, slab), pl.ds(col0, n)],
                                     stage_ref.at[i % 2], sem.at[i % 2])

    copy(0).start()
    for i in range(n_slabs):
        if i + 1 < n_slabs:
            copy(i + 1).start()
        copy(i).wait()
        dst_ref[i * slab:(i + 1) * slab, :] = stage_ref[i % 2].astype(BF16)


def _stage_scratch(k, n):
    slab = max(r for r in range(SUBLANES, k + 1, SUBLANES) if k % r == 0 and (r == SUBLANES or 4 * r * n <= STAGE_BYTES))
    return [pltpu.VMEM((k, n), BF16), pltpu.VMEM((2, slab, n), F32), pltpu.SemaphoreType.DMA((2,))]


_HBM = pl.BlockSpec(memory_space=pl.ANY)


def _ada_kernel(c_ref, cctx_ref, w_ref, b_ref, o_ref):
    pad = jnp.zeros((SUBLANES - c_ref.shape[0] - 1, c_ref.shape[1]), F32)
    c = jnp.concatenate([c_ref[...], cctx_ref[...], pad], axis=0)
    sc = (c * jax.nn.sigmoid(c)).astype(BF16)
    o_ref[0] = jnp.dot(sc, w_ref[0].astype(BF16), preferred_element_type=F32) + b_ref[0]


def _ada_call(c, c_ctx, ada_w, ada_b):
    depth, d, n = ada_w.shape
    tn = 1536
    return pl.pallas_call(
        _ada_kernel,
        grid=(depth, n // tn),
        in_specs=[
            pl.BlockSpec(c.shape, lambda l, j: (0, 0)),
            pl.BlockSpec((1, d), lambda l, j: (0, 0)),
            pl.BlockSpec((1, d, tn), lambda l, j: (l, 0, j)),
            pl.BlockSpec((1, 1, tn), lambda l, j: (l, 0, j)),
        ],
        out_specs=pl.BlockSpec((1, SUBLANES, tn), lambda l, j: (l, 0, j)),
        out_shape=jax.ShapeDtypeStruct((depth, SUBLANES, n), F32),
        compiler_params=_params(("parallel", "parallel")),
        name="ada_ln",
    )(c, c_ctx.reshape(1, d), ada_w, ada_b.reshape(depth, 1, n))


def _qk_norm_rope(y_ref, ss_ref, off, ss_off, g, cos, sin_signed, even_lane):
    outs = []
    for j in range(QK_WIDTH // LANES):
        cols = slice(LANES * j, LANES * (j + 1))
        ss = ss_ref[:, ss_off + LANES * j:ss_off + LANES * (j + 1)]
        xc = (y_ref[:, off + LANES * j:off + LANES * (j + 1)]
              * lax.rsqrt(ss * (1.0 / HEAD_DIM) + EPS) * g[:, cols])
        partner = jnp.where(even_lane, pltpu.roll(xc, LANES - 1, 1), pltpu.roll(xc, 1, 1))
        outs.append(xc * cos + partner * sin_signed)
    return jnp.concatenate(outs, axis=1)


def _inproj_kernel(x_ref, mod_ref, g_ref, w_hbm, bd_ref, qg_ref, kg_ref, cos_ref, sin_ref,
                   *refs, with_q, mod_row):
    *out_refs, y_ref, ss_ref, w_ref, w_stage, w_sem = refs

    @pl.when(_first_step())
    def _():
        _stage_weight(w_hbm, 0, w_ref, w_stage, w_sem, col0=0 if with_q else POOL_WIDTH + QK_WIDTH)

    d = x_ref.shape[-1]
    rows = y_ref.shape[1]
    mod = _mod_row(mod_ref, mod_row)
    gain = g_ref[...] * (1.0 + mod[:, d:2 * d])
    shift = mod[:, 0:d]
    lane = lax.broadcasted_iota(jnp.int32, (rows, LANES), 1)
    even_lane = (lane % 2) == 0
    qk_off = POOL_WIDTH if with_q else 0
    qk_cols = ss_ref.shape[2]

    def project(j):
        x = x_ref[0, j * rows:(j + 1) * rows, :]
        ms = jnp.mean(x * x, axis=-1, keepdims=True)
        h = (x * lax.rsqrt(ms + EPS) * gain + shift).astype(BF16)
        y_ref[j % 2] = jnp.dot(h, w_ref[...], preferred_element_type=F32)
        for p in range(0, qk_cols, 256):
            t = y_ref[j % 2, :, qk_off + p:qk_off + p + 256]
            ss_ref[j % 2, :, p:p + 256] = jnp.dot((t * t).astype(BF16), bd_ref[...],
                                                  preferred_element_type=F32)

    def epilogue(j):
        blk = slice(j * rows, (j + 1) * rows)
        y = y_ref.at[j % 2]
        ss = ss_ref.at[j % 2]
        cos = cos_ref[blk, :]
        sin_signed = sin_ref[blk, :]
        if with_q:
            pool_ref, q_ref, k_ref, vt_ref = out_refs
            pool_ref[0, blk, :] = y[:, :POOL_WIDTH]
            q_ref[0, blk, :] = _qk_norm_rope(y, ss, qk_off, 0, qg_ref[...], cos, sin_signed,
                                             even_lane).astype(BF16)
            k_at = QK_WIDTH
        else:
            k_ref, vt_ref = out_refs
            k_at = 0
        k_ref[0, blk, :] = _qk_norm_rope(y, ss, qk_off + k_at, k_at, kg_ref[...], cos, sin_signed,
                                         even_lane).astype(BF16)
        vt_ref[0, 0, :, blk] = y[:, qk_off + k_at + QK_WIDTH:].T.astype(BF16)

    _software_pipeline(x_ref.shape[1] // rows, project, epilogue)


def _inproj_call(x, mods, layer, mod_row, g, w, bd, qg, kg, cos, sin_signed, with_q):
    b, s, d = x.shape
    tm = min(TM_PROJ, s)
    nt = s // tm
    dv = N_HEADS * V_DIM
    n = w.shape[2] if with_q else QK_WIDTH + dv
    out_shape = [jax.ShapeDtypeStruct((b, s, QK_WIDTH), BF16),
                 jax.ShapeDtypeStruct((b, nt, dv, tm), BF16)]
    out_specs = [pl.BlockSpec((1, tm, QK_WIDTH), lambda bi, i: (bi, i, 0)),
                 pl.BlockSpec((1, 1, dv, tm), lambda bi, i: (bi, i, 0, 0))]
    if with_q:
        out_shape = [jax.ShapeDtypeStruct((b, s, POOL_WIDTH), F32),
                     jax.ShapeDtypeStruct((b, s, QK_WIDTH), BF16)] + out_shape
        out_specs = [pl.BlockSpec((1, tm, POOL_WIDTH), lambda bi, i: (bi, i, 0)),
                     pl.BlockSpec((1, tm, QK_WIDTH), lambda bi, i: (bi, i, 0))] + out_specs
    return pl.pallas_call(
        functools.partial(_inproj_kernel, with_q=with_q, mod_row=mod_row),
        grid=(b, nt),
        in_specs=[
            pl.BlockSpec((1, tm, d), lambda bi, i: (bi, i, 0)),
            _mod_spec(mods, layer),
            _const_spec((1, d)),
            _HBM,
            _const_spec(bd.shape),
            _const_spec(qg.shape),
            _const_spec(kg.shape),
            pl.BlockSpec((tm, LANES), lambda bi, i: (i, 0)),
            pl.BlockSpec((tm, LANES), lambda bi, i: (i, 0)),
        ],
        out_specs=out_specs,
        out_shape=out_shape,
        scratch_shapes=[pltpu.VMEM((2, min(ROWS_PROJ, tm), n), F32),
                        pltpu.VMEM((2, min(ROWS_PROJ, tm), n - dv - (POOL_WIDTH if with_q else 0)), F32),
                        *_stage_scratch(d, n)],
        compiler_params=_params(("arbitrary", "arbitrary")),
        name="inproj_q" if with_q else "inproj_ctx",
    )(x, mods, g, w, bd, qg, kg, cos, sin_signed)


def _attn_kernel(bound_ref, lam_ref, subg_ref, q_ref, k_ref, vt_ref, kc_ref, vct_ref, o_ref,
                 st_ref, l_ref, acc_ref, *, lam_init):
    tq = q_ref.shape[1] // TILES_PER_STEP
    n_chunks = vt_ref.shape[1]
    tk = vt_ref.shape[-1]

    @pl.when(pl.program_id(0) == 0)
    def _():
        l_ref[...] = jnp.ones(l_ref.shape, F32)
        acc_ref[...] = jnp.zeros(acc_ref.shape, F32)

    bounded = bound_ref[0] <= MAX_UNSHIFTED_SCORE

    def stacked_queries(tile):
        q = q_ref[0, tile * tq:(tile + 1) * tq, :]
        lane = lax.broadcasted_iota(jnp.int32, q.shape, 1)
        zero = jnp.zeros_like(q)
        return jnp.concatenate([jnp.where(lane < HEAD_DIM, q, zero),
                                jnp.where(lane >= HEAD_DIM, q, zero)], axis=0)

    def chunks():
        for c in range(n_chunks):
            for j in range(0, tk, ATTN_KEYS):
                yield k_ref[0, c * tk + j:c * tk + j + ATTN_KEYS, :], vt_ref[0, c, :, j:j + ATTN_KEYS]
        for j in range(0, kc_ref.shape[1], ATTN_KEYS):
            yield kc_ref[0, j:j + ATTN_KEYS, :], vct_ref[0, 0, :, j:j + ATTN_KEYS]

    def scores(kb, qq):
        return lax.dot_general(kb, qq, (((1,), (1,)), ((), ())), preferred_element_type=F32)

    def write_previous_step():
        lam_rows = lam_ref[...]
        a1 = jnp.sum(lam_rows[0:1] * lam_rows[1:2], axis=1, keepdims=True)
        a2 = jnp.sum(lam_rows[2:3] * lam_rows[3:4], axis=1, keepdims=True)
        lam = jnp.exp(a1) - jnp.exp(a2) + lam_init
        for tile in range(TILES_PER_STEP):
            l = l_ref[tile]
            acc = acc_ref[tile]
            ot = acc[:, :tq] / l[:, :tq] - lam * (acc[:, tq:] / l[:, tq:])
            o = ot.T
            ms = jnp.mean(o * o, axis=-1, keepdims=True)
            o_ref[0, tile * tq:(tile + 1) * tq, :] = (
                o * lax.rsqrt(ms + EPS) * (subg_ref[...] * (1.0 - lam_init))).astype(BF16)

    @pl.when(bounded)
    def _():
        work = list(chunks())
        write_previous_step()
        for tile in range(TILES_PER_STEP):
            l = jnp.zeros((1, 2 * tq), F32)
            acc = jnp.zeros((V_DIM, 2 * tq), F32)
            qq = stacked_queries(tile)
            slot = 2 * tile
            st_ref[slot, :work[0][0].shape[0], :] = scores(work[0][0], qq)
            for c, (kb, vtb) in enumerate(work):
                if c + 1 < len(work):
                    nxt = work[c + 1][0]
                    st_ref[slot + (c + 1) % 2, :nxt.shape[0], :] = scores(nxt, qq)
                p = jnp.exp2(st_ref[slot + c % 2, :kb.shape[0], :])
                l = l + jnp.sum(p, axis=0, keepdims=True)
                acc = acc + jnp.dot(vtb, p.astype(BF16), preferred_element_type=F32)
            l_ref[tile] = l
            acc_ref[tile] = acc

    @pl.when(jnp.logical_not(bounded))
    def _():
        write_previous_step()
        for tile in range(TILES_PER_STEP):
            m = jnp.full((1, 2 * tq), -jnp.inf, F32)
            l = jnp.zeros((1, 2 * tq), F32)
            acc = jnp.zeros((V_DIM, 2 * tq), F32)
            qq = stacked_queries(tile)
            for kb, vtb in chunks():
                st = scores(kb, qq)
                m_new = jnp.maximum(m, jnp.max(st, axis=0, keepdims=True))
                alpha = jnp.exp2(m - m_new)
                p = jnp.exp2(st - m_new)
                l = alpha * l + jnp.sum(p, axis=0, keepdims=True)
                acc = alpha * acc + jnp.dot(vtb, p.astype(BF16), preferred_element_type=F32)
                m = m_new
            l_ref[tile] = l
            acc_ref[tile] = acc


def _score_bound(q_gain, k_gain):
    slack = 1.0 + 2.0 ** -5
    return (HEAD_DIM * slack * jnp.max(jnp.abs(q_gain)) * jnp.max(jnp.abs(k_gain))).reshape(1)


def _attn_call(score_bound, lam_rows, sub_g, q, k, vt, kc, vct, lam_init):
    b, s, _ = q.shape
    tq = min(TQ, s // TILES_PER_STEP)
    tqs = TILES_PER_STEP * tq
    n_chunks, tk = vt.shape[1], vt.shape[3]
    ctx_len = kc.shape[1]
    nq = s // tqs
    n_tiles = b * N_HEADS * nq

    def tile(t):
        t = jnp.minimum(t, n_tiles - 1)
        return t // (N_HEADS * nq), (t // nq) % N_HEADS, t % nq

    def q_map(t):
        bi, h, i = tile(t)
        return bi, i, h

    def out_map(t):
        bi, h, i = tile(jnp.maximum(t - 1, 0))
        return bi, i, h

    def kv_map(t):
        bi, h, _ = tile(t)
        return bi, 0, h

    def vt_map(t):
        bi, h, _ = tile(t)
        return bi, 0, h, 0

    return pl.pallas_call(
        functools.partial(_attn_kernel, lam_init=lam_init),
        grid=(n_tiles + 1,),
        in_specs=[
            pl.BlockSpec(memory_space=pltpu.SMEM),
            _const_spec(lam_rows.shape),
            _const_spec(sub_g.shape),
            pl.BlockSpec((1, tqs, V_DIM), q_map),
            pl.BlockSpec((1, s, V_DIM), kv_map),
            pl.BlockSpec((1, n_chunks, V_DIM, tk), vt_map),
            pl.BlockSpec((1, ctx_len, V_DIM), kv_map),
            pl.BlockSpec((1, 1, V_DIM, ctx_len), vt_map),
        ],
        out_specs=pl.BlockSpec((1, tqs, V_DIM), out_map),
        out_shape=jax.ShapeDtypeStruct((b, s, N_HEADS * V_DIM), BF16),
        scratch_shapes=[pltpu.VMEM((2 * TILES_PER_STEP, ATTN_KEYS, 2 * tq), F32),
                        pltpu.VMEM((TILES_PER_STEP, 1, 2 * tq), F32),
                        pltpu.VMEM((TILES_PER_STEP, V_DIM, 2 * tq), F32)],
        compiler_params=_params(("arbitrary",)),
        name="diff_attn",
    )(score_bound, lam_rows, sub_g, q, k, vt, kc, vct)


def _fill_pool_halo(pm_ref, pp_ref, pn_ref, ext_ref):
    tm = pm_ref.shape[1]
    i = pl.program_id(1)
    last = pl.num_programs(1) - 1
    ext_ref[0:POOL_HALO, :] = jnp.where(i > 0, pp_ref[0], 0.0)
    ext_ref[POOL_HALO:POOL_HALO + tm, :] = pm_ref[0]
    ext_ref[POOL_HALO + tm:2 * POOL_HALO + tm, :] = jnp.where(i < last, pn_ref[0], 0.0)


def _pool_merge_rows(blk, x_ref, mod, attn_ref, pw_ref, ps_ref, wo_ref, ext_ref, seq):
    d = x_ref.shape[-1]
    tm = x_ref.shape[1]
    rows = blk.stop - blk.start

    def window_sum(lo, hi, half):
        acc = None
        for off in range(lo, hi + 1):
            piece = ext_ref[pl.ds(POOL_HALO + blk.start + off, rows), LANES * half:LANES * (half + 1)]
            acc = piece if acc is None else acc + piece
        return acc

    t = pl.program_id(1) * tm + blk.start + lax.broadcasted_iota(jnp.int32, (rows, LANES), 0)
    lane = lax.broadcasted_iota(jnp.int32, (rows, LANES), 1)
    low_group = lane < POOL_GROUP
    pooled = []
    for half in range(2):
        w_small, w_big = POOL_WINDOWS[2 * half], POOL_WINDOWS[2 * half + 1]
        s_small = window_sum(-(w_small // 2), w_small // 2 - 1, half)
        s_big = (s_small + window_sum(-(w_big // 2), -(w_small // 2) - 1, half)
                 + window_sum(w_small // 2, w_big // 2 - 1, half))
        reach = jnp.where(low_group, w_small // 2, w_big // 2)
        cnt = jnp.minimum(t + reach, seq) - jnp.maximum(t - reach, 0)
        mean = jnp.where(low_group, s_small, s_big) / cnt.astype(F32)
        pooled.append(mean - ext_ref[pl.ds(POOL_HALO + blk.start, rows), LANES * half:LANES * (half + 1)])
    pooled = jnp.concatenate(pooled, axis=1).astype(BF16)
    pool_y = (jnp.dot(pooled, pw_ref[...], preferred_element_type=F32) * ps_ref[...]).astype(BF16)
    y = (jnp.dot(pool_y, wo_ref[0:POOL_WIDTH, :], preferred_element_type=F32)
         + jnp.dot(attn_ref[0, blk, :], wo_ref[POOL_WIDTH:, :], preferred_element_type=F32))
    return x_ref[0, blk, :] + mod[:, 2 * d:3 * d] * y


def _ffn_rows(x, mod, g, w1_ref, w2_ref):
    d = x.shape[-1]
    h = _norm_mod(x, g, mod[:, 3 * d:4 * d], mod[:, 4 * d:5 * d]).astype(BF16)
    a = jnp.maximum(jnp.dot(h, w1_ref[...], preferred_element_type=F32), 0.0)
    a = (a * a).astype(BF16)
    y = jnp.dot(a, w2_ref[...], preferred_element_type=F32)
    return x + mod[:, 5 * d:6 * d] * y


def _ffn_blocks(x_block, o_ref, mod, g, w1_ref, w2_ref):
    tm = o_ref.shape[1]
    rows = min(ROWS_FFN, tm)
    for r in range(tm // rows):
        blk = slice(r * rows, (r + 1) * rows)
        o_ref[0, blk, :] = _ffn_rows(x_block(blk), mod, g, w1_ref, w2_ref)


def _ffn_kernel(x_ref, mod_ref, g_ref, w1_hbm, w2_hbm, o_ref,
                w1_ref, w1_stage, w1_sem, w2_ref, w2_stage, w2_sem, *, layer):
    @pl.when(_first_step())
    def _():
        _stage_weight(w1_hbm, layer, w1_ref, w1_stage, w1_sem)
        _stage_weight(w2_hbm, layer, w2_ref, w2_stage, w2_sem)

    _ffn_blocks(lambda blk: x_ref[0, blk, :], o_ref, _mod_row(mod_ref), g_ref[...], w1_ref, w2_ref)


def _merge_ffn_kernel(x_ref, mod_ref, pm_ref, pp_ref, pn_ref, attn_ref, pw_ref, ps_ref, wo_hbm,
                      g_ref, w1_hbm, w2_hbm, o_ref, ext_ref, x1_ref,
                      wo_ref, wo_stage, wo_sem, w1_ref, w1_stage, w1_sem, w2_ref, w2_stage, w2_sem, *,
                      seq, layer):
    @pl.when(_first_step())
    def _():
        _stage_weight(wo_hbm, layer // 2, wo_ref, wo_stage, wo_sem)
        _stage_weight(w1_hbm, layer, w1_ref, w1_stage, w1_sem)
        _stage_weight(w2_hbm, layer, w2_ref, w2_stage, w2_sem)

    mod = _mod_row(mod_ref)
    rows = x1_ref.shape[1]
    _fill_pool_halo(pm_ref, pp_ref, pn_ref, ext_ref)

    def merge(j):
        blk = slice(j * rows, (j + 1) * rows)
        x1_ref[j % 2] = _pool_merge_rows(blk, x_ref, mod, attn_ref, pw_ref, ps_ref, wo_ref, ext_ref, seq)

    def ffn(j):
        o_ref[0, j * rows:(j + 1) * rows, :] = _ffn_rows(x1_ref[j % 2], mod, g_ref[...], w1_ref, w2_ref)

    _software_pipeline(x_ref.shape[1] // rows, merge, ffn)


def _merge_ffn_call(x, mods, pool_in, attn, pool_w_bd, pool_scale, w_out, g, w1, w2, layer):
    b, s, d = x.shape
    tm = min(TM_FFN, s)
    hb = tm // POOL_HALO
    n_halo = s // POOL_HALO
    return pl.pallas_call(
        functools.partial(_merge_ffn_kernel, seq=s, layer=layer),
        grid=(b, s // tm),
        in_specs=[
            pl.BlockSpec((1, tm, d), lambda bi, i: (bi, i, 0)),
            _mod_spec(mods, layer),
            pl.BlockSpec((1, tm, POOL_WIDTH), lambda bi, i: (bi, i, 0)),
            pl.BlockSpec((1, POOL_HALO, POOL_WIDTH),
                         lambda bi, i: (bi, jnp.maximum(i * hb - 1, 0), 0)),
            pl.BlockSpec((1, POOL_HALO, POOL_WIDTH),
                         lambda bi, i: (bi, jnp.minimum((i + 1) * hb, n_halo - 1), 0)),
            pl.BlockSpec((1, tm, attn.shape[-1]), lambda bi, i: (bi, i, 0)),
            _const_spec(pool_w_bd.shape),
            _const_spec(pool_scale.shape),
            _HBM,
            _const_spec((1, d)),
            _HBM,
            _HBM,
        ],
        out_specs=pl.BlockSpec((1, tm, d), lambda bi, i: (bi, i, 0)),
        out_shape=jax.ShapeDtypeStruct((b, s, d), F32),
        scratch_shapes=[pltpu.VMEM((tm + 2 * POOL_HALO, POOL_WIDTH), F32),
                        pltpu.VMEM((2, min(ROWS_FFN, tm), d), F32), *_stage_scratch(*w_out.shape[1:]),
                        *_stage_scratch(*w1.shape[1:]), *_stage_scratch(*w2.shape[1:])],
        compiler_params=_params(("arbitrary", "arbitrary")),
        name="pool_merge_ffn",
    )(x, mods, pool_in, pool_in, pool_in, attn, pool_w_bd, pool_scale, w_out, g, w1, w2)


def _ffn_call(x, mods, g, w1, w2, layer):
    b, s, d = x.shape
    tm = min(TM_FFN, s)
    return pl.pallas_call(
        functools.partial(_ffn_kernel, layer=layer),
        grid=(b, s // tm),
        in_specs=[
            pl.BlockSpec((1, tm, d), lambda bi, i: (bi, i, 0)),
            _mod_spec(mods, layer),
            _const_spec((1, d)),
            _HBM,
            _HBM,
        ],
        out_specs=pl.BlockSpec((1, tm, d), lambda bi, i: (bi, i, 0)),
        out_shape=jax.ShapeDtypeStruct((b, s, d), F32),
        scratch_shapes=[*_stage_scratch(*w1.shape[1:]), *_stage_scratch(*w2.shape[1:])],
        compiler_params=_params(("arbitrary", "arbitrary")),
        name="ffn",
    )(x, mods, g, w1, w2)


def _sgmlp_kernel(x_ref, mod_ref, g_ref, win_hbm, lng_ref, lnb_ref, sgw_ref, sgb_ref, wout_hbm,
                  o_ref, a_ref, win_ref, win_stage, win_sem, wout_ref, wout_stage, wout_sem):
    @pl.when(_first_step())
    def _():
        _stage_weight(win_hbm, 0, win_ref, win_stage, win_sem)
        _stage_weight(wout_hbm, 0, wout_ref, wout_stage, wout_sem)

    d = x_ref.shape[-1]
    rows = a_ref.shape[1]
    width = win_ref.shape[1] // 2
    mod = _mod_row(mod_ref)
    gain = g_ref[...] * (1.0 + mod[:, d:2 * d])
    shift = mod[:, 0:d]
    gate = mod[:, 2 * d:3 * d]

    def project(j):
        x = x_ref[0, j * rows:(j + 1) * rows, :]
        ms = jnp.mean(x * x, axis=-1, keepdims=True)
        h = (x * lax.rsqrt(ms + EPS) * gain + shift).astype(BF16)
        a_ref[j % 2] = jnp.dot(h, win_ref[...], preferred_element_type=F32)

    def gelu(t):
        return 0.5 * t * (1.0 + lax.erf(t * (1.0 / math.sqrt(2.0))))

    def epilogue(j):
        blk = slice(j * rows, (j + 1) * rows)
        a = a_ref.at[j % 2]
        v = gelu(a[:, width:])
        mu = jnp.mean(v, axis=-1, keepdims=True)
        vc = v - mu
        var = jnp.mean(vc * vc, axis=-1, keepdims=True)
        vn = (vc * lax.rsqrt(var + EPS) * lng_ref[...] + lnb_ref[...]).astype(BF16)
        mixed = []
        for r in range(rows // CHUNK):
            cols = [jnp.dot(sgw_ref[gi],
                            vn[r * CHUNK:(r + 1) * CHUNK, gi * SG_GROUP_DIM:(gi + 1) * SG_GROUP_DIM],
                            preferred_element_type=F32) for gi in range(SG_GROUPS)]
            mixed.append(jnp.concatenate(cols, axis=1) + sgb_ref[...])
        z = (gelu(a[:, :width]) * jnp.concatenate(mixed, axis=0)).astype(BF16)
        y = jnp.dot(z, wout_ref[...], preferred_element_type=F32)
        o_ref[0, blk, :] = x_ref[0, blk, :] + gate * y

    _software_pipeline(x_ref.shape[1] // rows, project, epilogue)


def _sgmlp_call(x, mods, layer, g, w_in, ln_g, ln_b, sg_w, sg_b_full, w_out):
    b, s, d = x.shape
    tm = min(TM_SG, s)
    return pl.pallas_call(
        _sgmlp_kernel,
        grid=(b, s // tm),
        in_specs=[
            pl.BlockSpec((1, tm, d), lambda bi, i: (bi, i, 0)),
            _mod_spec(mods, layer),
            _const_spec((1, d)),
            _HBM,
            _const_spec(ln_g.shape),
            _const_spec(ln_b.shape),
            _const_spec(sg_w.shape),
            _const_spec(sg_b_full.shape),
            _HBM,
        ],
        out_specs=pl.BlockSpec((1, tm, d), lambda bi, i: (bi, i, 0)),
        out_shape=jax.ShapeDtypeStruct((b, s, d), F32),
        scratch_shapes=[pltpu.VMEM((2, min(ROWS_SG, tm), w_in.shape[2]), F32),
                        *_stage_scratch(*w_in.shape[1:]), *_stage_scratch(*w_out.shape[1:])],
        compiler_params=_params(("arbitrary", "arbitrary")),
        name="sg_mlp",
    )(x, mods, g, w_in, ln_g, ln_b, sg_w, sg_b_full, w_out)


def _rope_tables(n_tok):
    t = np.arange(n_tok, dtype=np.int32)
    rows = (t // GRID_W).astype(np.float32)
    cols = (t % GRID_W).astype(np.float32)
    pairs = HEAD_DIM // 4
    inv = (np.float32(ROPE_THETA) ** (-np.arange(pairs, dtype=np.float32) / np.float32(pairs))).astype(np.float32)
    ang = np.concatenate([rows[:, None] * inv, cols[:, None] * inv], axis=-1)
    cos, sin = np.cos(ang), np.sin(ang)
    cos2 = np.repeat(cos, 2, axis=-1)
    sin2 = np.stack([-sin, sin], axis=-1).reshape(n_tok, HEAD_DIM)
    return (jnp.asarray(np.tile(cos2, (1, LANES // HEAD_DIM)), F32),
            jnp.asarray(np.tile(sin2, (1, LANES // HEAD_DIM)), F32))


def _block_diag(blocks):
    n = len(blocks)
    rows = []
    for i, blk in enumerate(blocks):
        z = jnp.zeros_like(blk)
        rows.append(jnp.concatenate([blk if j == i else z for j in range(n)], axis=1))
    return jnp.concatenate(rows, axis=0)


def kernel(x, c, ctx, c_ctx, ada_w, ada_b, norm_mix_g, norm_ffn_g, ffn_w1, ffn_w2, ev_w_in, ev_w_out,
           pool_w, pool_scale, q_norm_g, k_norm_g, lam_q1, lam_k1, lam_q2, lam_k2, sub_norm_g,
           od_w_in, sg_ln_g, sg_ln_b, sg_w, sg_b, od_w_out):
    b, s, d = x.shape
    depth = ada_w.shape[0]
    assert depth == 2 and b <= SUBLANES - 1

    mods = _ada_call(c, c_ctx, ada_w, ada_b)

    n_groups = QK_WIDTH // HEAD_DIM
    qg = jnp.tile(q_norm_g[0] * (HEAD_DIM ** -0.5 * math.log2(math.e)), n_groups)[None, :]
    kg = jnp.tile(k_norm_g[0], n_groups)[None, :]
    ones = jnp.ones((HEAD_DIM, HEAD_DIM), BF16)
    bd = _block_diag([ones] * (256 // HEAD_DIM))
    cos, sin_signed = _rope_tables(s)
    ctx_len = ctx.shape[1]
    cos_ctx = jnp.ones((ctx_len, LANES), F32)
    sin_ctx = jnp.zeros((ctx_len, LANES), F32)
    g_mix0 = norm_mix_g[0][None, :]

    pool_in, q, k, vt = _inproj_call(x, mods, 0, None, g_mix0, ev_w_in, bd, qg, kg, cos, sin_signed, True)
    kc, vct = _inproj_call(ctx, mods, 0, b, g_mix0, ev_w_in, bd, qg, kg, cos_ctx, sin_ctx, False)

    lam_init = 0.8 - 0.6 * math.exp(-0.3 * 0)
    lam_rows = jnp.pad(jnp.stack([lam_q1[0], lam_k1[0], lam_q2[0], lam_k2[0]]),
                       ((0, SUBLANES - 4), (0, LANES - HEAD_DIM)))
    attn = _attn_call(_score_bound(qg, kg), lam_rows, sub_norm_g[0][None, :], q, k, vt, kc, vct, lam_init)

    pool_w_bd = _block_diag([pool_w[0, gi] for gi in range(len(POOL_WINDOWS))]).astype(BF16)
    x = _merge_ffn_call(x, mods, pool_in, attn, pool_w_bd, pool_scale[0][None, :],
                        ev_w_out, norm_ffn_g[0][None, :], ffn_w1, ffn_w2, 0)

    sg_b_full = jnp.repeat(sg_b[0].T, SG_GROUP_DIM, axis=1)
    x = _sgmlp_call(x, mods, 1, norm_mix_g[1][None, :], od_w_in,
                    sg_ln_g[0][None, :], sg_ln_b[0][None, :], sg_w[0].astype(BF16), sg_b_full, od_w_out)
    x = _ffn_call(x, mods, norm_ffn_g[1][None, :], ffn_w1, ffn_w2, 1)
    return x
```

```python
import functools
import math

import jax
import jax.numpy as jnp
import numpy as np
from jax import lax
from jax.experimental import pallas as pl
from jax.experimental.pallas import tpu as pltpu

F32 = jnp.float32
BF16 = jnp.bfloat16

EPS = 1e-6
GRID_W = 64
ROPE_THETA = 10000.0
POOL_WINDOWS = (2, 4, 8, 16)
POOL_GROUP = 64
POOL_WIDTH = 256
HEAD_DIM = 64
V_DIM = 128
N_HEADS = 6
QK_WIDTH = N_HEADS * 2 * HEAD_DIM
Q_GAIN_SCALE = HEAD_DIM ** -0.5 * math.log2(math.e)
CHUNK = 128
SG_GROUPS = 8
SG_GROUP_DIM = 256

LANES = 128
SUBLANES = 8
MXU_TILE = 256
POOL_HALO = 8
VMEM_LIMIT = 56 * 1024 * 1024
STAGE_BYTES = 2 * 1024 * 1024

MAX_UNSHIFTED_SCORE = 60.0

ADA_COLS = 1536
TM_PROJ = 1024
ROWS_PROJ = 256
TQ = 512
TILES_PER_STEP = 2
ATTN_KEYS = 1024
TM_FFN = 512
TM_FFN_ALONE = 1024
ROWS_FFN = 256
TM_SG = 512
ROWS_SG = 256


def _params(sem):
    return pltpu.CompilerParams(dimension_semantics=sem, vmem_limit_bytes=VMEM_LIMIT)


def _const_spec(shape):
    zeros = (0,) * len(shape)
    return pl.BlockSpec(shape, lambda *_: zeros, pipeline_mode=pl.Buffered(1))


def _norm_mod(x, g, shift, scale):
    ms = jnp.mean(x * x, axis=-1, keepdims=True)
    return x * lax.rsqrt(ms + EPS) * (g * (1.0 + scale)) + shift


def _mod_row(mod_ref, row=None):
    if row is None:
        row = pl.program_id(0)
    return mod_ref[0, pl.ds(row, 1), :]


def _mod_spec(mods, layer):
    return pl.BlockSpec((1,) + mods.shape[1:], lambda *_: (layer, 0, 0))


def _software_pipeline(n, produce, consume):
    produce(0)
    for j in range(n):
        if j + 1 < n:
            produce(j + 1)
        consume(j)


def _first_step():
    return jnp.logical_and(pl.program_id(0) == 0, pl.program_id(1) == 0)


def _stage_weight(w_hbm, layer, dst_ref, stage_ref, sem, col0=0):
    k, n = dst_ref.shape
    slab = stage_ref.shape[1]
    n_slabs = k // slab

    def copy(i):
        return pltpu.make_async_copy(w_hbm.at[layer, pl.ds(i * slab, slab), pl.ds(col0, n)],
                                     stage_ref.at[i % 2], sem.at[i % 2])

    copy(0).start()
    for i in range(n_slabs):
        if i + 1 < n_slabs:
            copy(i + 1).start()
        copy(i).wait()
        dst_ref[i * slab:(i + 1) * slab, :] = stage_ref[i % 2].astype(BF16)


def _stage_scratch(k, n):
    slab = max(r for r in range(SUBLANES, k + 1, SUBLANES) if k % r == 0 and (r == SUBLANES or 4 * r * n <= STAGE_BYTES))
    return [pltpu.VMEM((k, n), BF16), pltpu.VMEM((2, slab, n), F32), pltpu.SemaphoreType.DMA((2,))]


_HBM = pl.BlockSpec(memory_space=pl.ANY)


def _ada_kernel(c_ref, cctx_ref, w_ref, b_ref, o_ref):
    pad = jnp.zeros((SUBLANES - c_ref.shape[0] - 1, c_ref.shape[1]), F32)
    c = jnp.concatenate([c_ref[...], cctx_ref[...], pad], axis=0)
    sc = (c * jax.nn.sigmoid(c)).astype(BF16)
    o_ref[0] = jnp.dot(sc, w_ref[0].astype(BF16), preferred_element_type=F32) + b_ref[0]


def _ada_call(c, c_ctx, ada_w, ada_b):
    depth, d, n = ada_w.shape
    tn = ADA_COLS
    return pl.pallas_call(
        _ada_kernel,
        grid=(depth, n // tn),
        in_specs=[
            pl.BlockSpec(c.shape, lambda l, j: (0, 0)),
            pl.BlockSpec((1, d), lambda l, j: (0, 0)),
            pl.BlockSpec((1, d, tn), lambda l, j: (l, 0, j)),
            pl.BlockSpec((1, 1, tn), lambda l, j: (l, 0, j)),
        ],
        out_specs=pl.BlockSpec((1, SUBLANES, tn), lambda l, j: (l, 0, j)),
        out_shape=jax.ShapeDtypeStruct((depth, SUBLANES, n), F32),
        compiler_params=_params(("parallel", "parallel")),
        name="ada_ln",
    )(c, c_ctx.reshape(1, d), ada_w, ada_b.reshape(depth, 1, n))


def _qk_norm_rope(y_ref, ms_ref, off, ms_off, g, cos, sin_signed, even_lane):
    outs = []
    for j in range(QK_WIDTH // LANES):
        ms = ms_ref[:, ms_off + LANES * j:ms_off + LANES * (j + 1)]
        xc = y_ref[:, off + LANES * j:off + LANES * (j + 1)] * lax.rsqrt(ms + EPS) * g
        partner = jnp.where(even_lane, pltpu.roll(xc, LANES - 1, 1), pltpu.roll(xc, 1, 1))
        outs.append(xc * cos + partner * sin_signed)
    return jnp.concatenate(outs, axis=1)


def _inproj_kernel(x_ref, mod_ref, g_ref, w_hbm, bd_ref, qg_ref, kg_ref, cos_ref, sin_ref,
                   *refs, with_q, mod_row):
    *out_refs, y_ref, ss_ref, w_ref, w_stage, w_sem = refs

    @pl.when(_first_step())
    def _():
        _stage_weight(w_hbm, 0, w_ref, w_stage, w_sem, col0=0 if with_q else POOL_WIDTH + QK_WIDTH)

    d = x_ref.shape[-1]
    rows = y_ref.shape[1]
    mod = _mod_row(mod_ref, mod_row)
    gain = g_ref[...] * (1.0 + mod[:, d:2 * d])
    shift = mod[:, 0:d]
    lane = lax.broadcasted_iota(jnp.int32, (rows, LANES), 1)
    even_lane = (lane % 2) == 0
    q_gain = jnp.concatenate([qg_ref[...]] * (LANES // HEAD_DIM), axis=1) * Q_GAIN_SCALE
    k_gain = jnp.concatenate([kg_ref[...]] * (LANES // HEAD_DIM), axis=1)
    qk_off = POOL_WIDTH if with_q else 0
    qk_cols = ss_ref.shape[2]

    def project(j):
        x = x_ref[0, j * rows:(j + 1) * rows, :]
        ms = jnp.mean(x * x, axis=-1, keepdims=True)
        h = (x * lax.rsqrt(ms + EPS) * gain + shift).astype(BF16)
        y_ref[j % 2] = jnp.dot(h, w_ref[...], preferred_element_type=F32)
        for p in range(0, qk_cols, MXU_TILE):
            t = y_ref[j % 2, :, qk_off + p:qk_off + p + MXU_TILE]
            ss_ref[j % 2, :, p:p + MXU_TILE] = jnp.dot((t * t).astype(BF16), bd_ref[...],
                                                       preferred_element_type=F32)

    def epilogue(j):
        blk = slice(j * rows, (j + 1) * rows)
        y = y_ref.at[j % 2]
        ss = ss_ref.at[j % 2]
        cos = cos_ref[blk, :]
        sin_signed = sin_ref[blk, :]
        if with_q:
            pool_ref, q_ref, k_ref, vt_ref = out_refs
            pool_ref[0, blk, :] = y[:, :POOL_WIDTH]
            q_ref[0, blk, :] = _qk_norm_rope(y, ss, qk_off, 0, q_gain, cos, sin_signed,
                                             even_lane).astype(BF16)
            k_at = QK_WIDTH
        else:
            k_ref, vt_ref = out_refs
            k_at = 0
        k_ref[0, blk, :] = _qk_norm_rope(y, ss, qk_off + k_at, k_at, k_gain, cos, sin_signed,
                                         even_lane).astype(BF16)
        vt_ref[0, 0, :, blk] = y[:, qk_off + k_at + QK_WIDTH:].T.astype(BF16)

    _software_pipeline(x_ref.shape[1] // rows, project, epilogue)


def _inproj_call(x, mods, layer, mod_row, g, w, bd, qg, kg, cos, sin_signed, with_q):
    b, s, d = x.shape
    tm = min(TM_PROJ, s)
    nt = s // tm
    dv = N_HEADS * V_DIM
    n = w.shape[2] if with_q else QK_WIDTH + dv
    out_shape = [jax.ShapeDtypeStruct((b, s, QK_WIDTH), BF16),
                 jax.ShapeDtypeStruct((b, nt, dv, tm), BF16)]
    out_specs = [pl.BlockSpec((1, tm, QK_WIDTH), lambda bi, i: (bi, i, 0)),
                 pl.BlockSpec((1, 1, dv, tm), lambda bi, i: (bi, i, 0, 0))]
    if with_q:
        out_shape = [jax.ShapeDtypeStruct((b, s, POOL_WIDTH), F32),
                     jax.ShapeDtypeStruct((b, s, QK_WIDTH), BF16)] + out_shape
        out_specs = [pl.BlockSpec((1, tm, POOL_WIDTH), lambda bi, i: (bi, i, 0)),
                     pl.BlockSpec((1, tm, QK_WIDTH), lambda bi, i: (bi, i, 0))] + out_specs
    return pl.pallas_call(
        functools.partial(_inproj_kernel, with_q=with_q, mod_row=mod_row),
        grid=(b, nt),
        in_specs=[
            pl.BlockSpec((1, tm, d), lambda bi, i: (bi, i, 0)),
            _mod_spec(mods, layer),
            _const_spec((1, d)),
            _HBM,
            _const_spec(bd.shape),
            _const_spec(qg.shape),
            _const_spec(kg.shape),
            pl.BlockSpec((tm, LANES), lambda bi, i: (i, 0)),
            pl.BlockSpec((tm, LANES), lambda bi, i: (i, 0)),
        ],
        out_specs=out_specs,
        out_shape=out_shape,
        scratch_shapes=[pltpu.VMEM((2, min(ROWS_PROJ, tm), n), F32),
                        pltpu.VMEM((2, min(ROWS_PROJ, tm), n - dv - (POOL_WIDTH if with_q else 0)), F32),
                        *_stage_scratch(d, n)],
        compiler_params=_params(("arbitrary", "arbitrary")),
        name="inproj_q" if with_q else "inproj_ctx",
    )(x, mods, g, w, bd, qg, kg, cos, sin_signed)


def _attn_kernel(bound_ref, lam_ref, subg_ref, q_ref, k_ref, vt_ref, kc_ref, vct_ref, o_ref,
                 st_ref, l_ref, acc_ref, *, lam_init):
    tq = q_ref.shape[1] // TILES_PER_STEP
    n_chunks = vt_ref.shape[1]
    tk = vt_ref.shape[-1]

    @pl.when(pl.program_id(0) == 0)
    def _():
        l_ref[...] = jnp.ones(l_ref.shape, F32)
        acc_ref[...] = jnp.zeros(acc_ref.shape, F32)

    bounded = bound_ref[0] <= MAX_UNSHIFTED_SCORE

    def stacked_queries(tile):
        q = q_ref[0, tile * tq:(tile + 1) * tq, :]
        lane = lax.broadcasted_iota(jnp.int32, q.shape, 1)
        zero = jnp.zeros_like(q)
        return jnp.concatenate([jnp.where(lane < HEAD_DIM, q, zero),
                                jnp.where(lane >= HEAD_DIM, q, zero)], axis=0)

    def chunks():
        for c in range(n_chunks):
            for j in range(0, tk, ATTN_KEYS):
                yield k_ref[0, c * tk + j:c * tk + j + ATTN_KEYS, :], vt_ref[0, c, :, j:j + ATTN_KEYS]
        for j in range(0, kc_ref.shape[1], ATTN_KEYS):
            yield kc_ref[0, j:j + ATTN_KEYS, :], vct_ref[0, 0, :, j:j + ATTN_KEYS]

    def scores(kb, qq):
        return lax.dot_general(kb, qq, (((1,), (1,)), ((), ())), preferred_element_type=F32)

    def write_previous_step():
        lam_rows = lam_ref[...]
        a1 = jnp.sum(lam_rows[0:1] * lam_rows[1:2], axis=1, keepdims=True)
        a2 = jnp.sum(lam_rows[2:3] * lam_rows[3:4], axis=1, keepdims=True)
        lam = jnp.exp(a1) - jnp.exp(a2) + lam_init
        for tile in range(TILES_PER_STEP):
            l = l_ref[tile]
            acc = acc_ref[tile]
            ot = acc[:, :tq] / l[:, :tq] - lam * (acc[:, tq:] / l[:, tq:])
            o = ot.T
            ms = jnp.mean(o * o, axis=-1, keepdims=True)
            o_ref[0, tile * tq:(tile + 1) * tq, :] = (
                o * lax.rsqrt(ms + EPS) * (subg_ref[...] * (1.0 - lam_init))).astype(BF16)

    @pl.when(bounded)
    def _():
        work = list(chunks())
        write_previous_step()
        for tile in range(TILES_PER_STEP):
            l = jnp.zeros((1, 2 * tq), F32)
            acc = jnp.zeros((V_DIM, 2 * tq), F32)
            qq = stacked_queries(tile)
            slot = 2 * tile
            st_ref[slot, :work[0][0].shape[0], :] = scores(work[0][0], qq)
            for c, (kb, vtb) in enumerate(work):
                if c + 1 < len(work):
                    nxt = work[c + 1][0]
                    st_ref[slot + (c + 1) % 2, :nxt.shape[0], :] = scores(nxt, qq)
                p = jnp.exp2(st_ref[slot + c % 2, :kb.shape[0], :])
                l = l + jnp.sum(p, axis=0, keepdims=True)
                acc = acc + jnp.dot(vtb, p.astype(BF16), preferred_element_type=F32)
            l_ref[tile] = l
            acc_ref[tile] = acc

    @pl.when(jnp.logical_not(bounded))
    def _():
        write_previous_step()
        for tile in range(TILES_PER_STEP):
            m = jnp.full((1, 2 * tq), -jnp.inf, F32)
            l = jnp.zeros((1, 2 * tq), F32)
            acc = jnp.zeros((V_DIM, 2 * tq), F32)
            qq = stacked_queries(tile)
            for kb, vtb in chunks():
                st = scores(kb, qq)
                m_new = jnp.maximum(m, jnp.max(st, axis=0, keepdims=True))
                alpha = jnp.exp2(m - m_new)
                p = jnp.exp2(st - m_new)
                l = alpha * l + jnp.sum(p, axis=0, keepdims=True)
                acc = alpha * acc + jnp.dot(vtb, p.astype(BF16), preferred_element_type=F32)
                m = m_new
            l_ref[tile] = l
            acc_ref[tile] = acc


def _score_bound(q_gain, k_gain):
    slack = 1.0 + 2.0 ** -5
    return (HEAD_DIM * slack * jnp.max(jnp.abs(q_gain)) * jnp.max(jnp.abs(k_gain))).reshape(1)


def _attn_call(score_bound, lam_rows, sub_g, q, k, vt, kc, vct, lam_init):
    b, s, _ = q.shape
    tq = min(TQ, s // TILES_PER_STEP)
    tqs = TILES_PER_STEP * tq
    n_chunks, tk = vt.shape[1], vt.shape[3]
    ctx_len = kc.shape[1]
    nq = s // tqs
    n_tiles = b * N_HEADS * nq

    def tile(t):
        t = jnp.minimum(t, n_tiles - 1)
        return t // (N_HEADS * nq), (t // nq) % N_HEADS, t % nq

    def q_map(t):
        bi, h, i = tile(t)
        return bi, i, h

    def out_map(t):
        bi, h, i = tile(jnp.maximum(t - 1, 0))
        return bi, i, h

    def kv_map(t):
        bi, h, _ = tile(t)
        return bi, 0, h

    def vt_map(t):
        bi, h, _ = tile(t)
        return bi, 0, h, 0

    return pl.pallas_call(
        functools.partial(_attn_kernel, lam_init=lam_init),
        grid=(n_tiles + 1,),
        in_specs=[
            pl.BlockSpec(memory_space=pltpu.SMEM),
            _const_spec(lam_rows.shape),
            _const_spec(sub_g.shape),
            pl.BlockSpec((1, tqs, V_DIM), q_map),
            pl.BlockSpec((1, s, V_DIM), kv_map),
            pl.BlockSpec((1, n_chunks, V_DIM, tk), vt_map),
            pl.BlockSpec((1, ctx_len, V_DIM), kv_map),
            pl.BlockSpec((1, 1, V_DIM, ctx_len), vt_map),
        ],
        out_specs=pl.BlockSpec((1, tqs, V_DIM), out_map),
        out_shape=jax.ShapeDtypeStruct((b, s, N_HEADS * V_DIM), BF16),
        scratch_shapes=[pltpu.VMEM((2 * TILES_PER_STEP, ATTN_KEYS, 2 * tq), F32),
                        pltpu.VMEM((TILES_PER_STEP, 1, 2 * tq), F32),
                        pltpu.VMEM((TILES_PER_STEP, V_DIM, 2 * tq), F32)],
        compiler_params=_params(("arbitrary",)),
        name="diff_attn",
    )(score_bound, lam_rows, sub_g, q, k, vt, kc, vct)


def _fill_pool_halo(pm_ref, pp_ref, pn_ref, ext_ref):
    tm = pm_ref.shape[1]
    i = pl.program_id(1)
    last = pl.num_programs(1) - 1
    ext_ref[0:POOL_HALO, :] = jnp.where(i > 0, pp_ref[0], 0.0)
    ext_ref[POOL_HALO:POOL_HALO + tm, :] = pm_ref[0]
    ext_ref[POOL_HALO + tm:2 * POOL_HALO + tm, :] = jnp.where(i < last, pn_ref[0], 0.0)


def _pool_merge_rows(blk, x_ref, mod, attn_ref, pw_ref, ps_ref, wo_ref, ext_ref, seq):
    d = x_ref.shape[-1]
    tm = x_ref.shape[1]
    rows = blk.stop - blk.start

    def window_sum(lo, hi, half):
        acc = None
        for off in range(lo, hi + 1):
            piece = ext_ref[pl.ds(POOL_HALO + blk.start + off, rows), LANES * half:LANES * (half + 1)]
            acc = piece if acc is None else acc + piece
        return acc

    t = pl.program_id(1) * tm + blk.start + lax.broadcasted_iota(jnp.int32, (rows, LANES), 0)
    lane = lax.broadcasted_iota(jnp.int32, (rows, LANES), 1)
    low_group = lane < POOL_GROUP
    pooled = []
    for half in range(2):
        w_small, w_big = POOL_WINDOWS[2 * half], POOL_WINDOWS[2 * half + 1]
        s_small = window_sum(-(w_small // 2), w_small // 2 - 1, half)
        s_big = (s_small + window_sum(-(w_big // 2), -(w_small // 2) - 1, half)
                 + window_sum(w_small // 2, w_big // 2 - 1, half))
        reach = jnp.where(low_group, w_small // 2, w_big // 2)
        cnt = jnp.minimum(t + reach, seq) - jnp.maximum(t - reach, 0)
        mean = jnp.where(low_group, s_small, s_big) / cnt.astype(F32)
        pooled.append(mean - ext_ref[pl.ds(POOL_HALO + blk.start, rows), LANES * half:LANES * (half + 1)])
    pooled = jnp.concatenate(pooled, axis=1).astype(BF16)
    pool_y = (jnp.dot(pooled, pw_ref[...], preferred_element_type=F32) * ps_ref[...]).astype(BF16)
    y = (jnp.dot(pool_y, wo_ref[0:POOL_WIDTH, :], preferred_element_type=F32)
         + jnp.dot(attn_ref[0, blk, :], wo_ref[POOL_WIDTH:, :], preferred_element_type=F32))
    return x_ref[0, blk, :] + mod[:, 2 * d:3 * d] * y


def _ffn_rows(x, mod, g, w1_ref, w2_ref):
    d = x.shape[-1]
    h = _norm_mod(x, g, mod[:, 3 * d:4 * d], mod[:, 4 * d:5 * d]).astype(BF16)
    a = jnp.maximum(jnp.dot(h, w1_ref[...], preferred_element_type=F32), 0.0)
    a = (a * a).astype(BF16)
    y = jnp.dot(a, w2_ref[...], preferred_element_type=F32)
    return x + mod[:, 5 * d:6 * d] * y


def _ffn_blocks(x_block, o_ref, mod, g, w1_ref, w2_ref):
    tm = o_ref.shape[1]
    rows = min(ROWS_FFN, tm)
    for r in range(tm // rows):
        blk = slice(r * rows, (r + 1) * rows)
        o_ref[0, blk, :] = _ffn_rows(x_block(blk), mod, g, w1_ref, w2_ref)


def _ffn_kernel(x_ref, mod_ref, g_ref, w1_hbm, w2_hbm, o_ref,
                w1_ref, w1_stage, w1_sem, w2_ref, w2_stage, w2_sem, *, layer):
    @pl.when(_first_step())
    def _():
        _stage_weight(w1_hbm, layer, w1_ref, w1_stage, w1_sem)
        _stage_weight(w2_hbm, layer, w2_ref, w2_stage, w2_sem)

    _ffn_blocks(lambda blk: x_ref[0, blk, :], o_ref, _mod_row(mod_ref), g_ref[...], w1_ref, w2_ref)


def _merge_ffn_kernel(x_ref, mod_ref, pm_ref, pp_ref, pn_ref, attn_ref, pw_ref, ps_ref, wo_hbm,
                      g_ref, w1_hbm, w2_hbm, o_ref, ext_ref, x1_ref,
                      wo_ref, wo_stage, wo_sem, w1_ref, w1_stage, w1_sem, w2_ref, w2_stage, w2_sem, *,
                      seq, layer):
    @pl.when(_first_step())
    def _():
        _stage_weight(wo_hbm, layer // 2, wo_ref, wo_stage, wo_sem)
        _stage_weight(w1_hbm, layer, w1_ref, w1_stage, w1_sem)
        _stage_weight(w2_hbm, layer, w2_ref, w2_stage, w2_sem)

    mod = _mod_row(mod_ref)
    rows = x1_ref.shape[1]
    _fill_pool_halo(pm_ref, pp_ref, pn_ref, ext_ref)

    def merge(j):
        blk = slice(j * rows, (j + 1) * rows)
        x1_ref[j % 2] = _pool_merge_rows(blk, x_ref, mod, attn_ref, pw_ref, ps_ref, wo_ref, ext_ref, seq)

    def ffn(j):
        o_ref[0, j * rows:(j + 1) * rows, :] = _ffn_rows(x1_ref[j % 2], mod, g_ref[...], w1_ref, w2_ref)

    _software_pipeline(x_ref.shape[1] // rows, merge, ffn)


def _merge_ffn_call(x, mods, pool_in, attn, pool_w_bd, pool_scale, w_out, g, w1, w2, layer):
    b, s, d = x.shape
    tm = min(TM_FFN, s)
    hb = tm // POOL_HALO
    n_halo = s // POOL_HALO
    return pl.pallas_call(
        functools.partial(_merge_ffn_kernel, seq=s, layer=layer),
        grid=(b, s // tm),
        in_specs=[
            pl.BlockSpec((1, tm, d), lambda bi, i: (bi, i, 0)),
            _mod_spec(mods, layer),
            pl.BlockSpec((1, tm, POOL_WIDTH), lambda bi, i: (bi, i, 0)),
            pl.BlockSpec((1, POOL_HALO, POOL_WIDTH),
                         lambda bi, i: (bi, jnp.maximum(i * hb - 1, 0), 0)),
            pl.BlockSpec((1, POOL_HALO, POOL_WIDTH),
                         lambda bi, i: (bi, jnp.minimum((i + 1) * hb, n_halo - 1), 0)),
            pl.BlockSpec((1, tm, attn.shape[-1]), lambda bi, i: (bi, i, 0)),
            _const_spec(pool_w_bd.shape),
            _const_spec(pool_scale.shape),
            _HBM,
            _const_spec((1, d)),
            _HBM,
            _HBM,
        ],
        out_specs=pl.BlockSpec((1, tm, d), lambda bi, i: (bi, i, 0)),
        out_shape=jax.ShapeDtypeStruct((b, s, d), F32),
        scratch_shapes=[pltpu.VMEM((tm + 2 * POOL_HALO, POOL_WIDTH), F32),
                        pltpu.VMEM((2, min(ROWS_FFN, tm), d), F32), *_stage_scratch(*w_out.shape[1:]),
                        *_stage_scratch(*w1.shape[1:]), *_stage_scratch(*w2.shape[1:])],
        compiler_params=_params(("arbitrary", "arbitrary")),
        name="pool_merge_ffn",
    )(x, mods, pool_in, pool_in, pool_in, attn, pool_w_bd, pool_scale, w_out, g, w1, w2)


def _ffn_call(x, mods, g, w1, w2, layer):
    b, s, d = x.shape
    tm = min(TM_FFN_ALONE, s)
    return pl.pallas_call(
        functools.partial(_ffn_kernel, layer=layer),
        grid=(b, s // tm),
        in_specs=[
            pl.BlockSpec((1, tm, d), lambda bi, i: (bi, i, 0)),
            _mod_spec(mods, layer),
            _const_spec((1, d)),
            _HBM,
            _HBM,
        ],
        out_specs=pl.BlockSpec((1, tm, d), lambda bi, i: (bi, i, 0)),
        out_shape=jax.ShapeDtypeStruct((b, s, d), F32),
        scratch_shapes=[*_stage_scratch(*w1.shape[1:]), *_stage_scratch(*w2.shape[1:])],
        compiler_params=_params(("arbitrary", "arbitrary")),
        name="ffn",
    )(x, mods, g, w1, w2)


def _sgmlp_kernel(x_ref, mod_ref, g_ref, win_hbm, lng_ref, lnb_ref, sgw_ref, sgb_ref, wout_hbm,
                  o_ref, a_ref, win_ref, win_stage, win_sem, wout_ref, wout_stage, wout_sem):
    @pl.when(_first_step())
    def _():
        _stage_weight(win_hbm, 0, win_ref, win_stage, win_sem)
        _stage_weight(wout_hbm, 0, wout_ref, wout_stage, wout_sem)

    d = x_ref.shape[-1]
    rows = a_ref.shape[1]
    width = win_ref.shape[1] // 2
    mod = _mod_row(mod_ref)
    gain = g_ref[...] * (1.0 + mod[:, d:2 * d])
    shift = mod[:, 0:d]
    gate = mod[:, 2 * d:3 * d]

    def project(j):
        x = x_ref[0, j * rows:(j + 1) * rows, :]
        ms = jnp.mean(x * x, axis=-1, keepdims=True)
        h = (x * lax.rsqrt(ms + EPS) * gain + shift).astype(BF16)
        a_ref[j % 2] = jnp.dot(h, win_ref[...], preferred_element_type=F32)

    def gelu(t):
        return 0.5 * t * (1.0 + lax.erf(t * (1.0 / math.sqrt(2.0))))

    def epilogue(j):
        blk = slice(j * rows, (j + 1) * rows)
        a = a_ref.at[j % 2]
        v = gelu(a[:, width:])
        mu = jnp.mean(v, axis=-1, keepdims=True)
        vc = v - mu
        var = jnp.mean(vc * vc, axis=-1, keepdims=True)
        vn = (vc * lax.rsqrt(var + EPS) * lng_ref[...] + lnb_ref[...]).astype(BF16)
        z = []
        for gi in range(SG_GROUPS):
            cols = slice(gi * SG_GROUP_DIM, (gi + 1) * SG_GROUP_DIM)
            sgw = sgw_ref[gi].astype(BF16)
            mixed = jnp.concatenate(
                [jnp.dot(sgw, vn[r * CHUNK:(r + 1) * CHUNK, cols], preferred_element_type=F32)
                 + sgb_ref[:, cols] for r in range(rows // CHUNK)], axis=0)
            z.append((gelu(a[:, cols]) * mixed).astype(BF16))
        z = jnp.concatenate(z, axis=1)
        y = jnp.dot(z, wout_ref[...], preferred_element_type=F32)
        o_ref[0, blk, :] = x_ref[0, blk, :] + gate * y

    _software_pipeline(x_ref.shape[1] // rows, project, epilogue)


def _sgmlp_call(x, mods, layer, g, w_in, ln_g, ln_b, sg_w, sg_b_full, w_out):
    b, s, d = x.shape
    tm = min(TM_SG, s)
    return pl.pallas_call(
        _sgmlp_kernel,
        grid=(b, s // tm),
        in_specs=[
            pl.BlockSpec((1, tm, d), lambda bi, i: (bi, i, 0)),
            _mod_spec(mods, layer),
            _const_spec((1, d)),
            _HBM,
            _const_spec(ln_g.shape),
            _const_spec(ln_b.shape),
            _const_spec(sg_w.shape),
            _const_spec(sg_b_full.shape),
            _HBM,
        ],
        out_specs=pl.BlockSpec((1, tm, d), lambda bi, i: (bi, i, 0)),
        out_shape=jax.ShapeDtypeStruct((b, s, d), F32),
        scratch_shapes=[pltpu.VMEM((2, min(ROWS_SG, tm), w_in.shape[2]), F32),
                        *_stage_scratch(*w_in.shape[1:]), *_stage_scratch(*w_out.shape[1:])],
        compiler_params=_params(("arbitrary", "arbitrary")),
        name="sg_mlp",
    )(x, mods, g, w_in, ln_g, ln_b, sg_w, sg_b_full, w_out)


def _rope_tables(n_tok):
    t = np.arange(n_tok, dtype=np.int32)
    rows = (t // GRID_W).astype(np.float32)
    cols = (t % GRID_W).astype(np.float32)
    pairs = HEAD_DIM // 4
    inv = (np.float32(ROPE_THETA) ** (-np.arange(pairs, dtype=np.float32) / np.float32(pairs))).astype(np.float32)
    ang = np.concatenate([rows[:, None] * inv, cols[:, None] * inv], axis=-1)
    cos, sin = np.cos(ang), np.sin(ang)
    cos2 = np.repeat(cos, 2, axis=-1)
    sin2 = np.stack([-sin, sin], axis=-1).reshape(n_tok, HEAD_DIM)
    return (jnp.asarray(np.tile(cos2, (1, LANES // HEAD_DIM)), F32),
            jnp.asarray(np.tile(sin2, (1, LANES // HEAD_DIM)), F32))


def _block_diag(blocks):
    n = len(blocks)
    rows = []
    for i, blk in enumerate(blocks):
        z = jnp.zeros_like(blk)
        rows.append(jnp.concatenate([blk if j == i else z for j in range(n)], axis=1))
    return jnp.concatenate(rows, axis=0)


def kernel(x, c, ctx, c_ctx, ada_w, ada_b, norm_mix_g, norm_ffn_g, ffn_w1, ffn_w2, ev_w_in, ev_w_out,
           pool_w, pool_scale, q_norm_g, k_norm_g, lam_q1, lam_k1, lam_q2, lam_k2, sub_norm_g,
           od_w_in, sg_ln_g, sg_ln_b, sg_w, sg_b, od_w_out):
    b, s, d = x.shape
    depth = ada_w.shape[0]
    assert depth == 2 and b <= SUBLANES - 1

    mods = _ada_call(c, c_ctx, ada_w, ada_b)

    qg = q_norm_g[0][None, :]
    kg = k_norm_g[0][None, :]
    averaging = jnp.full((HEAD_DIM, HEAD_DIM), 1.0 / HEAD_DIM, BF16)
    bd = _block_diag([averaging] * (MXU_TILE // HEAD_DIM))
    cos, sin_signed = _rope_tables(s)
    ctx_len = ctx.shape[1]
    cos_ctx = jnp.ones((ctx_len, LANES), F32)
    sin_ctx = jnp.zeros((ctx_len, LANES), F32)
    g_mix0 = norm_mix_g[0][None, :]

    pool_in, q, k, vt = _inproj_call(x, mods, 0, None, g_mix0, ev_w_in, bd, qg, kg, cos, sin_signed, True)
    kc, vct = _inproj_call(ctx, mods, 0, b, g_mix0, ev_w_in, bd, qg, kg, cos_ctx, sin_ctx, False)

    lam_init = 0.8 - 0.6 * math.exp(-0.3 * 0)
    lam_rows = jnp.pad(jnp.stack([lam_q1[0], lam_k1[0], lam_q2[0], lam_k2[0]]),
                       ((0, SUBLANES - 4), (0, LANES - HEAD_DIM)))
    attn = _attn_call(_score_bound(qg * Q_GAIN_SCALE, kg), lam_rows, sub_norm_g[0][None, :], q, k, vt, kc, vct,
                      lam_init)

    pool_w_bd = _block_diag([pool_w[0, gi] for gi in range(len(POOL_WINDOWS))]).astype(BF16)
    x = _merge_ffn_call(x, mods, pool_in, attn, pool_w_bd, pool_scale[0][None, :],
                        ev_w_out, norm_ffn_g[0][None, :], ffn_w1, ffn_w2, 0)

    sg_b_full = jnp.repeat(sg_b[0].T, SG_GROUP_DIM, axis=1)
    x = _sgmlp_call(x, mods, 1, norm_mix_g[1][None, :], od_w_in,
                    sg_ln_g[0][None, :], sg_ln_b[0][None, :], sg_w[0], sg_b_full, od_w_out)
    x = _ffn_call(x, mods, norm_ffn_g[1][None, :], ffn_w1, ffn_w2, 1)
    return x
```

```python
import functools
import math

import jax
import jax.numpy as jnp
import numpy as np
from jax import lax
from jax.experimental import pallas as pl
from jax.experimental.pallas import tpu as pltpu

F32 = jnp.float32
BF16 = jnp.bfloat16

EPS = 1e-6
GRID_W = 64
ROPE_THETA = 10000.0
POOL_WINDOWS = (2, 4, 8, 16)
POOL_GROUP = 64
POOL_WIDTH = 256
HEAD_DIM = 64
V_DIM = 128
N_HEADS = 6
QK_WIDTH = N_HEADS * 2 * HEAD_DIM
Q_GAIN_SCALE = HEAD_DIM ** -0.5 * math.log2(math.e)
CHUNK = 128
SG_GROUPS = 8
SG_GROUP_DIM = 256

LANES = 128
SUBLANES = 8
MXU_TILE = 256
POOL_HALO = 8
VMEM_LIMIT = 56 * 1024 * 1024
STAGE_BYTES = 2 * 1024 * 1024

MAX_UNSHIFTED_SCORE = 60.0

ADA_COLS = 1536
TM_PROJ = 1024
ROWS_PROJ = 256
TQ = 512
TILES_PER_STEP = 2
ATTN_KEYS = 1024
TM_FFN = 512
TM_FFN_ALONE = 1024
ROWS_FFN = 256
TM_SG = 512
ROWS_SG = 256


def _params(sem):
    return pltpu.CompilerParams(dimension_semantics=sem, vmem_limit_bytes=VMEM_LIMIT)


def _const_spec(shape):
    zeros = (0,) * len(shape)
    return pl.BlockSpec(shape, lambda *_: zeros, pipeline_mode=pl.Buffered(1))


def _norm_mod(x, g, shift, scale):
    ms = jnp.mean(x * x, axis=-1, keepdims=True)
    return x * lax.rsqrt(ms + EPS) * (g * (1.0 + scale)) + shift


def _mod_row(mod_ref, row=None):
    if row is None:
        row = pl.program_id(0)
    return mod_ref[0, pl.ds(row, 1), :]


def _mod_spec(mods, layer):
    return pl.BlockSpec((1,) + mods.shape[1:], lambda *_: (layer, 0, 0))


def _software_pipeline(n, produce, consume):
    produce(0)
    for j in range(n):
        if j + 1 < n:
            produce(j + 1)
        consume(j)


def _first_step():
    return jnp.logical_and(pl.program_id(0) == 0, pl.program_id(1) == 0)


def _stage_weight(w_hbm, layer, dst_ref, stage_ref, sem, col0=0):
    k, n = dst_ref.shape
    slab = stage_ref.shape[1]
    n_slabs = k // slab

    def copy(i):
        return pltpu.make_async_copy(w_hbm.at[layer, pl.ds(i * slab, slab), pl.ds(col0, n)],
                                     stage_ref.at[i % 2], sem.at[i % 2])

    copy(0).start()
    for i in range(n_slabs):
        if i + 1 < n_slabs:
            copy(i + 1).start()
        copy(i).wait()
        dst_ref[i * slab:(i + 1) * slab, :] = stage_ref[i % 2].astype(BF16)


def _stage_scratch(k, n):
    slab = max(r for r in range(SUBLANES, k + 1, SUBLANES) if k % r == 0 and (r == SUBLANES or 4 * r * n <= STAGE_BYTES))
    return [pltpu.VMEM((k, n), BF16), pltpu.VMEM((2, slab, n), F32), pltpu.SemaphoreType.DMA((2,))]


_HBM = pl.BlockSpec(memory_space=pl.ANY)


def _ada_kernel(c_ref, cctx_ref, w_ref, b_ref, o_ref):
    pad = jnp.zeros((SUBLANES - c_ref.shape[0] - 1, c_ref.shape[1]), F32)
    c = jnp.concatenate([c_ref[...], cctx_ref[...], pad], axis=0)
    sc = (c * jax.nn.sigmoid(c)).astype(BF16)
    o_ref[0] = jnp.dot(sc, w_ref[0].astype(BF16), preferred_element_type=F32) + b_ref[0]


def _ada_call(c, c_ctx, ada_w, ada_b):
    depth, d, n = ada_w.shape
    tn = ADA_COLS
    return pl.pallas_call(
        _ada_kernel,
        grid=(depth, n // tn),
        in_specs=[
            pl.BlockSpec(c.shape, lambda l, j: (0, 0)),
            pl.BlockSpec((1, d), lambda l, j: (0, 0)),
            pl.BlockSpec((1, d, tn), lambda l, j: (l, 0, j)),
            pl.BlockSpec((1, 1, tn), lambda l, j: (l, 0, j)),
        ],
        out_specs=pl.BlockSpec((1, SUBLANES, tn), lambda l, j: (l, 0, j)),
        out_shape=jax.ShapeDtypeStruct((depth, SUBLANES, n), F32),
        compiler_params=_params(("parallel", "parallel")),
        name="ada_ln",
    )(c, c_ctx.reshape(1, d), ada_w, ada_b.reshape(depth, 1, n))


def _qk_norm_rope(y_ref, ms_ref, off, ms_off, g, cos, sin_signed, even_lane):
    outs = []
    for j in range(QK_WIDTH // LANES):
        ms = ms_ref[:, ms_off + LANES * j:ms_off + LANES * (j + 1)]
        xc = y_ref[:, off + LANES * j:off + LANES * (j + 1)] * lax.rsqrt(ms + EPS) * g
        partner = jnp.where(even_lane, pltpu.roll(xc, LANES - 1, 1), pltpu.roll(xc, 1, 1))
        outs.append(xc * cos + partner * sin_signed)
    return jnp.concatenate(outs, axis=1)


def _inproj_kernel(x_ref, mod_ref, g_ref, w_hbm, bd_ref, qg_ref, kg_ref, cos_ref, sin_ref,
                   *refs, with_q, mod_row):
    *out_refs, y_ref, ss_ref, w_ref, w_stage, w_sem = refs

    @pl.when(_first_step())
    def _():
        _stage_weight(w_hbm, 0, w_ref, w_stage, w_sem, col0=0 if with_q else POOL_WIDTH + QK_WIDTH)

    d = x_ref.shape[-1]
    rows = y_ref.shape[1]
    mod = _mod_row(mod_ref, mod_row)
    gain = g_ref[...] * (1.0 + mod[:, d:2 * d])
    shift = mod[:, 0:d]
    lane = lax.broadcasted_iota(jnp.int32, (rows, LANES), 1)
    even_lane = (lane % 2) == 0
    q_gain = jnp.concatenate([qg_ref[...]] * (LANES // HEAD_DIM), axis=1) * Q_GAIN_SCALE
    k_gain = jnp.concatenate([kg_ref[...]] * (LANES // HEAD_DIM), axis=1)
    qk_off = POOL_WIDTH if with_q else 0
    qk_cols = ss_ref.shape[2]

    def project(j):
        x = x_ref[0, j * rows:(j + 1) * rows, :]
        ms = jnp.mean(x * x, axis=-1, keepdims=True)
        h = (x * lax.rsqrt(ms + EPS) * gain + shift).astype(BF16)
        y_ref[j % 2] = jnp.dot(h, w_ref[...], preferred_element_type=F32)
        for p in range(0, qk_cols, MXU_TILE):
            t = y_ref[j % 2, :, qk_off + p:qk_off + p + MXU_TILE]
            ss_ref[j % 2, :, p:p + MXU_TILE] = jnp.dot((t * t).astype(BF16), bd_ref[...],
                                                       preferred_element_type=F32)

    def epilogue(j):
        blk = slice(j * rows, (j + 1) * rows)
        y = y_ref.at[j % 2]
        ss = ss_ref.at[j % 2]
        cos = cos_ref[blk, :]
        sin_signed = sin_ref[blk, :]
        if with_q:
            pool_ref, q_ref, k_ref, vt_ref = out_refs
            pool_ref[0, blk, :] = y[:, :POOL_WIDTH]
            q_ref[0, 0, :, blk] = _qk_norm_rope(y, ss, qk_off, 0, q_gain, cos, sin_signed,
                                                even_lane).T.astype(BF16)
            k_at = QK_WIDTH
        else:
            k_ref, vt_ref = out_refs
            k_at = 0
        k_ref[0, blk, :] = _qk_norm_rope(y, ss, qk_off + k_at, k_at, k_gain, cos, sin_signed,
                                         even_lane).astype(BF16)
        vt_ref[0, 0, :, blk] = y[:, qk_off + k_at + QK_WIDTH:].T.astype(BF16)

    _software_pipeline(x_ref.shape[1] // rows, project, epilogue)


def _inproj_call(x, mods, layer, mod_row, g, w, bd, qg, kg, cos, sin_signed, with_q):
    b, s, d = x.shape
    tm = min(TM_PROJ, s)
    nt = s // tm
    dv = N_HEADS * V_DIM
    n = w.shape[2] if with_q else QK_WIDTH + dv
    out_shape = [jax.ShapeDtypeStruct((b, s, QK_WIDTH), BF16),
                 jax.ShapeDtypeStruct((b, nt, dv, tm), BF16)]
    out_specs = [pl.BlockSpec((1, tm, QK_WIDTH), lambda bi, i: (bi, i, 0)),
                 pl.BlockSpec((1, 1, dv, tm), lambda bi, i: (bi, i, 0, 0))]
    if with_q:
        out_shape = [jax.ShapeDtypeStruct((b, s, POOL_WIDTH), F32),
                     jax.ShapeDtypeStruct((b, nt, QK_WIDTH, tm), BF16)] + out_shape
        out_specs = [pl.BlockSpec((1, tm, POOL_WIDTH), lambda bi, i: (bi, i, 0)),
                     pl.BlockSpec((1, 1, QK_WIDTH, tm), lambda bi, i: (bi, i, 0, 0))] + out_specs
    return pl.pallas_call(
        functools.partial(_inproj_kernel, with_q=with_q, mod_row=mod_row),
        grid=(b, nt),
        in_specs=[
            pl.BlockSpec((1, tm, d), lambda bi, i: (bi, i, 0)),
            _mod_spec(mods, layer),
            _const_spec((1, d)),
            _HBM,
            _const_spec(bd.shape),
            _const_spec(qg.shape),
            _const_spec(kg.shape),
            pl.BlockSpec((tm, LANES), lambda bi, i: (i, 0)),
            pl.BlockSpec((tm, LANES), lambda bi, i: (i, 0)),
        ],
        out_specs=out_specs,
        out_shape=out_shape,
        scratch_shapes=[pltpu.VMEM((2, min(ROWS_PROJ, tm), n), F32),
                        pltpu.VMEM((2, min(ROWS_PROJ, tm), n - dv - (POOL_WIDTH if with_q else 0)), F32),
                        *_stage_scratch(d, n)],
        compiler_params=_params(("arbitrary", "arbitrary")),
        name="inproj_q" if with_q else "inproj_ctx",
    )(x, mods, g, w, bd, qg, kg, cos, sin_signed)


def _attn_kernel(bound_ref, lam_ref, subg_ref, q_ref, k_ref, vt_ref, kc_ref, vct_ref, o_ref,
                 st_ref, l_ref, acc_ref, *, lam_init):
    tq = q_ref.shape[-1] // TILES_PER_STEP
    n_chunks = vt_ref.shape[1]
    tk = vt_ref.shape[-1]

    @pl.when(pl.program_id(0) == 0)
    def _():
        l_ref[...] = jnp.ones(l_ref.shape, F32)
        acc_ref[...] = jnp.zeros(acc_ref.shape, F32)

    bounded = bound_ref[0] <= MAX_UNSHIFTED_SCORE

    def stacked_queries(tile):
        qt = q_ref[0, 0, :, tile * tq:(tile + 1) * tq]
        row = lax.broadcasted_iota(jnp.int32, qt.shape, 0)
        zero = jnp.zeros_like(qt)
        return jnp.concatenate([jnp.where(row < HEAD_DIM, qt, zero),
                                jnp.where(row >= HEAD_DIM, qt, zero)], axis=1)

    def chunks():
        for c in range(n_chunks):
            for j in range(0, tk, ATTN_KEYS):
                yield k_ref[0, c * tk + j:c * tk + j + ATTN_KEYS, :], vt_ref[0, c, :, j:j + ATTN_KEYS]
        for j in range(0, kc_ref.shape[1], ATTN_KEYS):
            yield kc_ref[0, j:j + ATTN_KEYS, :], vct_ref[0, 0, :, j:j + ATTN_KEYS]

    def scores(kb, qq):
        return jnp.dot(kb, qq, preferred_element_type=F32)

    def write_previous_step():
        lam_rows = lam_ref[...]
        a1 = jnp.sum(lam_rows[0:1] * lam_rows[1:2], axis=1, keepdims=True)
        a2 = jnp.sum(lam_rows[2:3] * lam_rows[3:4], axis=1, keepdims=True)
        lam = jnp.exp(a1) - jnp.exp(a2) + lam_init
        for tile in range(TILES_PER_STEP):
            l = l_ref[tile]
            acc = acc_ref[tile]
            ot = acc[:, :tq] / l[:, :tq] - lam * (acc[:, tq:] / l[:, tq:])
            o = ot.T
            ms = jnp.mean(o * o, axis=-1, keepdims=True)
            o_ref[0, tile * tq:(tile + 1) * tq, :] = (
                o * lax.rsqrt(ms + EPS) * (subg_ref[...] * (1.0 - lam_init))).astype(BF16)

    @pl.when(bounded)
    def _():
        work = list(chunks())
        write_previous_step()
        for tile in range(TILES_PER_STEP):
            l = jnp.zeros((1, 2 * tq), F32)
            acc = jnp.zeros((V_DIM, 2 * tq), F32)
            qq = stacked_queries(tile)
            slot = 2 * tile
            st_ref[slot, :work[0][0].shape[0], :] = scores(work[0][0], qq)
            for c, (kb, vtb) in enumerate(work):
                if c + 1 < len(work):
                    nxt = work[c + 1][0]
                    st_ref[slot + (c + 1) % 2, :nxt.shape[0], :] = scores(nxt, qq)
                p = jnp.exp2(st_ref[slot + c % 2, :kb.shape[0], :])
                l = l + jnp.sum(p, axis=0, keepdims=True)
                acc = acc + jnp.dot(vtb, p.astype(BF16), preferred_element_type=F32)
            l_ref[tile] = l
            acc_ref[tile] = acc

    @pl.when(jnp.logical_not(bounded))
    def _():
        write_previous_step()
        for tile in range(TILES_PER_STEP):
            m = jnp.full((1, 2 * tq), -jnp.inf, F32)
            l = jnp.zeros((1, 2 * tq), F32)
            acc = jnp.zeros((V_DIM, 2 * tq), F32)
            qq = stacked_queries(tile)
            for kb, vtb in chunks():
                st = scores(kb, qq)
                m_new = jnp.maximum(m, jnp.max(st, axis=0, keepdims=True))
                alpha = jnp.exp2(m - m_new)
                p = jnp.exp2(st - m_new)
                l = alpha * l + jnp.sum(p, axis=0, keepdims=True)
                acc = alpha * acc + jnp.dot(vtb, p.astype(BF16), preferred_element_type=F32)
                m = m_new
            l_ref[tile] = l
            acc_ref[tile] = acc


def _score_bound(q_gain, k_gain):
    slack = 1.0 + 2.0 ** -5
    return (HEAD_DIM * slack * jnp.max(jnp.abs(q_gain)) * jnp.max(jnp.abs(k_gain))).reshape(1)


def _attn_call(score_bound, lam_rows, sub_g, qt, k, vt, kc, vct, lam_init):
    b, s, _ = k.shape
    tq = min(TQ, s // TILES_PER_STEP)
    tqs = TILES_PER_STEP * tq
    n_chunks, tk = vt.shape[1], vt.shape[3]
    ctx_len = kc.shape[1]
    nq = s // tqs
    n_tiles = b * N_HEADS * nq

    def tile(t):
        t = jnp.minimum(t, n_tiles - 1)
        return t // (N_HEADS * nq), (t // nq) % N_HEADS, t % nq

    per_chunk = qt.shape[3] // tqs

    def q_map(t):
        bi, h, i = tile(t)
        return bi, i // per_chunk, h, i % per_chunk

    def out_map(t):
        bi, h, i = tile(jnp.maximum(t - 1, 0))
        return bi, i, h

    def kv_map(t):
        bi, h, _ = tile(t)
        return bi, 0, h

    def vt_map(t):
        bi, h, _ = tile(t)
        return bi, 0, h, 0

    return pl.pallas_call(
        functools.partial(_attn_kernel, lam_init=lam_init),
        grid=(n_tiles + 1,),
        in_specs=[
            pl.BlockSpec(memory_space=pltpu.SMEM),
            _const_spec(lam_rows.shape),
            _const_spec(sub_g.shape),
            pl.BlockSpec((1, 1, V_DIM, tqs), q_map),
            pl.BlockSpec((1, s, V_DIM), kv_map),
            pl.BlockSpec((1, n_chunks, V_DIM, tk), vt_map),
            pl.BlockSpec((1, ctx_len, V_DIM), kv_map),
            pl.BlockSpec((1, 1, V_DIM, ctx_len), vt_map),
        ],
        out_specs=pl.BlockSpec((1, tqs, V_DIM), out_map),
        out_shape=jax.ShapeDtypeStruct((b, s, N_HEADS * V_DIM), BF16),
        scratch_shapes=[pltpu.VMEM((2 * TILES_PER_STEP, ATTN_KEYS, 2 * tq), F32),
                        pltpu.VMEM((TILES_PER_STEP, 1, 2 * tq), F32),
                        pltpu.VMEM((TILES_PER_STEP, V_DIM, 2 * tq), F32)],
        compiler_params=_params(("arbitrary",)),
        name="diff_attn",
    )(score_bound, lam_rows, sub_g, qt, k, vt, kc, vct)


def _fill_pool_halo(pm_ref, pp_ref, pn_ref, ext_ref):
    tm = pm_ref.shape[1]
    i = pl.program_id(1)
    last = pl.num_programs(1) - 1
    ext_ref[0:POOL_HALO, :] = jnp.where(i > 0, pp_ref[0], 0.0)
    ext_ref[POOL_HALO:POOL_HALO + tm, :] = pm_ref[0]
    ext_ref[POOL_HALO + tm:2 * POOL_HALO + tm, :] = jnp.where(i < last, pn_ref[0], 0.0)


def _pool_merge_rows(blk, x_ref, mod, attn_ref, pw_ref, ps_ref, wo_ref, ext_ref, seq):
    d = x_ref.shape[-1]
    tm = x_ref.shape[1]
    rows = blk.stop - blk.start

    def window_sum(lo, hi, half):
        acc = None
        for off in range(lo, hi + 1):
            piece = ext_ref[pl.ds(POOL_HALO + blk.start + off, rows), LANES * half:LANES * (half + 1)]
            acc = piece if acc is None else acc + piece
        return acc

    t = pl.program_id(1) * tm + blk.start + lax.broadcasted_iota(jnp.int32, (rows, LANES), 0)
    lane = lax.broadcasted_iota(jnp.int32, (rows, LANES), 1)
    low_group = lane < POOL_GROUP
    pooled = []
    for half in range(2):
        w_small, w_big = POOL_WINDOWS[2 * half], POOL_WINDOWS[2 * half + 1]
        s_small = window_sum(-(w_small // 2), w_small // 2 - 1, half)
        s_big = (s_small + window_sum(-(w_big // 2), -(w_small // 2) - 1, half)
                 + window_sum(w_small // 2, w_big // 2 - 1, half))
        reach = jnp.where(low_group, w_small // 2, w_big // 2)
        cnt = jnp.minimum(t + reach, seq) - jnp.maximum(t - reach, 0)
        mean = jnp.where(low_group, s_small, s_big) / cnt.astype(F32)
        pooled.append(mean - ext_ref[pl.ds(POOL_HALO + blk.start, rows), LANES * half:LANES * (half + 1)])
    pooled = jnp.concatenate(pooled, axis=1).astype(BF16)
    pool_y = (jnp.dot(pooled, pw_ref[...], preferred_element_type=F32) * ps_ref[...]).astype(BF16)
    y = (jnp.dot(pool_y, wo_ref[0:POOL_WIDTH, :], preferred_element_type=F32)
         + jnp.dot(attn_ref[0, blk, :], wo_ref[POOL_WIDTH:, :], preferred_element_type=F32))
    return x_ref[0, blk, :] + mod[:, 2 * d:3 * d] * y


def _ffn_rows(x, mod, g, w1_ref, w2_ref):
    d = x.shape[-1]
    h = _norm_mod(x, g, mod[:, 3 * d:4 * d], mod[:, 4 * d:5 * d]).astype(BF16)
    a = jnp.maximum(jnp.dot(h, w1_ref[...], preferred_element_type=F32), 0.0)
    a = (a * a).astype(BF16)
    y = jnp.dot(a, w2_ref[...], preferred_element_type=F32)
    return x + mod[:, 5 * d:6 * d] * y


def _ffn_blocks(x_block, o_ref, mod, g, w1_ref, w2_ref):
    tm = o_ref.shape[1]
    rows = min(ROWS_FFN, tm)
    for r in range(tm // rows):
        blk = slice(r * rows, (r + 1) * rows)
        o_ref[0, blk, :] = _ffn_rows(x_block(blk), mod, g, w1_ref, w2_ref)


def _ffn_kernel(x_ref, mod_ref, g_ref, w1_hbm, w2_hbm, o_ref,
                w1_ref, w1_stage, w1_sem, w2_ref, w2_stage, w2_sem, *, layer):
    @pl.when(_first_step())
    def _():
        _stage_weight(w1_hbm, layer, w1_ref, w1_stage, w1_sem)
        _stage_weight(w2_hbm, layer, w2_ref, w2_stage, w2_sem)

    _ffn_blocks(lambda blk: x_ref[0, blk, :], o_ref, _mod_row(mod_ref), g_ref[...], w1_ref, w2_ref)


def _merge_ffn_kernel(x_ref, mod_ref, pm_ref, pp_ref, pn_ref, attn_ref, pw_ref, ps_ref, wo_hbm,
                      g_ref, w1_hbm, w2_hbm, o_ref, ext_ref, x1_ref,
                      wo_ref, wo_stage, wo_sem, w1_ref, w1_stage, w1_sem, w2_ref, w2_stage, w2_sem, *,
                      seq, layer):
    @pl.when(_first_step())
    def _():
        _stage_weight(wo_hbm, layer // 2, wo_ref, wo_stage, wo_sem)
        _stage_weight(w1_hbm, layer, w1_ref, w1_stage, w1_sem)
        _stage_weight(w2_hbm, layer, w2_ref, w2_stage, w2_sem)

    mod = _mod_row(mod_ref)
    rows = x1_ref.shape[1]
    _fill_pool_halo(pm_ref, pp_ref, pn_ref, ext_ref)

    def merge(j):
        blk = slice(j * rows, (j + 1) * rows)
        x1_ref[j % 2] = _pool_merge_rows(blk, x_ref, mod, attn_ref, pw_ref, ps_ref, wo_ref, ext_ref, seq)

    def ffn(j):
        o_ref[0, j * rows:(j + 1) * rows, :] = _ffn_rows(x1_ref[j % 2], mod, g_ref[...], w1_ref, w2_ref)

    _software_pipeline(x_ref.shape[1] // rows, merge, ffn)


def _merge_ffn_call(x, mods, pool_in, attn, pool_w_bd, pool_scale, w_out, g, w1, w2, layer):
    b, s, d = x.shape
    tm = min(TM_FFN, s)
    hb = tm // POOL_HALO
    n_halo = s // POOL_HALO
    return pl.pallas_call(
        functools.partial(_merge_ffn_kernel, seq=s, layer=layer),
        grid=(b, s // tm),
        in_specs=[
            pl.BlockSpec((1, tm, d), lambda bi, i: (bi, i, 0)),
            _mod_spec(mods, layer),
            pl.BlockSpec((1, tm, POOL_WIDTH), lambda bi, i: (bi, i, 0)),
            pl.BlockSpec((1, POOL_HALO, POOL_WIDTH),
                         lambda bi, i: (bi, jnp.maximum(i * hb - 1, 0), 0)),
            pl.BlockSpec((1, POOL_HALO, POOL_WIDTH),
                         lambda bi, i: (bi, jnp.minimum((i + 1) * hb, n_halo - 1), 0)),
            pl.BlockSpec((1, tm, attn.shape[-1]), lambda bi, i: (bi, i, 0)),
            _const_spec(pool_w_bd.shape),
            _const_spec(pool_scale.shape),
            _HBM,
            _const_spec((1, d)),
            _HBM,
            _HBM,
        ],
        out_specs=pl.BlockSpec((1, tm, d), lambda bi, i: (bi, i, 0)),
        out_shape=jax.ShapeDtypeStruct((b, s, d), F32),
        scratch_shapes=[pltpu.VMEM((tm + 2 * POOL_HALO, POOL_WIDTH), F32),
                        pltpu.VMEM((2, min(ROWS_FFN, tm), d), F32), *_stage_scratch(*w_out.shape[1:]),
                        *_stage_scratch(*w1.shape[1:]), *_stage_scratch(*w2.shape[1:])],
        compiler_params=_params(("arbitrary", "arbitrary")),
        name="pool_merge_ffn",
    )(x, mods, pool_in, pool_in, pool_in, attn, pool_w_bd, pool_scale, w_out, g, w1, w2)


def _ffn_call(x, mods, g, w1, w2, layer):
    b, s, d = x.shape
    tm = min(TM_FFN_ALONE, s)
    return pl.pallas_call(
        functools.partial(_ffn_kernel, layer=layer),
        grid=(b, s // tm),
        in_specs=[
            pl.BlockSpec((1, tm, d), lambda bi, i: (bi, i, 0)),
            _mod_spec(mods, layer),
            _const_spec((1, d)),
            _HBM,
            _HBM,
        ],
        out_specs=pl.BlockSpec((1, tm, d), lambda bi, i: (bi, i, 0)),
        out_shape=jax.ShapeDtypeStruct((b, s, d), F32),
        scratch_shapes=[*_stage_scratch(*w1.shape[1:]), *_stage_scratch(*w2.shape[1:])],
        compiler_params=_params(("arbitrary", "arbitrary")),
        name="ffn",
    )(x, mods, g, w1, w2)


def _sgmlp_kernel(x_ref, mod_ref, g_ref, win_hbm, lng_ref, lnb_ref, sgw_ref, sgb_ref, wout_hbm,
                  o_ref, a_ref, win_ref, win_stage, win_sem, wout_ref, wout_stage, wout_sem):
    @pl.when(_first_step())
    def _():
        _stage_weight(win_hbm, 0, win_ref, win_stage, win_sem)
        _stage_weight(wout_hbm, 0, wout_ref, wout_stage, wout_sem)

    d = x_ref.shape[-1]
    rows = a_ref.shape[1]
    width = win_ref.shape[1] // 2
    mod = _mod_row(mod_ref)
    gain = g_ref[...] * (1.0 + mod[:, d:2 * d])
    shift = mod[:, 0:d]
    gate = mod[:, 2 * d:3 * d]

    def project(j):
        x = x_ref[0, j * rows:(j + 1) * rows, :]
        ms = jnp.mean(x * x, axis=-1, keepdims=True)
        h = (x * lax.rsqrt(ms + EPS) * gain + shift).astype(BF16)
        a_ref[j % 2] = jnp.dot(h, win_ref[...], preferred_element_type=F32)

    def gelu(t):
        return 0.5 * t * (1.0 + lax.erf(t * (1.0 / math.sqrt(2.0))))

    def epilogue(j):
        blk = slice(j * rows, (j + 1) * rows)
        a = a_ref.at[j % 2]
        v = gelu(a[:, width:])
        mu = jnp.mean(v, axis=-1, keepdims=True)
        vc = v - mu
        var = jnp.mean(vc * vc, axis=-1, keepdims=True)
        vn = (vc * lax.rsqrt(var + EPS) * lng_ref[...] + lnb_ref[...]).astype(BF16)
        z = []
        for gi in range(SG_GROUPS):
            cols = slice(gi * SG_GROUP_DIM, (gi + 1) * SG_GROUP_DIM)
            sgw = sgw_ref[gi].astype(BF16)
            mixed = jnp.concatenate(
                [jnp.dot(sgw, vn[r * CHUNK:(r + 1) * CHUNK, cols], preferred_element_type=F32)
                 + sgb_ref[:, cols] for r in range(rows // CHUNK)], axis=0)
            z.append((gelu(a[:, cols]) * mixed).astype(BF16))
        z = jnp.concatenate(z, axis=1)
        y = jnp.dot(z, wout_ref[...], preferred_element_type=F32)
        o_ref[0, blk, :] = x_ref[0, blk, :] + gate * y

    _software_pipeline(x_ref.shape[1] // rows, project, epilogue)


def _sgmlp_call(x, mods, layer, g, w_in, ln_g, ln_b, sg_w, sg_b_full, w_out):
    b, s, d = x.shape
    tm = min(TM_SG, s)
    return pl.pallas_call(
        _sgmlp_kernel,
        grid=(b, s // tm),
        in_specs=[
            pl.BlockSpec((1, tm, d), lambda bi, i: (bi, i, 0)),
            _mod_spec(mods, layer),
            _const_spec((1, d)),
            _HBM,
            _const_spec(ln_g.shape),
            _const_spec(ln_b.shape),
            _const_spec(sg_w.shape),
            _const_spec(sg_b_full.shape),
            _HBM,
        ],
        out_specs=pl.BlockSpec((1, tm, d), lambda bi, i: (bi, i, 0)),
        out_shape=jax.ShapeDtypeStruct((b, s, d), F32),
        scratch_shapes=[pltpu.VMEM((2, min(ROWS_SG, tm), w_in.shape[2]), F32),
                        *_stage_scratch(*w_in.shape[1:]), *_stage_scratch(*w_out.shape[1:])],
        compiler_params=_params(("arbitrary", "arbitrary")),
        name="sg_mlp",
    )(x, mods, g, w_in, ln_g, ln_b, sg_w, sg_b_full, w_out)


def _rope_tables(n_tok):
    t = np.arange(n_tok, dtype=np.int32)
    rows = (t // GRID_W).astype(np.float32)
    cols = (t % GRID_W).astype(np.float32)
    pairs = HEAD_DIM // 4
    inv = (np.float32(ROPE_THETA) ** (-np.arange(pairs, dtype=np.float32) / np.float32(pairs))).astype(np.float32)
    ang = np.concatenate([rows[:, None] * inv, cols[:, None] * inv], axis=-1)
    cos, sin = np.cos(ang), np.sin(ang)
    cos2 = np.repeat(cos, 2, axis=-1)
    sin2 = np.stack([-sin, sin], axis=-1).reshape(n_tok, HEAD_DIM)
    return (jnp.asarray(np.tile(cos2, (1, LANES // HEAD_DIM)), F32),
            jnp.asarray(np.tile(sin2, (1, LANES // HEAD_DIM)), F32))


def _block_diag(blocks):
    n = len(blocks)
    rows = []
    for i, blk in enumerate(blocks):
        z = jnp.zeros_like(blk)
        rows.append(jnp.concatenate([blk if j == i else z for j in range(n)], axis=1))
    return jnp.concatenate(rows, axis=0)


def kernel(x, c, ctx, c_ctx, ada_w, ada_b, norm_mix_g, norm_ffn_g, ffn_w1, ffn_w2, ev_w_in, ev_w_out,
           pool_w, pool_scale, q_norm_g, k_norm_g, lam_q1, lam_k1, lam_q2, lam_k2, sub_norm_g,
           od_w_in, sg_ln_g, sg_ln_b, sg_w, sg_b, od_w_out):
    b, s, d = x.shape
    depth = ada_w.shape[0]
    assert depth == 2 and b <= SUBLANES - 1

    mods = _ada_call(c, c_ctx, ada_w, ada_b)

    qg = q_norm_g[0][None, :]
    kg = k_norm_g[0][None, :]
    averaging = jnp.full((HEAD_DIM, HEAD_DIM), 1.0 / HEAD_DIM, BF16)
    bd = _block_diag([averaging] * (MXU_TILE // HEAD_DIM))
    cos, sin_signed = _rope_tables(s)
    ctx_len = ctx.shape[1]
    cos_ctx = jnp.ones((ctx_len, LANES), F32)
    sin_ctx = jnp.zeros((ctx_len, LANES), F32)
    g_mix0 = norm_mix_g[0][None, :]

    pool_in, q, k, vt = _inproj_call(x, mods, 0, None, g_mix0, ev_w_in, bd, qg, kg, cos, sin_signed, True)
    kc, vct = _inproj_call(ctx, mods, 0, b, g_mix0, ev_w_in, bd, qg, kg, cos_ctx, sin_ctx, False)

    lam_init = 0.8 - 0.6 * math.exp(-0.3 * 0)
    lam_rows = jnp.pad(jnp.stack([lam_q1[0], lam_k1[0], lam_q2[0], lam_k2[0]]),
                       ((0, SUBLANES - 4), (0, LANES - HEAD_DIM)))
    attn = _attn_call(_score_bound(qg * Q_GAIN_SCALE, kg), lam_rows, sub_norm_g[0][None, :], q, k, vt, kc, vct,
                      lam_init)

    pool_w_bd = _block_diag([pool_w[0, gi] for gi in range(len(POOL_WINDOWS))]).astype(BF16)
    x = _merge_ffn_call(x, mods, pool_in, attn, pool_w_bd, pool_scale[0][None, :],
                        ev_w_out, norm_ffn_g[0][None, :], ffn_w1, ffn_w2, 0)

    sg_b_full = jnp.repeat(sg_b[0].T, SG_GROUP_DIM, axis=1)
    x = _sgmlp_call(x, mods, 1, norm_mix_g[1][None, :], od_w_in,
                    sg_ln_g[0][None, :], sg_ln_b[0][None, :], sg_w[0], sg_b_full, od_w_out)
    x = _ffn_call(x, mods, norm_ffn_g[1][None, :], ffn_w1, ffn_w2, 1)
    return x
```

```python
import functools
import math

import jax
import jax.numpy as jnp
import numpy as np
from jax import lax
from jax.experimental import pallas as pl
from jax.experimental.pallas import tpu as pltpu

F32 = jnp.float32
BF16 = jnp.bfloat16

EPS = 1e-6
GRID_W = 64
ROPE_THETA = 10000.0
POOL_WINDOWS = (2, 4, 8, 16)
POOL_GROUP = 64
POOL_WIDTH = 256
HEAD_DIM = 64
V_DIM = 128
N_HEADS = 6
QK_WIDTH = N_HEADS * 2 * HEAD_DIM
Q_GAIN_SCALE = HEAD_DIM ** -0.5 * math.log2(math.e)
CHUNK = 128
SG_GROUPS = 8
SG_GROUP_DIM = 256

LANES = 128
SUBLANES = 8
MXU_TILE = 256
POOL_HALO = 8
VMEM_LIMIT = 56 * 1024 * 1024
STAGE_BYTES = 2 * 1024 * 1024

MAX_UNSHIFTED_SCORE = 60.0

ADA_COLS = 1536
TM_PROJ = 1024
ROWS_PROJ = 256
TQ = 512
TILES_PER_STEP = 2
ATTN_KEYS = 1024
TM_FFN = 512
TM_FFN_ALONE = 1024
ROWS_FFN = 256
TM_SG = 512
ROWS_SG = 256


def _params(sem):
    return pltpu.CompilerParams(dimension_semantics=sem, vmem_limit_bytes=VMEM_LIMIT)


def _const_spec(shape):
    zeros = (0,) * len(shape)
    return pl.BlockSpec(shape, lambda *_: zeros, pipeline_mode=pl.Buffered(1))


def _norm_mod(x, g, shift, scale):
    ms = jnp.mean(x * x, axis=-1, keepdims=True)
    return x * lax.rsqrt(ms + EPS) * (g * (1.0 + scale)) + shift


def _mod_row(mod_ref, row=None):
    if row is None:
        row = pl.program_id(0)
    return mod_ref[0, pl.ds(row, 1), :]


def _mod_spec(mods, layer):
    return pl.BlockSpec((1,) + mods.shape[1:], lambda *_: (layer, 0, 0))


def _software_pipeline(n, produce, consume):
    produce(0)
    for j in range(n):
        if j + 1 < n:
            produce(j + 1)
        consume(j)


def _first_step():
    return jnp.logical_and(pl.program_id(0) == 0, pl.program_id(1) == 0)


def _stage_weight(w_hbm, layer, dst_ref, stage_ref, sem, col0=0):
    k, n = dst_ref.shape
    slab = stage_ref.shape[1]
    n_slabs = k // slab

    def copy(i):
        return pltpu.make_async_copy(w_hbm.at[layer, pl.ds(i * slab, slab), pl.ds(col0, n)],
                                     stage_ref.at[i % 2], sem.at[i % 2])

    copy(0).start()
    for i in range(n_slabs):
        if i + 1 < n_slabs:
            copy(i + 1).start()
        copy(i).wait()
        dst_ref[i * slab:(i + 1) * slab, :] = stage_ref[i % 2].astype(BF16)


def _stage_scratch(k, n):
    slab = max(r for r in range(SUBLANES, k + 1, SUBLANES) if k % r == 0 and (r == SUBLANES or 4 * r * n <= STAGE_BYTES))
    return [pltpu.VMEM((k, n), BF16), pltpu.VMEM((2, slab, n), F32), pltpu.SemaphoreType.DMA((2,))]


_HBM = pl.BlockSpec(memory_space=pl.ANY)


def _ada_kernel(c_ref, cctx_ref, w_ref, b_ref, o_ref):
    pad = jnp.zeros((SUBLANES - c_ref.shape[0] - 1, c_ref.shape[1]), F32)
    c = jnp.concatenate([c_ref[...], cctx_ref[...], pad], axis=0)
    sc = (c * jax.nn.sigmoid(c)).astype(BF16)
    o_ref[0] = jnp.dot(sc, w_ref[0].astype(BF16), preferred_element_type=F32) + b_ref[0]


def _ada_call(c, c_ctx, ada_w, ada_b):
    depth, d, n = ada_w.shape
    tn = ADA_COLS
    return pl.pallas_call(
        _ada_kernel,
        grid=(depth, n // tn),
        in_specs=[
            pl.BlockSpec(c.shape, lambda l, j: (0, 0)),
            pl.BlockSpec((1, d), lambda l, j: (0, 0)),
            pl.BlockSpec((1, d, tn), lambda l, j: (l, 0, j)),
            pl.BlockSpec((1, 1, tn), lambda l, j: (l, 0, j)),
        ],
        out_specs=pl.BlockSpec((1, SUBLANES, tn), lambda l, j: (l, 0, j)),
        out_shape=jax.ShapeDtypeStruct((depth, SUBLANES, n), F32),
        compiler_params=_params(("parallel", "parallel")),
        name="ada_ln",
    )(c, c_ctx.reshape(1, d), ada_w, ada_b.reshape(depth, 1, n))


def _qk_norm_rope(y_ref, ms_ref, off, ms_off, g, cos, sin_signed, even_lane):
    outs = []
    for j in range(QK_WIDTH // LANES):
        ms = ms_ref[:, ms_off + LANES * j:ms_off + LANES * (j + 1)]
        xc = y_ref[:, off + LANES * j:off + LANES * (j + 1)] * lax.rsqrt(ms + EPS) * g
        partner = jnp.where(even_lane, pltpu.roll(xc, LANES - 1, 1), pltpu.roll(xc, 1, 1))
        outs.append(xc * cos + partner * sin_signed)
    return jnp.concatenate(outs, axis=1)


def _inproj_kernel(x_ref, mod_ref, g_ref, w_hbm, bd_ref, qg_ref, kg_ref, cos_ref, sin_ref,
                   *refs, with_q, mod_row):
    *out_refs, y_ref, ss_ref, w_ref, w_stage, w_sem = refs

    @pl.when(_first_step())
    def _():
        _stage_weight(w_hbm, 0, w_ref, w_stage, w_sem, col0=0 if with_q else POOL_WIDTH + QK_WIDTH)

    d = x_ref.shape[-1]
    rows = y_ref.shape[1]
    mod = _mod_row(mod_ref, mod_row)
    gain = g_ref[...] * (1.0 + mod[:, d:2 * d])
    shift = mod[:, 0:d]
    lane = lax.broadcasted_iota(jnp.int32, (rows, LANES), 1)
    even_lane = (lane % 2) == 0
    q_gain = jnp.concatenate([qg_ref[...]] * (LANES // HEAD_DIM), axis=1) * Q_GAIN_SCALE
    k_gain = jnp.concatenate([kg_ref[...]] * (LANES // HEAD_DIM), axis=1)
    qk_off = POOL_WIDTH if with_q else 0
    qk_cols = ss_ref.shape[2]

    def project(j):
        x = x_ref[0, j * rows:(j + 1) * rows, :]
        ms = jnp.mean(x * x, axis=-1, keepdims=True)
        h = (x * lax.rsqrt(ms + EPS) * gain + shift).astype(BF16)
        y_ref[j % 2] = jnp.dot(h, w_ref[...], preferred_element_type=F32)
        for p in range(0, qk_cols, MXU_TILE):
            t = y_ref[j % 2, :, qk_off + p:qk_off + p + MXU_TILE]
            ss_ref[j % 2, :, p:p + MXU_TILE] = jnp.dot((t * t).astype(BF16), bd_ref[...],
                                                       preferred_element_type=F32)

    def epilogue(j):
        blk = slice(j * rows, (j + 1) * rows)
        y = y_ref.at[j % 2]
        ss = ss_ref.at[j % 2]
        cos = cos_ref[blk, :]
        sin_signed = sin_ref[blk, :]
        if with_q:
            pool_ref, q_ref, k_ref, vt_ref = out_refs
            pool_ref[0, blk, :] = y[:, :POOL_WIDTH]
            q_ref[0, blk, :] = _qk_norm_rope(y, ss, qk_off, 0, q_gain, cos, sin_signed,
                                             even_lane).astype(BF16)
            k_at = QK_WIDTH
        else:
            k_ref, vt_ref = out_refs
            k_at = 0
        k_ref[0, blk, :] = _qk_norm_rope(y, ss, qk_off + k_at, k_at, k_gain, cos, sin_signed,
                                         even_lane).astype(BF16)
        vt_ref[0, 0, :, blk] = y[:, qk_off + k_at + QK_WIDTH:].T.astype(BF16)

    _software_pipeline(x_ref.shape[1] // rows, project, epilogue)


def _inproj_call(x, mods, layer, mod_row, g, w, bd, qg, kg, cos, sin_signed, with_q):
    b, s, d = x.shape
    tm = min(TM_PROJ, s)
    nt = s // tm
    dv = N_HEADS * V_DIM
    n = w.shape[2] if with_q else QK_WIDTH + dv
    out_shape = [jax.ShapeDtypeStruct((b, s, QK_WIDTH), BF16),
                 jax.ShapeDtypeStruct((b, nt, dv, tm), BF16)]
    out_specs = [pl.BlockSpec((1, tm, QK_WIDTH), lambda bi, i: (bi, i, 0)),
                 pl.BlockSpec((1, 1, dv, tm), lambda bi, i: (bi, i, 0, 0))]
    if with_q:
        out_shape = [jax.ShapeDtypeStruct((b, s, POOL_WIDTH), F32),
                     jax.ShapeDtypeStruct((b, s, QK_WIDTH), BF16)] + out_shape
        out_specs = [pl.BlockSpec((1, tm, POOL_WIDTH), lambda bi, i: (bi, i, 0)),
                     pl.BlockSpec((1, tm, QK_WIDTH), lambda bi, i: (bi, i, 0))] + out_specs
    return pl.pallas_call(
        functools.partial(_inproj_kernel, with_q=with_q, mod_row=mod_row),
        grid=(b, nt),
        in_specs=[
            pl.BlockSpec((1, tm, d), lambda bi, i: (bi, i, 0)),
            _mod_spec(mods, layer),
            _const_spec((1, d)),
            _HBM,
            _const_spec(bd.shape),
            _const_spec(qg.shape),
            _const_spec(kg.shape),
            pl.BlockSpec((tm, LANES), lambda bi, i: (i, 0)),
            pl.BlockSpec((tm, LANES), lambda bi, i: (i, 0)),
        ],
        out_specs=out_specs,
        out_shape=out_shape,
        scratch_shapes=[pltpu.VMEM((2, min(ROWS_PROJ, tm), n), F32),
                        pltpu.VMEM((2, min(ROWS_PROJ, tm), n - dv - (POOL_WIDTH if with_q else 0)), F32),
                        *_stage_scratch(d, n)],
        compiler_params=_params(("arbitrary", "arbitrary")),
        name="inproj_q" if with_q else "inproj_ctx",
    )(x, mods, g, w, bd, qg, kg, cos, sin_signed)


def _attn_kernel(bound_ref, lam_ref, subg_ref, q_ref, k_ref, vt_ref, kc_ref, vct_ref, o_ref,
                 st_ref, l_ref, acc_ref, *, lam_init):
    tq = q_ref.shape[1] // TILES_PER_STEP
    n_chunks = vt_ref.shape[1]
    tk = vt_ref.shape[-1]

    @pl.when(pl.program_id(0) == 0)
    def _():
        l_ref[...] = jnp.ones(l_ref.shape, F32)
        acc_ref[...] = jnp.zeros(acc_ref.shape, F32)

    bounded = bound_ref[0] <= MAX_UNSHIFTED_SCORE

    def stacked_queries(tile):
        q = q_ref[0, tile * tq:(tile + 1) * tq, :]
        lane = lax.broadcasted_iota(jnp.int32, q.shape, 1)
        zero = jnp.zeros_like(q)
        return jnp.concatenate([jnp.where(lane < HEAD_DIM, q, zero),
                                jnp.where(lane >= HEAD_DIM, q, zero)], axis=0)

    def chunks():
        for c in range(n_chunks):
            for j in range(0, tk, ATTN_KEYS):
                yield k_ref[0, c * tk + j:c * tk + j + ATTN_KEYS, :], vt_ref[0, c, :, j:j + ATTN_KEYS]
        for j in range(0, kc_ref.shape[1], ATTN_KEYS):
            yield kc_ref[0, j:j + ATTN_KEYS, :], vct_ref[0, 0, :, j:j + ATTN_KEYS]

    def scores(kb, qq):
        return lax.dot_general(kb, qq, (((1,), (1,)), ((), ())), preferred_element_type=F32)

    def write_previous_step():
        lam_rows = lam_ref[...]
        a1 = jnp.sum(lam_rows[0:1] * lam_rows[1:2], axis=1, keepdims=True)
        a2 = jnp.sum(lam_rows[2:3] * lam_rows[3:4], axis=1, keepdims=True)
        lam = jnp.exp(a1) - jnp.exp(a2) + lam_init
        for tile in range(TILES_PER_STEP):
            l = l_ref[tile]
            acc = acc_ref[tile]
            ot = acc[:, :tq] / l[:, :tq] - lam * (acc[:, tq:] / l[:, tq:])
            o = ot.T
            ms = jnp.mean(o * o, axis=-1, keepdims=True)
            o_ref[0, tile * tq:(tile + 1) * tq, :] = (
                o * lax.rsqrt(ms + EPS) * (subg_ref[...] * (1.0 - lam_init))).astype(BF16)

    @pl.when(bounded)
    def _():
        work = list(chunks())
        write_previous_step()
        for tile in range(TILES_PER_STEP):
            l = jnp.zeros((1, 2 * tq), F32)
            acc = jnp.zeros((V_DIM, 2 * tq), F32)
            qq = stacked_queries(tile)
            slot = 2 * tile
            st_ref[slot, :work[0][0].shape[0], :] = scores(work[0][0], qq)
            for c, (kb, vtb) in enumerate(work):
                if c + 1 < len(work):
                    nxt = work[c + 1][0]
                    st_ref[slot + (c + 1) % 2, :nxt.shape[0], :] = scores(nxt, qq)
                p = jnp.exp2(st_ref[slot + c % 2, :kb.shape[0], :])
                l = l + jnp.sum(p, axis=0, keepdims=True)
                acc = acc + jnp.dot(vtb, p.astype(BF16), preferred_element_type=F32)
            l_ref[tile] = l
            acc_ref[tile] = acc

    @pl.when(jnp.logical_not(bounded))
    def _():
        write_previous_step()
        for tile in range(TILES_PER_STEP):
            m = jnp.full((1, 2 * tq), -jnp.inf, F32)
            l = jnp.zeros((1, 2 * tq), F32)
            acc = jnp.zeros((V_DIM, 2 * tq), F32)
            qq = stacked_queries(tile)
            for kb, vtb in chunks():
                st = scores(kb, qq)
                m_new = jnp.maximum(m, jnp.max(st, axis=0, keepdims=True))
                alpha = jnp.exp2(m - m_new)
                p = jnp.exp2(st - m_new)
                l = alpha * l + jnp.sum(p, axis=0, keepdims=True)
                acc = alpha * acc + jnp.dot(vtb, p.astype(BF16), preferred_element_type=F32)
                m = m_new
            l_ref[tile] = l
            acc_ref[tile] = acc


def _score_bound(q_gain, k_gain):
    slack = 1.0 + 2.0 ** -5
    return (HEAD_DIM * slack * jnp.max(jnp.abs(q_gain)) * jnp.max(jnp.abs(k_gain))).reshape(1)


def _attn_call(score_bound, lam_rows, sub_g, q, k, vt, kc, vct, lam_init):
    b, s, _ = q.shape
    tq = min(TQ, s // TILES_PER_STEP)
    tqs = TILES_PER_STEP * tq
    n_chunks, tk = vt.shape[1], vt.shape[3]
    ctx_len = kc.shape[1]
    nq = s // tqs
    n_tiles = b * N_HEADS * nq

    def tile(t):
        t = jnp.minimum(t, n_tiles - 1)
        return t // (N_HEADS * nq), (t // nq) % N_HEADS, t % nq

    def q_map(t):
        bi, h, i = tile(t)
        return bi, i, h

    def out_map(t):
        bi, h, i = tile(jnp.maximum(t - 1, 0))
        return bi, i, h

    def kv_map(t):
        bi, h, _ = tile(t)
        return bi, 0, h

    def vt_map(t):
        bi, h, _ = tile(t)
        return bi, 0, h, 0

    return pl.pallas_call(
        functools.partial(_attn_kernel, lam_init=lam_init),
        grid=(n_tiles + 1,),
        in_specs=[
            pl.BlockSpec(memory_space=pltpu.SMEM),
            _const_spec(lam_rows.shape),
            _const_spec(sub_g.shape),
            pl.BlockSpec((1, tqs, V_DIM), q_map),
            pl.BlockSpec((1, s, V_DIM), kv_map),
            pl.BlockSpec((1, n_chunks, V_DIM, tk), vt_map),
            pl.BlockSpec((1, ctx_len, V_DIM), kv_map),
            pl.BlockSpec((1, 1, V_DIM, ctx_len), vt_map),
        ],
        out_specs=pl.BlockSpec((1, tqs, V_DIM), out_map),
        out_shape=jax.ShapeDtypeStruct((b, s, N_HEADS * V_DIM), BF16),
        scratch_shapes=[pltpu.VMEM((2 * TILES_PER_STEP, ATTN_KEYS, 2 * tq), F32),
                        pltpu.VMEM((TILES_PER_STEP, 1, 2 * tq), F32),
                        pltpu.VMEM((TILES_PER_STEP, V_DIM, 2 * tq), F32)],
        compiler_params=_params(("arbitrary",)),
        name="diff_attn",
    )(score_bound, lam_rows, sub_g, q, k, vt, kc, vct)


def _fill_pool_halo(pm_ref, pp_ref, pn_ref, ext_ref):
    tm = pm_ref.shape[1]
    i = pl.program_id(1)
    last = pl.num_programs(1) - 1
    ext_ref[0:POOL_HALO, :] = jnp.where(i > 0, pp_ref[0], 0.0)
    ext_ref[POOL_HALO:POOL_HALO + tm, :] = pm_ref[0]
    ext_ref[POOL_HALO + tm:2 * POOL_HALO + tm, :] = jnp.where(i < last, pn_ref[0], 0.0)


def _pool_merge_rows(blk, x_ref, mod, attn_ref, pw_ref, ps_ref, wo_ref, ext_ref, seq):
    d = x_ref.shape[-1]
    tm = x_ref.shape[1]
    rows = blk.stop - blk.start

    def window_sum(lo, hi, half):
        acc = None
        for off in range(lo, hi + 1):
            piece = ext_ref[pl.ds(POOL_HALO + blk.start + off, rows), LANES * half:LANES * (half + 1)]
            acc = piece if acc is None else acc + piece
        return acc

    t = pl.program_id(1) * tm + blk.start + lax.broadcasted_iota(jnp.int32, (rows, LANES), 0)
    lane = lax.broadcasted_iota(jnp.int32, (rows, LANES), 1)
    low_group = lane < POOL_GROUP
    pooled = []
    for half in range(2):
        w_small, w_big = POOL_WINDOWS[2 * half], POOL_WINDOWS[2 * half + 1]
        s_small = window_sum(-(w_small // 2), w_small // 2 - 1, half)
        s_big = (s_small + window_sum(-(w_big // 2), -(w_small // 2) - 1, half)
                 + window_sum(w_small // 2, w_big // 2 - 1, half))
        reach = jnp.where(low_group, w_small // 2, w_big // 2)
        cnt = jnp.minimum(t + reach, seq) - jnp.maximum(t - reach, 0)
        mean = jnp.where(low_group, s_small, s_big) / cnt.astype(F32)
        pooled.append(mean - ext_ref[pl.ds(POOL_HALO + blk.start, rows), LANES * half:LANES * (half + 1)])
    pooled = jnp.concatenate(pooled, axis=1).astype(BF16)
    pool_y = (jnp.dot(pooled, pw_ref[...], preferred_element_type=F32) * ps_ref[...]).astype(BF16)
    y = (jnp.dot(pool_y, wo_ref[0:POOL_WIDTH, :], preferred_element_type=F32)
         + jnp.dot(attn_ref[0, blk, :], wo_ref[POOL_WIDTH:, :], preferred_element_type=F32))
    return x_ref[0, blk, :] + mod[:, 2 * d:3 * d] * y


def _ffn_rows(x, mod, g, w1_ref, w2_ref):
    d = x.shape[-1]
    h = _norm_mod(x, g, mod[:, 3 * d:4 * d], mod[:, 4 * d:5 * d]).astype(BF16)
    a = jnp.maximum(jnp.dot(h, w1_ref[...], preferred_element_type=F32), 0.0)
    a = (a * a).astype(BF16)
    y = jnp.dot(a, w2_ref[...], preferred_element_type=F32)
    return x + mod[:, 5 * d:6 * d] * y


def _ffn_blocks(x_block, o_ref, mod, g, w1_ref, w2_ref):
    tm = o_ref.shape[1]
    rows = min(ROWS_FFN, tm)
    for r in range(tm // rows):
        blk = slice(r * rows, (r + 1) * rows)
        o_ref[0, blk, :] = _ffn_rows(x_block(blk), mod, g, w1_ref, w2_ref)


def _ffn_kernel(x_ref, mod_ref, g_ref, w1_hbm, w2_hbm, o_ref,
                w1_ref, w1_stage, w1_sem, w2_ref, w2_stage, w2_sem, *, layer):
    @pl.when(_first_step())
    def _():
        _stage_weight(w1_hbm, layer, w1_ref, w1_stage, w1_sem)
        _stage_weight(w2_hbm, layer, w2_ref, w2_stage, w2_sem)

    _ffn_blocks(lambda blk: x_ref[0, blk, :], o_ref, _mod_row(mod_ref), g_ref[...], w1_ref, w2_ref)


def _merge_ffn_kernel(x_ref, mod_ref, pm_ref, pp_ref, pn_ref, attn_ref, pw_ref, ps_ref, wo_hbm,
                      g_ref, w1_hbm, w2_hbm, o_ref, ext_ref, x1_ref,
                      wo_ref, wo_stage, wo_sem, w1_ref, w1_stage, w1_sem, w2_ref, w2_stage, w2_sem, *,
                      seq, layer):
    @pl.when(_first_step())
    def _():
        _stage_weight(wo_hbm, layer // 2, wo_ref, wo_stage, wo_sem)
        _stage_weight(w1_hbm, layer, w1_ref, w1_stage, w1_sem)
        _stage_weight(w2_hbm, layer, w2_ref, w2_stage, w2_sem)

    mod = _mod_row(mod_ref)
    rows = x1_ref.shape[1]
    _fill_pool_halo(pm_ref, pp_ref, pn_ref, ext_ref)

    def merge(j):
        blk = slice(j * rows, (j + 1) * rows)
        x1_ref[j % 2] = _pool_merge_rows(blk, x_ref, mod, attn_ref, pw_ref, ps_ref, wo_ref, ext_ref, seq)

    def ffn(j):
        o_ref[0, j * rows:(j + 1) * rows, :] = _ffn_rows(x1_ref[j % 2], mod, g_ref[...], w1_ref, w2_ref)

    _software_pipeline(x_ref.shape[1] // rows, merge, ffn)


def _merge_ffn_call(x, mods, pool_in, attn, pool_w_bd, pool_scale, w_out, g, w1, w2, layer):
    b, s, d = x.shape
    tm = min(TM_FFN, s)
    hb = tm // POOL_HALO
    n_halo = s // POOL_HALO
    return pl.pallas_call(
        functools.partial(_merge_ffn_kernel, seq=s, layer=layer),
        grid=(b, s // tm),
        in_specs=[
            pl.BlockSpec((1, tm, d), lambda bi, i: (bi, i, 0)),
            _mod_spec(mods, layer),
            pl.BlockSpec((1, tm, POOL_WIDTH), lambda bi, i: (bi, i, 0)),
            pl.BlockSpec((1, POOL_HALO, POOL_WIDTH),
                         lambda bi, i: (bi, jnp.maximum(i * hb - 1, 0), 0)),
            pl.BlockSpec((1, POOL_HALO, POOL_WIDTH),
                         lambda bi, i: (bi, jnp.minimum((i + 1) * hb, n_halo - 1), 0)),
            pl.BlockSpec((1, tm, attn.shape[-1]), lambda bi, i: (bi, i, 0)),
            _const_spec(pool_w_bd.shape),
            _const_spec(pool_scale.shape),
            _HBM,
            _const_spec((1, d)),
            _HBM,
            _HBM,
        ],
        out_specs=pl.BlockSpec((1, tm, d), lambda bi, i: (bi, i, 0)),
        out_shape=jax.ShapeDtypeStruct((b, s, d), F32),
        scratch_shapes=[pltpu.VMEM((tm + 2 * POOL_HALO, POOL_WIDTH), F32),
                        pltpu.VMEM((2, min(ROWS_FFN, tm), d), F32), *_stage_scratch(*w_out.shape[1:]),
                        *_stage_scratch(*w1.shape[1:]), *_stage_scratch(*w2.shape[1:])],
        compiler_params=_params(("arbitrary", "arbitrary")),
        name="pool_merge_ffn",
    )(x, mods, pool_in, pool_in, pool_in, attn, pool_w_bd, pool_scale, w_out, g, w1, w2)


def _ffn_call(x, mods, g, w1, w2, layer):
    b, s, d = x.shape
    tm = min(TM_FFN_ALONE, s)
    return pl.pallas_call(
        functools.partial(_ffn_kernel, layer=layer),
        grid=(b, s // tm),
        in_specs=[
            pl.BlockSpec((1, tm, d), lambda bi, i: (bi, i, 0)),
            _mod_spec(mods, layer),
            _const_spec((1, d)),
            _HBM,
            _HBM,
        ],
        out_specs=pl.BlockSpec((1, tm, d), lambda bi, i: (bi, i, 0)),
        out_shape=jax.ShapeDtypeStruct((b, s, d), F32),
        scratch_shapes=[*_stage_scratch(*w1.shape[1:]), *_stage_scratch(*w2.shape[1:])],
        compiler_params=_params(("arbitrary", "arbitrary")),
        name="ffn",
    )(x, mods, g, w1, w2)


def _sgmlp_kernel(x_ref, mod_ref, g_ref, win_hbm, lng_ref, lnb_ref, sgw_ref, sgb_ref, wout_hbm,
                  o_ref, a_ref, win_ref, win_stage, win_sem, wout_ref, wout_stage, wout_sem):
    @pl.when(_first_step())
    def _():
        _stage_weight(win_hbm, 0, win_ref, win_stage, win_sem)
        _stage_weight(wout_hbm, 0, wout_ref, wout_stage, wout_sem)

    d = x_ref.shape[-1]
    rows = a_ref.shape[1]
    width = win_ref.shape[1] // 2
    mod = _mod_row(mod_ref)
    gain = g_ref[...] * (1.0 + mod[:, d:2 * d])
    shift = mod[:, 0:d]
    gate = mod[:, 2 * d:3 * d]

    def project(j):
        x = x_ref[0, j * rows:(j + 1) * rows, :]
        ms = jnp.mean(x * x, axis=-1, keepdims=True)
        h = (x * lax.rsqrt(ms + EPS) * gain + shift).astype(BF16)
        a_ref[j % 2] = jnp.dot(h, win_ref[...], preferred_element_type=F32)

    def gelu(t):
        return 0.5 * t * (1.0 + lax.erf(t * (1.0 / math.sqrt(2.0))))

    def epilogue(j):
        blk = slice(j * rows, (j + 1) * rows)
        a = a_ref.at[j % 2]
        v = gelu(a[:, width:])
        mu = jnp.mean(v, axis=-1, keepdims=True)
        var = jnp.mean(v * v, axis=-1, keepdims=True) - mu * mu
        vn = ((v - mu) * lax.rsqrt(var + EPS) * lng_ref[...] + lnb_ref[...]).astype(BF16)
        z = []
        for gi in range(SG_GROUPS):
            cols = slice(gi * SG_GROUP_DIM, (gi + 1) * SG_GROUP_DIM)
            sgw = sgw_ref[gi].astype(BF16)
            mixed = jnp.concatenate(
                [jnp.dot(sgw, vn[r * CHUNK:(r + 1) * CHUNK, cols], preferred_element_type=F32)
                 + sgb_ref[:, cols] for r in range(rows // CHUNK)], axis=0)
            z.append((gelu(a[:, cols]) * mixed).astype(BF16))
        z = jnp.concatenate(z, axis=1)
        y = jnp.dot(z, wout_ref[...], preferred_element_type=F32)
        o_ref[0, blk, :] = x_ref[0, blk, :] + gate * y

    _software_pipeline(x_ref.shape[1] // rows, project, epilogue)


def _sgmlp_call(x, mods, layer, g, w_in, ln_g, ln_b, sg_w, sg_b_full, w_out):
    b, s, d = x.shape
    tm = min(TM_SG, s)
    return pl.pallas_call(
        _sgmlp_kernel,
        grid=(b, s // tm),
        in_specs=[
            pl.BlockSpec((1, tm, d), lambda bi, i: (bi, i, 0)),
            _mod_spec(mods, layer),
            _const_spec((1, d)),
            _HBM,
            _const_spec(ln_g.shape),
            _const_spec(ln_b.shape),
            _const_spec(sg_w.shape),
            _const_spec(sg_b_full.shape),
            _HBM,
        ],
        out_specs=pl.BlockSpec((1, tm, d), lambda bi, i: (bi, i, 0)),
        out_shape=jax.ShapeDtypeStruct((b, s, d), F32),
        scratch_shapes=[pltpu.VMEM((2, min(ROWS_SG, tm), w_in.shape[2]), F32),
                        *_stage_scratch(*w_in.shape[1:]), *_stage_scratch(*w_out.shape[1:])],
        compiler_params=_params(("arbitrary", "arbitrary")),
        name="sg_mlp",
    )(x, mods, g, w_in, ln_g, ln_b, sg_w, sg_b_full, w_out)


def _rope_tables(n_tok):
    t = np.arange(n_tok, dtype=np.int32)
    rows = (t // GRID_W).astype(np.float32)
    cols = (t % GRID_W).astype(np.float32)
    pairs = HEAD_DIM // 4
    inv = (np.float32(ROPE_THETA) ** (-np.arange(pairs, dtype=np.float32) / np.float32(pairs))).astype(np.float32)
    ang = np.concatenate([rows[:, None] * inv, cols[:, None] * inv], axis=-1)
    cos, sin = np.cos(ang), np.sin(ang)
    cos2 = np.repeat(cos, 2, axis=-1)
    sin2 = np.stack([-sin, sin], axis=-1).reshape(n_tok, HEAD_DIM)
    return (jnp.asarray(np.tile(cos2, (1, LANES // HEAD_DIM)), F32),
            jnp.asarray(np.tile(sin2, (1, LANES // HEAD_DIM)), F32))


def _block_diag(blocks):
    n = len(blocks)
    rows = []
    for i, blk in enumerate(blocks):
        z = jnp.zeros_like(blk)
        rows.append(jnp.concatenate([blk if j == i else z for j in range(n)], axis=1))
    return jnp.concatenate(rows, axis=0)


def kernel(x, c, ctx, c_ctx, ada_w, ada_b, norm_mix_g, norm_ffn_g, ffn_w1, ffn_w2, ev_w_in, ev_w_out,
           pool_w, pool_scale, q_norm_g, k_norm_g, lam_q1, lam_k1, lam_q2, lam_k2, sub_norm_g,
           od_w_in, sg_ln_g, sg_ln_b, sg_w, sg_b, od_w_out):
    b, s, d = x.shape
    depth = ada_w.shape[0]
    assert depth == 2 and b <= SUBLANES - 1

    mods = _ada_call(c, c_ctx, ada_w, ada_b)

    qg = q_norm_g[0][None, :]
    kg = k_norm_g[0][None, :]
    averaging = jnp.full((HEAD_DIM, HEAD_DIM), 1.0 / HEAD_DIM, BF16)
    bd = _block_diag([averaging] * (MXU_TILE // HEAD_DIM))
    cos, sin_signed = _rope_tables(s)
    ctx_len = ctx.shape[1]
    cos_ctx = jnp.ones((ctx_len, LANES), F32)
    sin_ctx = jnp.zeros((ctx_len, LANES), F32)
    g_mix0 = norm_mix_g[0][None, :]

    pool_in, q, k, vt = _inproj_call(x, mods, 0, None, g_mix0, ev_w_in, bd, qg, kg, cos, sin_signed, True)
    kc, vct = _inproj_call(ctx, mods, 0, b, g_mix0, ev_w_in, bd, qg, kg, cos_ctx, sin_ctx, False)

    lam_init = 0.8 - 0.6 * math.exp(-0.3 * 0)
    lam_rows = jnp.pad(jnp.stack([lam_q1[0], lam_k1[0], lam_q2[0], lam_k2[0]]),
                       ((0, SUBLANES - 4), (0, LANES - HEAD_DIM)))
    attn = _attn_call(_score_bound(qg * Q_GAIN_SCALE, kg), lam_rows, sub_norm_g[0][None, :], q, k, vt, kc, vct,
                      lam_init)

    pool_w_bd = _block_diag([pool_w[0, gi] for gi in range(len(POOL_WINDOWS))]).astype(BF16)
    x = _merge_ffn_call(x, mods, pool_in, attn, pool_w_bd, pool_scale[0][None, :],
                        ev_w_out, norm_ffn_g[0][None, :], ffn_w1, ffn_w2, 0)

    sg_b_full = jnp.repeat(sg_b[0].T, SG_GROUP_DIM, axis=1)
    x = _sgmlp_call(x, mods, 1, norm_mix_g[1][None, :], od_w_in,
                    sg_ln_g[0][None, :], sg_ln_b[0][None, :], sg_w[0], sg_b_full, od_w_out)
    x = _ffn_call(x, mods, norm_ffn_g[1][None, :], ffn_w1, ffn_w2, 1)
    return x
```
